```python
import math
import jax, jax.numpy as jnp
from jax import lax
import numpy as np

D_MODEL = 1024
BATCH = 8
SEQ = 2048
DEPTH = 2

PLE_DIM = 256
NORM_EPS = 1e-6
N_BRANCHES = 3
BRANCH_WIDTH = D_MODEL // 2

M_HEADS = 4
M_DK = BRANCH_WIDTH // M_HEADS
M_DV = BRANCH_WIDTH // M_HEADS
M_CONV = 4
M_CHUNK = 128
M_FBIAS_LO = 3.0
M_FBIAS_HI = 6.0

A_HEADS = 4
A_DV = BRANCH_WIDTH // A_HEADS
A_DQK = A_DV // 2
Q_BLOCK = 128
ALIBI_SLOPES = tuple(2.0 ** (-8.0 * (h + 1) / A_HEADS) for h in range(A_HEADS))

P_WINDOWS = (2, 4, 8, 16)
P_GROUPS = len(P_WINDOWS)
P_GC = BRANCH_WIDTH // P_GROUPS

D_FF = 2816
N_EXPERTS = 8
TOP_K = 2
D_FF_EXPERT = 3584
N_DENSE = (DEPTH + 1) // 2
N_MOE = DEPTH // 2

IN_SPLITS = (2 * M_HEADS * M_DK,
             M_HEADS * M_DV,
             BRANCH_WIDTH,
             M_HEADS,
             M_HEADS,
             A_HEADS * 2 * A_DQK,
             A_HEADS * 2 * A_DQK,
             A_HEADS * A_DV,
             BRANCH_WIDTH,
             N_BRANCHES * D_MODEL)
IN_WIDTH = sum(IN_SPLITS)

kernel_name = "hybrid_mlstm_diffattn_pool_moe_block"

F32 = jnp.float32


def rms_norm(x, gain):
    xf = x.astype(F32)
    y = xf * lax.rsqrt(jnp.mean(xf * xf, axis=-1, keepdims=True) + NORM_EPS)
    return (y * gain.astype(F32)).astype(x.dtype)


def causal_dwconv(x, w, b):
    C = x.shape[-1]
    y = lax.conv_general_dilated(x, w[:, None, :].astype(x.dtype), window_strides=(1,),
                                 padding=[(w.shape[0] - 1, 0)],
                                 dimension_numbers=('NWC', 'WIO', 'NWC'),
                                 feature_group_count=C)
    return y + b.astype(x.dtype)


def swiglu(h, w_gu, w_down):
    g, u = jnp.split(h @ w_gu, 2, axis=-1)
    return (jax.nn.silu(g) * u) @ w_down


def mlstm_chunkwise(q, k, v, i_pre, f_pre):
    B, H, S, DK = q.shape
    DV = v.shape[-1]
    L = M_CHUNK
    NC = S // L
    logf = jax.nn.log_sigmoid(f_pre)

    def to_chunks(a):
        return jnp.moveaxis(a.reshape(B, H, NC, L, *a.shape[3:]), 2, 0)

    xs = (to_chunks(q), to_chunks(k), to_chunks(v), to_chunks(i_pre), to_chunks(logf))
    causal = jnp.tril(jnp.ones((L, L), dtype=bool))

    def step(carry, xc):
        C, n, m = carry
        qx, kx, vx, ix, fx = xc
        b = jnp.cumsum(fx, axis=-1)
        logD = jnp.where(causal, b[..., :, None] - b[..., None, :] + ix[..., None, :], -jnp.inf)
        inter = b + m[..., None]
        m_t = jnp.maximum(jnp.max(logD, axis=-1), inter)
        Dw = jnp.exp(logD - m_t[..., None])
        w_inter = jnp.exp(inter - m_t)
        s = jnp.einsum('bhtd,bhsd->bhts', qx, kx) * Dw
        num = jnp.einsum('bhts,bhsv->bhtv', s, vx) + w_inter[..., None] * jnp.einsum('bhtd,bhdv->bhtv', qx, C)
        den = jnp.sum(s, axis=-1) + w_inter * jnp.einsum('bhtd,bhd->bht', qx, n)
        h = num / jnp.maximum(jnp.abs(den), jnp.exp(-m_t))[..., None]
        bL = b[..., -1]
        log_w = bL[..., None] - b + ix
        m_new = jnp.maximum(bL + m, jnp.max(log_w, axis=-1))
        decay = jnp.exp(bL + m - m_new)
        w = jnp.exp(log_w - m_new[..., None])
        C_new = decay[..., None, None] * C + jnp.einsum('bhs,bhsd,bhsv->bhdv', w, kx, vx)
        n_new = decay[..., None] * n + jnp.einsum('bhs,bhsd->bhd', w, kx)
        return (C_new, n_new, m_new), h

    init = (jnp.zeros((B, H, DK, DV), F32), jnp.zeros((B, H, DK), F32), jnp.zeros((B, H), F32))
    _, hs = lax.scan(step, init, xs)
    return jnp.moveaxis(hs, 0, 2).reshape(B, H, S, DV)


def mlstm_branch(qk_pre, v_pre, o_pre, i_pre, f_pre, conv_w, conv_b, gate_bias, head_gain):
    B, S, _ = qk_pre.shape
    qk = jax.nn.silu(causal_dwconv(qk_pre, conv_w, conv_b))
    q, k = jnp.split(qk, 2, axis=-1)

    def heads(a, d):
        return a.reshape(B, S, M_HEADS, d).transpose(0, 2, 1, 3).astype(F32)

    q = heads(q, M_DK)
    k = heads(k, M_DK) * (M_DK ** -0.5)
    v = heads(v_pre, M_DV)
    gb = gate_bias.astype(F32)
    ig = (i_pre.astype(F32) + gb[:M_HEADS]).transpose(0, 2, 1)
    fg = (f_pre.astype(F32) + gb[M_HEADS:]).transpose(0, 2, 1)
    h = mlstm_chunkwise(q, k, v, ig, fg).transpose(0, 2, 1, 3)
    h = rms_norm(h, head_gain.reshape(M_HEADS, M_DV)).reshape(B, S, BRANCH_WIDTH)
    h = h * jax.nn.sigmoid(o_pre.astype(F32))
    return h.astype(qk_pre.dtype)


def diff_attention_branch(q_pre, k_pre, v_pre, q_gain, k_gain, lam_params, head_gain, lam_init):
    B, S, _ = q_pre.shape
    dt = q_pre.dtype
    q = rms_norm(q_pre.reshape(B, S, A_HEADS, 2, A_DQK), q_gain).astype(F32) * (A_DQK ** -0.5)
    k = rms_norm(k_pre.reshape(B, S, A_HEADS, 2, A_DQK), k_gain).astype(F32)
    q = q.transpose(0, 2, 3, 1, 4)
    k = k.transpose(0, 2, 3, 1, 4)
    v = v_pre.reshape(B, S, A_HEADS, A_DV).transpose(0, 2, 1, 3).astype(F32)
    lp = lam_params.astype(F32)
    lam = jnp.exp(jnp.dot(lp[0], lp[1])) - jnp.exp(jnp.dot(lp[2], lp[3])) + lam_init
    slopes = jnp.asarray(ALIBI_SLOPES, F32)[:, None, None, None]
    outs = []
    for blk in range(S // Q_BLOCK):
        q0 = blk * Q_BLOCK
        q1 = q0 + Q_BLOCK
        s = jnp.einsum('bhmqd,bhmkd->bhmqk', q[:, :, :, q0:q1], k[:, :, :, :q1])
        dist = (jnp.arange(q0, q1)[:, None] - jnp.arange(q1)[None, :]).astype(F32)
        s = jnp.where(dist >= 0, s - slopes * dist, -jnp.inf)
        a = jax.nn.softmax(s, axis=-1)
        a = a[:, :, 0] - lam * a[:, :, 1]
        outs.append(jnp.einsum('bhqk,bhkd->bhqd', a, v[:, :, :q1]))
    o = jnp.concatenate(outs, axis=2).transpose(0, 2, 1, 3)
    o = rms_norm(o, head_gain.reshape(A_HEADS, A_DV)) * (1.0 - lam_init)
    return o.reshape(B, S, BRANCH_WIDTH).astype(dt)


def pool_branch(u, w_group, scale):
    B, S, _ = u.shape
    uf = u.astype(F32).reshape(B, S, P_GROUPS, P_GC)
    cs = jnp.pad(jnp.cumsum(uf, axis=1), ((0, 0), (1, 0), (0, 0), (0, 0)))
    t = jnp.arange(S)
    outs = []
    for g, w in enumerate(P_WINDOWS):
        lower = jnp.pad(cs[:, :S - w + 1, g], ((0, 0), (w - 1, 0), (0, 0)))
        cnt = jnp.minimum(t + 1, w).astype(F32)[None, :, None]
        outs.append((cs[:, 1:, g] - lower) / cnt)
    pooled = jnp.stack(outs, axis=2) - uf
    y = jnp.einsum('bsgc,gcd->bsgd', pooled, w_group.astype(F32)).reshape(B, S, BRANCH_WIDTH)
    return (y * scale.astype(F32)).astype(u.dtype)


def moe_swiglu(h, w_router, b_router, w_gu, w_down):
    logits = h.astype(F32) @ w_router.astype(F32) + b_router.astype(F32)
    top_val, top_idx = lax.top_k(logits, TOP_K)
    top_gate = jax.nn.softmax(top_val, axis=-1)
    combine = jnp.sum(jax.nn.one_hot(top_idx, N_EXPERTS, dtype=F32) * top_gate[..., None], axis=-2)
    out = jnp.zeros_like(h)
    for e in range(N_EXPERTS):
        out = out + combine[..., e:e + 1].astype(h.dtype) * swiglu(h, w_gu[e], w_down[e])
    return out


def setup_inputs(seed: int = 0) -> dict:
    key = jax.random.key(seed)
    ks = jax.random.split(key, 28)
    D = D_MODEL

    def nrm(i, shape, scale):
        return scale * jax.random.normal(ks[i], shape, F32)

    def gain(i, shape):
        return 1.0 + nrm(i, shape, 0.05)

    f_bias = jnp.linspace(M_FBIAS_LO, M_FBIAS_HI, M_HEADS, dtype=F32)
    m_gate_bias = jnp.concatenate([nrm(6, (DEPTH, M_HEADS), 0.1),
                                   f_bias[None, :] + nrm(7, (DEPTH, M_HEADS), 0.1)], axis=-1)
    return {
        "x": nrm(0, (BATCH, SEQ, D), 1.0),
        "p": nrm(1, (DEPTH, BATCH, SEQ, PLE_DIM), 1.0),
        "attn_norm": gain(2, (DEPTH, D)),
        "w_in": nrm(3, (DEPTH, D, IN_WIDTH), D ** -0.5),
        "m_conv_w": nrm(4, (DEPTH, M_CONV, 2 * M_HEADS * M_DK), M_CONV ** -0.5),
        "m_conv_b": nrm(5, (DEPTH, 2 * M_HEADS * M_DK), 0.02),
        "m_gate_bias": m_gate_bias,
        "m_head_norm": gain(8, (DEPTH, BRANCH_WIDTH)),
        "a_q_norm": gain(9, (DEPTH, A_DQK)),
        "a_k_norm": gain(10, (DEPTH, A_DQK)),
        "a_lambda": nrm(11, (DEPTH, 4, A_DQK), 0.1),
        "a_head_norm": gain(12, (DEPTH, BRANCH_WIDTH)),
        "pool_w": nrm(13, (DEPTH, P_GROUPS, P_GC, P_GC), P_GC ** -0.5),
        "pool_scale": 1.0 + nrm(14, (DEPTH, BRANCH_WIDTH), 0.1),
        "w_branch": nrm(15, (DEPTH, N_BRANCHES, BRANCH_WIDTH, D), BRANCH_WIDTH ** -0.5),
        "w_out": nrm(16, (DEPTH, D, D), D ** -0.5),
        "ffn_norm": gain(17, (DEPTH, D)),
        "dense_w_gu": nrm(18, (N_DENSE, D, 2 * D_FF), D ** -0.5),
        "dense_w_down": nrm(19, (N_DENSE, D_FF, D), D_FF ** -0.5),
        "router_w": nrm(20, (N_MOE, D, N_EXPERTS), D ** -0.5),
        "router_b": nrm(21, (N_MOE, N_EXPERTS), 0.01),
        "moe_w_gu": nrm(22, (N_MOE, N_EXPERTS, D, 2 * D_FF_EXPERT), D ** -0.5),
        "moe_w_down": nrm(23, (N_MOE, N_EXPERTS, D_FF_EXPERT, D), D_FF_EXPERT ** -0.5),
        "ple_norm": gain(24, (DEPTH, D)),
        "ple_w_gate": nrm(25, (DEPTH, D, D), D ** -0.5),
        "ple_w_proj": nrm(26, (DEPTH, PLE_DIM, D), PLE_DIM ** -0.5),
    }


def reference(x, p, attn_norm, w_in, m_conv_w, m_conv_b, m_gate_bias, m_head_norm,
              a_q_norm, a_k_norm, a_lambda, a_head_norm, pool_w, pool_scale,
              w_branch, w_out, ffn_norm, dense_w_gu, dense_w_down, router_w, router_b,
              moe_w_gu, moe_w_down, ple_norm, ple_w_gate, ple_w_proj):
    B, S, D = x.shape
    split_points = np.cumsum(np.array(IN_SPLITS))[:-1].tolist()
    for layer in range(DEPTH):
        h = rms_norm(x, attn_norm[layer])
        (m_qk, m_v, m_o, m_i, m_f, a_q, a_k, a_v, p_u, g_pre) = jnp.split(h @ w_in[layer], split_points, axis=-1)
        h_m = mlstm_branch(m_qk, m_v, m_o, m_i, m_f, m_conv_w[layer], m_conv_b[layer],
                           m_gate_bias[layer], m_head_norm[layer])
        lam_init = 0.8 - 0.6 * math.exp(-0.3 * layer)
        h_a = diff_attention_branch(a_q, a_k, a_v, a_q_norm[layer], a_k_norm[layer],
                                    a_lambda[layer], a_head_norm[layer], lam_init)
        h_p = pool_branch(p_u, pool_w[layer], pool_scale[layer])
        hb = jnp.stack([h_m, h_a, h_p], axis=2)
        yb = jnp.einsum('bsrc,rcd->bsrd', hb, w_branch[layer])
        gates = jax.nn.sigmoid(g_pre.reshape(B, S, N_BRANCHES, D))
        x = x + jnp.sum(gates * yb, axis=2) @ w_out[layer]
        hf = rms_norm(x, ffn_norm[layer])
        if layer % 2 == 0:
            j = layer // 2
            x = x + swiglu(hf, dense_w_gu[j], dense_w_down[j])
        else:
            j = layer // 2
            x = x + moe_swiglu(hf, router_w[j], router_b[j], moe_w_gu[j], moe_w_down[j])
        pg = jax.nn.sigmoid(rms_norm(x, ple_norm[layer]) @ ple_w_gate[layer])
        x = x + pg * (p[layer] @ ple_w_proj[layer])
    return x
```

```python
import functools
import math

import jax
import jax.numpy as jnp
from jax import lax
from jax.experimental import pallas as pl
from jax.experimental.pallas import tpu as pltpu

F32 = jnp.float32
BF16 = jnp.bfloat16
I32 = jnp.int32

D_MODEL = 1024
PLE_DIM = 256
NORM_EPS = 1e-6
BRANCH_WIDTH = 512

M_HEADS = 4
M_DK = 128
M_CONV = 4
M_CHUNK = 128

A_HEADS = 4
A_DV = 128
A_DQK = 64
ALIBI_SLOPES = tuple(2.0 ** (-8.0 * (h + 1) / A_HEADS) for h in range(A_HEADS))

P_WINDOWS = (2, 4, 8, 16)
P_GC = 128

D_FF = 2816
N_EXPERTS = 8
D_FF_EXPERT = 3584

PROJ_WIDTH = 7168
COL_MQK, COL_MV, COL_MO, COL_AQ, COL_AK, COL_AV, COL_PU, COL_G = 0, 1024, 1536, 2048, 2560, 3072, 3584, 4096

LANES = 128
NEG_BIG = -1e30

VMEM_LIMIT = 56 * 1024 * 1024


def _sigmoid(x):
    return 1.0 / (1.0 + jnp.exp(-x))


def _rms(x, gain):
    return x * lax.rsqrt(jnp.mean(x * x, axis=-1, keepdims=True) + NORM_EPS) * gain


def _dot(a, b):
    return jnp.dot(a, b, preferred_element_type=F32)


def _dot_nt(a, b):
    return lax.dot_general(a, b, (((1,), (1,)), ((), ())), preferred_element_type=F32)


def _dot_tn(a, b):
    return lax.dot_general(a, b, (((0,), (0,)), ((), ())), preferred_element_type=F32)


IN_TM = 1024
IN_TN = 1024


def _inproj_kernel(x_ref, gain_ref, w_ref, wif_ref, out_ref, gates_ref, hn_sc):
    @pl.when(pl.program_id(1) == 0)
    def _():
        hn = _rms(x_ref[...], gain_ref[...]).astype(BF16)
        hn_sc[...] = hn
        gates_ref[...] = _dot(hn, wif_ref[...])

    out_ref[...] = _dot(hn_sc[...], w_ref[...]).astype(BF16)


def _inproj(x2, gain, w_main, w_if):
    tok = x2.shape[0]
    return pl.pallas_call(
        _inproj_kernel,
        grid=(tok // IN_TM, PROJ_WIDTH // IN_TN),
        in_specs=[
            pl.BlockSpec((IN_TM, D_MODEL), lambda i, j: (i, 0)),
            pl.BlockSpec((1, D_MODEL), lambda i, j: (0, 0)),
            pl.BlockSpec((D_MODEL, IN_TN), lambda i, j: (0, j)),
            pl.BlockSpec((D_MODEL, LANES), lambda i, j: (0, 0)),
        ],
        out_specs=[
            pl.BlockSpec((IN_TM, IN_TN), lambda i, j: (i, j)),
            pl.BlockSpec((IN_TM, LANES), lambda i, j: (i, 0)),
        ],
        out_shape=[
            jax.ShapeDtypeStruct((tok, PROJ_WIDTH), BF16),
            jax.ShapeDtypeStruct((tok, LANES), F32),
        ],
        scratch_shapes=[pltpu.VMEM((IN_TM, D_MODEL), BF16)],
        compiler_params=pltpu.CompilerParams(
            dimension_semantics=("parallel", "arbitrary"), vmem_limit_bytes=VMEM_LIMIT),
        name="inproj",
    )(x2, gain, w_main, w_if)


def _mlstm_kernel(qk_ref, vo_ref, g_ref, cw_ref, cb_ref, gb_ref, hg_ref, out_ref,
                  c_sc, n_sc, m_sc, prev_sc):
    L = M_CHUNK

    @pl.when(pl.program_id(1) == 0)
    def _():
        c_sc[...] = jnp.zeros_like(c_sc)
        n_sc[...] = jnp.zeros_like(n_sc)
        m_sc[...] = jnp.zeros_like(m_sc)
        prev_sc[...] = jnp.zeros_like(prev_sc)

    x = qk_ref[...].astype(F32)
    prev = prev_sc[...]
    row1 = lax.broadcasted_iota(I32, (L, 1), 0)
    cw = cw_ref[...]
    y = cb_ref[...] + cw[M_CONV - 1:M_CONV, :] * x
    for s in range(1, M_CONV):
        shifted = jnp.where(row1 < s, pltpu.roll(prev, s, 0), pltpu.roll(x, s, 0))
        y = y + cw[M_CONV - 1 - s:M_CONV - s, :] * shifted
    prev_sc[...] = x
    qk = y * _sigmoid(y)

    g = g_ref[...] + gb_ref[...]
    lane = lax.broadcasted_iota(I32, (L, LANES), 1)
    rowl = lax.broadcasted_iota(I32, (L, LANES), 0)
    logf = jnp.minimum(g, 0.0) - jnp.log(1.0 + jnp.exp(-jnp.abs(g)))
    is_f = (lane >= M_HEADS) & (lane < 2 * M_HEADS)
    gc = jnp.where(is_f, logf, jnp.where(lane < M_HEADS, g, 0.0))
    bcum = jnp.where(is_f, logf, 0.0)
    k = 1
    while k < L:
        bcum = bcum + jnp.where(rowl >= k, pltpu.roll(bcum, k, 0), 0.0)
        k *= 2
    gc_t = gc.T
    bcum_t = bcum.T

    rr = lax.broadcasted_iota(I32, (L, L), 0)
    cc = lax.broadcasted_iota(I32, (L, L), 1)
    causal = rr >= cc
    hg = hg_ref[...]

    for h in range(M_HEADS):
        sl = slice(h * M_DK, (h + 1) * M_DK)
        b_col = bcum[:, M_HEADS + h:M_HEADS + h + 1]
        i_col = gc[:, h:h + 1]
        b_row = bcum_t[M_HEADS + h:M_HEADS + h + 1, :]
        i_row = gc_t[h:h + 1, :]
        m_prev = m_sc[h:h + 1, 0:1]

        log_d = jnp.where(causal, b_col - b_row + i_row, NEG_BIG)
        inter = b_col + m_prev
        m_t = jnp.maximum(jnp.max(log_d, axis=-1, keepdims=True), inter)
        d_w = jnp.exp(log_d - m_t)
        w_inter = jnp.exp(inter - m_t)

        qh = qk[:, sl]
        kh = qk[:, BRANCH_WIDTH + h * M_DK:BRANCH_WIDTH + (h + 1) * M_DK] * (M_DK ** -0.5)
        qb = qh.astype(BF16)
        vb = vo_ref[:, sl]
        c_h = c_sc[h]
        n_row = n_sc[h:h + 1, :]

        s = _dot_nt(qb, kh.astype(BF16)) * d_w
        num = _dot(s.astype(BF16), vb) + w_inter * _dot(qb, c_h.astype(BF16))
        den = jnp.sum(s, axis=-1, keepdims=True) + w_inter * jnp.sum(qh * n_row, axis=-1, keepdims=True)
        hh = num / jnp.maximum(jnp.abs(den), jnp.exp(-m_t))

        b_last = b_col[L - 1:L, :]
        log_w = b_last - b_col + i_col
        m_new = jnp.maximum(b_last + m_prev, jnp.max(log_w, axis=0, keepdims=True))
        decay = jnp.exp(b_last + m_prev - m_new)
        kw = kh * jnp.exp(log_w - m_new)
        c_sc[h] = decay * c_h + _dot_tn(kw.astype(BF16), vb)
        n_sc[h:h + 1, :] = decay * n_row + jnp.sum(kw, axis=0, keepdims=True)
        m_sc[h:h + 1, :] = jnp.broadcast_to(m_new, (1, LANES))

        hn = _rms(hh, hg[:, sl])
        o_pre = vo_ref[:, BRANCH_WIDTH + h * M_DK:BRANCH_WIDTH + (h + 1) * M_DK].astype(F32)
        out_ref[:, sl] = (hn * _sigmoid(o_pre)).astype(BF16)


def _mlstm(proj3, gates3, conv_w, conv_b, gate_bias, head_gain):
    b, s, _ = proj3.shape
    L = M_CHUNK
    return pl.pallas_call(
        _mlstm_kernel,
        grid=(b, s // L),
        in_specs=[
            pl.BlockSpec((None, L, 1024), lambda i, c: (i, c, COL_MQK // 1024)),
            pl.BlockSpec((None, L, 1024), lambda i, c: (i, c, COL_MV // 1024)),
            pl.BlockSpec((None, L, LANES), lambda i, c: (i, c, 0)),
            pl.BlockSpec((M_CONV, 1024), lambda i, c: (0, 0)),
            pl.BlockSpec((1, 1024), lambda i, c: (0, 0)),
            pl.BlockSpec((1, LANES), lambda i, c: (0, 0)),
            pl.BlockSpec((1, BRANCH_WIDTH), lambda i, c: (0, 0)),
        ],
        out_specs=pl.BlockSpec((None, L, BRANCH_WIDTH), lambda i, c: (i, c, 0)),
        out_shape=jax.ShapeDtypeStruct((b, s, BRANCH_WIDTH), BF16),
        scratch_shapes=[
            pltpu.VMEM((M_HEADS, M_DK, M_DK), F32),
            pltpu.VMEM((8, LANES), F32),
            pltpu.VMEM((8, LANES), F32),
            pltpu.VMEM((L, 1024), F32),
        ],
        compiler_params=pltpu.CompilerParams(
            dimension_semantics=("parallel", "arbitrary"), vmem_limit_bytes=VMEM_LIMIT),
        name="mlstm",
    )(proj3, proj3, gates3, conv_w, conv_b, gate_bias, head_gain)


AT_T = 256


def _halfnorm(x, gain):
    lane = lax.broadcasted_iota(I32, (1, LANES), 1)
    lo = lane < A_DQK
    x2 = x * x
    ms_lo = jnp.sum(jnp.where(lo, x2, 0.0), axis=-1, keepdims=True)
    ms_hi = jnp.sum(jnp.where(lo, 0.0, x2), axis=-1, keepdims=True)
    ms = jnp.where(lo, ms_lo, ms_hi) * (1.0 / A_DQK)
    return x * lax.rsqrt(ms + NORM_EPS) * gain


def _attn_kernel(slopes_ref, q_ref, k_ref, v_ref, qg_ref, kg_ref, lam_ref, hg_ref, out_ref, kn_sc,
                 *, lam_init):
    T = AT_T
    h = pl.program_id(1)
    qi = pl.program_id(2)

    @pl.when(qi == 0)
    def _():
        kn_sc[...] = _halfnorm(k_ref[...].astype(F32), kg_ref[...]).astype(BF16)

    lp = lam_ref[...]
    lam = (jnp.exp(jnp.sum(lp[0:1, :] * lp[1:2, :], axis=-1, keepdims=True))
           - jnp.exp(jnp.sum(lp[2:3, :] * lp[3:4, :], axis=-1, keepdims=True)) + lam_init)

    qn = _halfnorm(q_ref[...].astype(F32), qg_ref[...]) * (A_DQK ** -0.5)
    lane = lax.broadcasted_iota(I32, (1, LANES), 1)
    q1 = jnp.where(lane < A_DQK, qn, 0.0).astype(BF16)
    q2 = jnp.where(lane < A_DQK, 0.0, qn).astype(BF16)

    slope = slopes_ref[h]
    rr = lax.broadcasted_iota(I32, (T, T), 0)
    cc = lax.broadcasted_iota(I32, (T, T), 1)
    rel = (cc - rr).astype(F32) * slope

    def update(state, sc, vb):
        m, l, acc = state
        m_new = jnp.maximum(m, jnp.max(sc, axis=-1, keepdims=True))
        p = jnp.exp(sc - m_new)
        alpha = jnp.exp(m - m_new)
        return (m_new, alpha * l + jnp.sum(p, axis=-1, keepdims=True),
                alpha * acc + _dot(p.astype(BF16), vb))

    def block(j, st1, st2, diag):
        start = pl.multiple_of(j * T, T)
        kb = kn_sc[pl.ds(start, T), :]
        vb = v_ref[pl.ds(start, T), :]
        bias = rel + slope * ((j - qi) * T).astype(F32)
        s1 = _dot_nt(q1, kb) + bias
        s2 = _dot_nt(q2, kb) + bias
        if diag:
            s1 = jnp.where(rr >= cc, s1, NEG_BIG)
            s2 = jnp.where(rr >= cc, s2, NEG_BIG)
        return update(st1, s1, vb), update(st2, s2, vb)

    init = (jnp.full((T, 1), NEG_BIG, F32), jnp.zeros((T, 1), F32), jnp.zeros((T, A_DV), F32))
    st1, st2 = lax.fori_loop(0, qi, lambda j, c: block(j, c[0], c[1], False), (init, init))
    (_, l1, a1), (_, l2, a2) = block(qi, st1, st2, True)

    o = a1 / l1 - lam * (a2 / l2)
    out_ref[...] = (_rms(o, hg_ref[...]) * (1.0 - lam_init)).astype(BF16)


def _attention(proj3, q_gain, k_gain, lam_params, head_gain, lam_init):
    b, s, _ = proj3.shape
    T = AT_T
    slopes = jnp.asarray(ALIBI_SLOPES, F32)
    return pl.pallas_call(
        functools.partial(_attn_kernel, lam_init=lam_init),
        grid=(b, A_HEADS, s // T),
        in_specs=[
            pl.BlockSpec(memory_space=pltpu.SMEM),
            pl.BlockSpec((None, T, LANES), lambda i, h, q: (i, q, COL_AQ // LANES + h)),
            pl.BlockSpec((None, s, LANES), lambda i, h, q: (i, 0, COL_AK // LANES + h)),
            pl.BlockSpec((None, s, LANES), lambda i, h, q: (i, 0, COL_AV // LANES + h)),
            pl.BlockSpec((1, LANES), lambda i, h, q: (0, 0)),
            pl.BlockSpec((1, LANES), lambda i, h, q: (0, 0)),
            pl.BlockSpec((4, A_DQK), lambda i, h, q: (0, 0)),
            pl.BlockSpec((1, LANES), lambda i, h, q: (0, h)),
        ],
        out_specs=pl.BlockSpec((None, T, LANES), lambda i, h, q: (i, q, h)),
        out_shape=jax.ShapeDtypeStruct((b, s, BRANCH_WIDTH), BF16),
        scratch_shapes=[pltpu.VMEM((s, LANES), BF16)],
        compiler_params=pltpu.CompilerParams(
            dimension_semantics=("parallel", "parallel", "arbitrary"), vmem_limit_bytes=VMEM_LIMIT),
        name="diff_attn",
    )(slopes, proj3, proj3, proj3, q_gain, k_gain, lam_params, head_gain)


MG_TM = 512
MG_HALO = 128


def _merge_kernel(x_ref, hm_ref, ha_ref, u_ref, halo_ref, gm_ref, ga_ref, gp_ref, pw_ref, ps_ref,
                  wb_ref, wo_ref, fg_ref, *rest, seq_tiles, routed):
    if routed:
        wr_ref, br_ref, xo_ref, hf_ref, route_ref = rest
    else:
        xo_ref, hf_ref = rest
    TM = MG_TM
    i = pl.program_id(0)
    seq_tile = i % seq_tiles

    u = u_ref[...]
    halo = jnp.where(seq_tile > 0, halo_ref[...], jnp.zeros_like(halo_ref))
    d_main = (lax.broadcasted_iota(I32, (TM, TM), 0) - lax.broadcasted_iota(I32, (TM, TM), 1))
    d_halo = (lax.broadcasted_iota(I32, (TM, MG_HALO), 0) + MG_HALO
              - lax.broadcasted_iota(I32, (TM, MG_HALO), 1))
    t_pos = seq_tile * TM + lax.broadcasted_iota(I32, (TM, 1), 0)
    ps = ps_ref[...]
    hp_parts = []
    for g, w in enumerate(P_WINDOWS):
        sl = slice(g * P_GC, (g + 1) * P_GC)
        band = jnp.where((d_main >= 0) & (d_main < w), 1.0, 0.0).astype(BF16)
        band_h = jnp.where(d_halo < w, 1.0, 0.0).astype(BF16)
        ug = u[:, sl]
        sums = _dot(band, ug) + _dot(band_h, halo[:, sl])
        cnt = jnp.minimum(t_pos + 1, w).astype(F32)
        pooled = sums / cnt - ug.astype(F32)
        hp_parts.append((_dot(pooled.astype(BF16), pw_ref[g]) * ps[:, sl]).astype(BF16))
    hp = jnp.concatenate(hp_parts, axis=1)

    merged = (_sigmoid(gm_ref[...].astype(F32)) * _dot(hm_ref[...], wb_ref[0])
              + _sigmoid(ga_ref[...].astype(F32)) * _dot(ha_ref[...], wb_ref[1])
              + _sigmoid(gp_ref[...].astype(F32)) * _dot(hp, wb_ref[2]))
    x_new = x_ref[...] + _dot(merged.astype(BF16), wo_ref[...])
    xo_ref[...] = x_new
    hf = _rms(x_new, fg_ref[...])
    hf_ref[...] = hf.astype(BF16)

    if routed:
        logits = jnp.dot(hf, wr_ref[...], preferred_element_type=F32,
                         precision=lax.Precision.HIGHEST) + br_ref[...]
        lane = lax.broadcasted_iota(I32, (TM, LANES), 1)
        lanef = lane.astype(F32)
        lg = jnp.where(lane < N_EXPERTS, logits, NEG_BIG)
        m1 = jnp.max(lg, axis=-1, keepdims=True)
        i1 = jnp.min(jnp.where(lg == m1, lanef, float(LANES)), axis=-1, keepdims=True)
        lg2 = jnp.where(lanef == i1, NEG_BIG, lg)
        m2 = jnp.max(lg2, axis=-1, keepdims=True)
        i2 = jnp.min(jnp.where(lg2 == m2, lanef, float(LANES)), axis=-1, keepdims=True)
        g1 = 1.0 / (1.0 + jnp.exp(m2 - m1))
        g2 = 1.0 - g1
        route_ref[...] = jnp.where(lane == 0, g1, jnp.where(lane == 1, g2, jnp.where(
            lane == 2, i1, jnp.where(lane == 3, i2, 0.0))))


def _merge(x2, hm2, ha2, proj2, pool_w, pool_scale, w_branch, w_out, ffn_gain, seq, router=None):
    tok = x2.shape[0]
    TM = MG_TM
    routed = router is not None
    hb = TM // MG_HALO
    in_specs = [
        pl.BlockSpec((TM, D_MODEL), lambda i: (i, 0)),
        pl.BlockSpec((TM, BRANCH_WIDTH), lambda i: (i, 0)),
        pl.BlockSpec((TM, BRANCH_WIDTH), lambda i: (i, 0)),
        pl.BlockSpec((TM, BRANCH_WIDTH), lambda i: (i, COL_PU // BRANCH_WIDTH)),
        pl.BlockSpec((MG_HALO, BRANCH_WIDTH), lambda i: (jnp.maximum(i * hb - 1, 0), COL_PU // BRANCH_WIDTH)),
        pl.BlockSpec((TM, D_MODEL), lambda i: (i, COL_G // D_MODEL)),
        pl.BlockSpec((TM, D_MODEL), lambda i: (i, COL_G // D_MODEL + 1)),
        pl.BlockSpec((TM, D_MODEL), lambda i: (i, COL_G // D_MODEL + 2)),
        pl.BlockSpec((len(P_WINDOWS), P_GC, P_GC), lambda i: (0, 0, 0)),
        pl.BlockSpec((1, BRANCH_WIDTH), lambda i: (0, 0)),
        pl.BlockSpec((3, BRANCH_WIDTH, D_MODEL), lambda i: (0, 0, 0)),
        pl.BlockSpec((D_MODEL, D_MODEL), lambda i: (0, 0)),
        pl.BlockSpec((1, D_MODEL), lambda i: (0, 0)),
    ]
    args = [x2, hm2, ha2, proj2, proj2, proj2, proj2, proj2, pool_w, pool_scale, w_branch, w_out, ffn_gain]
    out_specs = [pl.BlockSpec((TM, D_MODEL), lambda i: (i, 0)),
                 pl.BlockSpec((TM, D_MODEL), lambda i: (i, 0))]
    out_shape = [jax.ShapeDtypeStruct((tok, D_MODEL), F32), jax.ShapeDtypeStruct((tok, D_MODEL), BF16)]
    if routed:
        in_specs += [pl.BlockSpec((D_MODEL, LANES), lambda i: (0, 0)),
                     pl.BlockSpec((1, LANES), lambda i: (0, 0))]
        args += list(router)
        out_specs.append(pl.BlockSpec((TM, LANES), lambda i: (i, 0)))
        out_shape.append(jax.ShapeDtypeStruct((tok, LANES), F32))
    return pl.pallas_call(
        functools.partial(_merge_kernel, seq_tiles=seq // TM, routed=routed),
        grid=(tok // TM,),
        in_specs=in_specs,
        out_specs=out_specs,
        out_shape=out_shape,
        compiler_params=pltpu.CompilerParams(
            dimension_semantics=("parallel",), vmem_limit_bytes=VMEM_LIMIT),
        name="merge_routed" if routed else "merge",
    )(*args)


def _ple_epilogue(x_new, p, pg_gain, wpg, wpp):
    gate = _sigmoid(_dot(_rms(x_new, pg_gain).astype(BF16), wpg))
    return x_new + gate * _dot(p.astype(BF16), wpp)


FF_TM = 512
FF_TF = 1408


def _ffn_kernel(hf_ref, x_ref, p_ref, wg_ref, wu_ref, wd_ref, pgn_ref, wpg_ref, wpp_ref, out_ref, acc_sc):
    j = pl.program_id(1)
    hf = hf_ref[...]
    g = _dot(hf, wg_ref[...])
    u = _dot(hf, wu_ref[...])
    part = _dot((g * _sigmoid(g) * u).astype(BF16), wd_ref[...])

    @pl.when(j == 0)
    def _():
        acc_sc[...] = part

    @pl.when(j > 0)
    def _():
        acc_sc[...] += part

    @pl.when(j == pl.num_programs(1) - 1)
    def _():
        out_ref[...] = _ple_epilogue(x_ref[...] + acc_sc[...], p_ref[...], pgn_ref[...],
                                     wpg_ref[...], wpp_ref[...])


def _ffn(hf2, x2, p2, w_gu, w_down, ple_gain, wpg, wpp):
    tok = x2.shape[0]
    TM, TF = FF_TM, FF_TF
    nf = D_FF // TF
    return pl.pallas_call(
        _ffn_kernel,
        grid=(tok // TM, nf),
        in_specs=[
            pl.BlockSpec((TM, D_MODEL), lambda i, j: (i, 0)),
            pl.BlockSpec((TM, D_MODEL), lambda i, j: (i, 0)),
            pl.BlockSpec((TM, PLE_DIM), lambda i, j: (i, 0)),
            pl.BlockSpec((D_MODEL, TF), lambda i, j: (0, j)),
            pl.BlockSpec((D_MODEL, TF), lambda i, j: (0, j + nf)),
            pl.BlockSpec((TF, D_MODEL), lambda i, j: (j, 0)),
            pl.BlockSpec((1, D_MODEL), lambda i, j: (0, 0)),
            pl.BlockSpec((D_MODEL, D_MODEL), lambda i, j: (0, 0)),
            pl.BlockSpec((PLE_DIM, D_MODEL), lambda i, j: (0, 0)),
        ],
        out_specs=pl.BlockSpec((TM, D_MODEL), lambda i, j: (i, 0)),
        out_shape=jax.ShapeDtypeStruct((tok, D_MODEL), F32),
        scratch_shapes=[pltpu.VMEM((TM, D_MODEL), F32)],
        compiler_params=pltpu.CompilerParams(
            dimension_semantics=("parallel", "arbitrary"), vmem_limit_bytes=VMEM_LIMIT),
        name="ffn_dense",
    )(hf2, x2, p2, w_gu, w_gu, w_down, ple_gain, wpg, wpp)


MOE_TM = 512
MOE_TF = 512
MOE_CH = 256
CB_TB = 256
CB_ALIGN = 16
CB_W = CB_TB + CB_ALIGN


def _moe_kernel(te_ref, tv_ref, clo_ref, chi_ref, tok_ref, gate_ref, hf_ref, wg_ref, wu_ref, wd_ref,
                y_ref, xb_sc, acc_sc):
    i = pl.program_id(0)
    j = pl.program_id(1)
    last = pl.num_programs(1) - 1
    valid = tv_ref[i] > 0

    @pl.when(valid & (j == 0))
    def _():
        tok_col = tok_ref[...]
        lane = lax.broadcasted_iota(I32, (1, MOE_CH), 1)
        acc_sc[...] = jnp.zeros_like(acc_sc)

        def chunk(c, carry):
            start = pl.multiple_of(c * MOE_CH, MOE_CH)
            onehot = jnp.where(tok_col == lane + start, 1.0, 0.0).astype(BF16)
            acc_sc[...] += _dot(onehot, hf_ref[pl.ds(start, MOE_CH), :])
            return carry

        lax.fori_loop(clo_ref[i], chi_ref[i] + 1, chunk, 0)
        xb_sc[...] = acc_sc[...].astype(BF16)

    @pl.when(valid)
    def _():
        xb = xb_sc[...]
        g = _dot(xb, wg_ref[...])
        u = _dot(xb, wu_ref[...])
        part = _dot((g * _sigmoid(g) * u).astype(BF16), wd_ref[...])

        @pl.when(j == 0)
        def _():
            acc_sc[...] = part

        @pl.when(j > 0)
        def _():
            acc_sc[...] += part

        @pl.when(j == last)
        def _():
            y_ref[...] = (acc_sc[...] * gate_ref[...]).astype(BF16)

    @pl.when(jnp.logical_not(valid) & (j == last))
    def _():
        y_ref[...] = jnp.zeros_like(y_ref)


def _moe(hf2, w_gu, w_down, tile_expert, tile_valid, chunk_lo, chunk_hi, tok_sorted, gate_sorted):
    tok = hf2.shape[0]
    TM, TF = MOE_TM, MOE_TF
    nt = tile_expert.shape[0]
    nf = D_FF_EXPERT // TF
    grid_spec = pltpu.PrefetchScalarGridSpec(
        num_scalar_prefetch=4,
        grid=(nt, nf),
        in_specs=[
            pl.BlockSpec((TM, 1), lambda i, j, te, tv, lo, hi: (i, 0)),
            pl.BlockSpec((TM, 1), lambda i, j, te, tv, lo, hi: (i, 0)),
            pl.BlockSpec((tok, D_MODEL), lambda i, j, te, tv, lo, hi: (0, 0),
                         pipeline_mode=pl.Buffered(1)),
            pl.BlockSpec((None, D_MODEL, TF), lambda i, j, te, tv, lo, hi: (te[i], 0, j)),
            pl.BlockSpec((None, D_MODEL, TF), lambda i, j, te, tv, lo, hi: (te[i], 0, j + nf)),
            pl.BlockSpec((None, TF, D_MODEL), lambda i, j, te, tv, lo, hi: (te[i], j, 0)),
        ],
        out_specs=pl.BlockSpec((TM, D_MODEL), lambda i, j, te, tv, lo, hi: (i, 0)),
        scratch_shapes=[pltpu.VMEM((TM, D_MODEL), BF16), pltpu.VMEM((TM, D_MODEL), F32)],
    )
    return pl.pallas_call(
        _moe_kernel,
        grid_spec=grid_spec,
        out_shape=jax.ShapeDtypeStruct((nt * TM, D_MODEL), BF16),
        compiler_params=pltpu.CompilerParams(
            dimension_semantics=("arbitrary", "arbitrary"), vmem_limit_bytes=VMEM_LIMIT),
        name="moe_experts",
    )(tile_expert, tile_valid, chunk_lo, chunk_hi, tok_sorted, gate_sorted, hf2, w_gu, w_gu, w_down)


def _combine_kernel(w0_ref, x_ref, p_ref, pos_ref, y_hbm, pgn_ref, wpg_ref, wpp_ref, out_ref, win_sc, sem):
    i = pl.program_id(0)

    def window_copy(e):
        start = pl.multiple_of(w0_ref[i * N_EXPERTS + e], CB_ALIGN)
        return pltpu.make_async_copy(y_hbm.at[pl.ds(start, CB_W), :], win_sc.at[e], sem.at[e])

    for e in range(N_EXPERTS):
        window_copy(e).start()

    pos = pos_ref[...]
    pos1 = pos[:, 0:1]
    pos2 = pos[:, 1:2]
    lane = lax.broadcasted_iota(I32, (1, CB_W), 1)
    moe_out = jnp.zeros((CB_TB, D_MODEL), F32)
    for e in range(N_EXPERTS):
        window_copy(e).wait()
        rows = lane + w0_ref[i * N_EXPERTS + e]
        onehot = jnp.where((pos1 == rows) | (pos2 == rows), 1.0, 0.0).astype(BF16)
        moe_out = moe_out + _dot(onehot, win_sc[e])

    out_ref[...] = _ple_epilogue(x_ref[...] + moe_out, p_ref[...], pgn_ref[...], wpg_ref[...], wpp_ref[...])


def _combine(x2, p2, pos, y_sorted, win_start, ple_gain, wpg, wpp):
    tok = x2.shape[0]
    TB = CB_TB
    grid_spec = pltpu.PrefetchScalarGridSpec(
        num_scalar_prefetch=1,
        grid=(tok // TB,),
        in_specs=[
            pl.BlockSpec((TB, D_MODEL), lambda i, w0: (i, 0)),
            pl.BlockSpec((TB, PLE_DIM), lambda i, w0: (i, 0)),
            pl.BlockSpec((TB, 2), lambda i, w0: (i, 0)),
            pl.BlockSpec(memory_space=pl.ANY),
            pl.BlockSpec((1, D_MODEL), lambda i, w0: (0, 0)),
            pl.BlockSpec((D_MODEL, D_MODEL), lambda i, w0: (0, 0)),
            pl.BlockSpec((PLE_DIM, D_MODEL), lambda i, w0: (0, 0)),
        ],
        out_specs=pl.BlockSpec((TB, D_MODEL), lambda i, w0: (i, 0)),
        scratch_shapes=[pltpu.VMEM((N_EXPERTS, CB_W, D_MODEL), BF16),
                        pltpu.SemaphoreType.DMA((N_EXPERTS,))],
    )
    return pl.pallas_call(
        _combine_kernel,
        grid_spec=grid_spec,
        out_shape=jax.ShapeDtypeStruct((tok, D_MODEL), F32),
        compiler_params=pltpu.CompilerParams(
            dimension_semantics=("arbitrary",), vmem_limit_bytes=VMEM_LIMIT),
        name="moe_combine",
    )(win_start, x2, p2, pos, y_sorted, ple_gain, wpg, wpp)


def _route_metadata(route, tok):
    TM = MOE_TM
    nt = (2 * tok) // TM + N_EXPERTS
    rows = nt * TM
    g1, g2 = route[:, 0], route[:, 1]
    i1, i2 = route[:, 2].astype(I32), route[:, 3].astype(I32)
    experts = jnp.arange(N_EXPERTS, dtype=I32)
    member = ((i1[:, None] == experts) | (i2[:, None] == experts)).astype(I32)
    rank = jnp.cumsum(member, axis=0) - member
    counts = jnp.sum(member, axis=0)
    tiles_e = (counts + TM - 1) // TM
    tile_end = jnp.cumsum(tiles_e)
    tile_start = tile_end - tiles_e
    seg_start = tile_start * TM
    pos1 = seg_start[i1] + jnp.take_along_axis(rank, i1[:, None], axis=1)[:, 0]
    pos2 = seg_start[i2] + jnp.take_along_axis(rank, i2[:, None], axis=1)[:, 0]
    t_ids = jnp.arange(tok, dtype=I32)
    both = jnp.concatenate([pos1, pos2])
    tok_sorted = jnp.full((rows,), -1, I32).at[both].set(jnp.concatenate([t_ids, t_ids]))
    gate_sorted = jnp.zeros((rows,), F32).at[both].set(jnp.concatenate([g1, g2]))

    tiles = jnp.arange(nt, dtype=I32)
    n_used = tile_end[-1]
    tile_valid = (tiles < n_used).astype(I32)
    te = jnp.sum((tiles[:, None] >= tile_end[None, :]).astype(I32), axis=1)
    te_last = jnp.sum(((n_used - 1) >= tile_end).astype(I32))
    tile_expert = jnp.where(tile_valid > 0, jnp.minimum(te, N_EXPERTS - 1), te_last).astype(I32)
    real = jnp.clip(counts[tile_expert] - (tiles - tile_start[tile_expert]) * TM, 1, TM)
    first_tok = tok_sorted[tiles * TM]
    last_tok = tok_sorted[tiles * TM + real - 1]
    chunk_lo = jnp.where(tile_valid > 0, first_tok // MOE_CH, 0).astype(I32)
    chunk_hi = jnp.where(tile_valid > 0, last_tok // MOE_CH, -1).astype(I32)

    blk_start = seg_start[None, :] + rank[::CB_TB, :]
    win_start = jnp.minimum((blk_start // CB_ALIGN) * CB_ALIGN, rows - CB_W).astype(I32).reshape(-1)
    pos = jnp.stack([pos1, pos2], axis=1).astype(I32)
    return (tile_expert, tile_valid, chunk_lo, chunk_hi, tok_sorted.reshape(rows, 1),
            gate_sorted.reshape(rows, 1), pos, win_start)


def _pack_w_in(w):
    o = [0, 1024, 1536, 2048, 2052, 2056, 2568, 3080, 3592, 4104, 7176]
    main = jnp.concatenate([w[:, o[0]:o[3]], w[:, o[5]:o[10]]], axis=1).astype(BF16)
    gates = jnp.pad(w[:, o[3]:o[5]], ((0, 0), (0, LANES - 2 * M_HEADS))).astype(BF16)
    return main, gates


def kernel(x, p, attn_norm, w_in, m_conv_w, m_conv_b, m_gate_bias, m_head_norm, a_q_norm, a_k_norm,
           a_lambda, a_head_norm, pool_w, pool_scale, w_branch, w_out, ffn_norm, dense_w_gu, dense_w_down,
           router_w, router_b, moe_w_gu, moe_w_down, ple_norm, ple_w_gate, ple_w_proj):
    b, s, d = x.shape
    depth = w_in.shape[0]
    tok = b * s
    x2 = x.reshape(tok, d)
    for layer in range(depth):
        w_main, w_if = _pack_w_in(w_in[layer])
        proj, gates = _inproj(x2, attn_norm[layer].reshape(1, d), w_main, w_if)
        proj3 = proj.reshape(b, s, PROJ_WIDTH)
        gate_bias = jnp.pad(m_gate_bias[layer], (0, LANES - 2 * M_HEADS)).reshape(1, LANES)
        h_m = _mlstm(proj3, gates.reshape(b, s, LANES), m_conv_w[layer], m_conv_b[layer].reshape(1, -1),
                     gate_bias, m_head_norm[layer].reshape(1, -1))
        lam_init = 0.8 - 0.6 * math.exp(-0.3 * layer)
        h_a = _attention(proj3, jnp.tile(a_q_norm[layer], 2).reshape(1, LANES),
                         jnp.tile(a_k_norm[layer], 2).reshape(1, LANES), a_lambda[layer],
                         a_head_norm[layer].reshape(1, -1), lam_init)
        p2 = p[layer].reshape(tok, PLE_DIM)
        ple_args = (ple_norm[layer].reshape(1, d), ple_w_gate[layer].astype(BF16), ple_w_proj[layer].astype(BF16))
        merge_args = (x2, h_m.reshape(tok, BRANCH_WIDTH), h_a.reshape(tok, BRANCH_WIDTH), proj,
                      pool_w[layer].astype(BF16), pool_scale[layer].reshape(1, -1),
                      w_branch[layer].astype(BF16), w_out[layer].astype(BF16), ffn_norm[layer].reshape(1, d), s)
        j = layer // 2
        if layer % 2 == 0:
            x_mid, hf = _merge(*merge_args)
            x2 = _ffn(hf, x_mid, p2, dense_w_gu[j].astype(BF16), dense_w_down[j].astype(BF16), *ple_args)
        else:
            wr = jnp.pad(router_w[j], ((0, 0), (0, LANES - N_EXPERTS)))
            br = jnp.pad(router_b[j], (0, LANES - N_EXPERTS)).reshape(1, LANES)
            x_mid, hf, route = _merge(*merge_args, router=(wr, br))
            (tile_expert, tile_valid, chunk_lo, chunk_hi, tok_sorted, gate_sorted, pos,
             win_start) = _route_metadata(route, tok)
            y_sorted = _moe(hf, moe_w_gu[j].astype(BF16), moe_w_down[j].astype(BF16), tile_expert,
                            tile_valid, chunk_lo, chunk_hi, tok_sorted, gate_sorted)
            x2 = _combine(x_mid, p2, pos, y_sorted, win_start, *ple_args)
    return x2.reshape(b, s, d)
```

```python
import functools
import math

import jax
import jax.numpy as jnp
from jax import lax
from jax.experimental import pallas as pl
from jax.experimental.pallas import tpu as pltpu

F32 = jnp.float32
BF16 = jnp.bfloat16
I32 = jnp.int32

D_MODEL = 1024
PLE_DIM = 256
NORM_EPS = 1e-6
BRANCH_WIDTH = 512

M_HEADS = 4
M_DK = 128
M_CONV = 4
M_CHUNK = 128

A_HEADS = 4
A_DV = 128
A_DQK = 64
ALIBI_SLOPES = tuple(2.0 ** (-8.0 * (h + 1) / A_HEADS) for h in range(A_HEADS))

P_WINDOWS = (2, 4, 8, 16)
P_GC = 128

D_FF = 2816
N_EXPERTS = 8
D_FF_EXPERT = 3584

PROJ_WIDTH = 7168
COL_MQK, COL_MV, COL_MO, COL_AQ, COL_AK, COL_AV, COL_PU, COL_G = 0, 1024, 1536, 2048, 2560, 3072, 3584, 4096

LANES = 128
NEG_BIG = -1e30

VMEM_LIMIT = 56 * 1024 * 1024


def _sigmoid(x):
    return 1.0 / (1.0 + jnp.exp(-x))


def _rms(x, gain):
    return x * lax.rsqrt(jnp.mean(x * x, axis=-1, keepdims=True) + NORM_EPS) * gain


def _dot(a, b):
    return jnp.dot(a, b, preferred_element_type=F32)


def _dot_nt(a, b):
    return lax.dot_general(a, b, (((1,), (1,)), ((), ())), preferred_element_type=F32)


def _dot_tn(a, b):
    return lax.dot_general(a, b, (((0,), (0,)), ((), ())), preferred_element_type=F32)


IN_TM = 1024
IN_TN = 1024


def _inproj_kernel(x_ref, gain_ref, w_ref, wif_ref, out_ref, gates_ref, hn_sc):
    @pl.when(pl.program_id(1) == 0)
    def _():
        hn = _rms(x_ref[...], gain_ref[...]).astype(BF16)
        hn_sc[...] = hn
        gates_ref[...] = _dot(hn, wif_ref[...])

    out_ref[...] = _dot(hn_sc[...], w_ref[...]).astype(BF16)


def _inproj(x2, gain, w_main, w_if):
    tok = x2.shape[0]
    return pl.pallas_call(
        _inproj_kernel,
        grid=(tok // IN_TM, PROJ_WIDTH // IN_TN),
        in_specs=[
            pl.BlockSpec((IN_TM, D_MODEL), lambda i, j: (i, 0)),
            pl.BlockSpec((1, D_MODEL), lambda i, j: (0, 0)),
            pl.BlockSpec((D_MODEL, IN_TN), lambda i, j: (0, j)),
            pl.BlockSpec((D_MODEL, LANES), lambda i, j: (0, 0)),
        ],
        out_specs=[
            pl.BlockSpec((IN_TM, IN_TN), lambda i, j: (i, j)),
            pl.BlockSpec((IN_TM, LANES), lambda i, j: (i, 0)),
        ],
        out_shape=[
            jax.ShapeDtypeStruct((tok, PROJ_WIDTH), BF16),
            jax.ShapeDtypeStruct((tok, LANES), F32),
        ],
        scratch_shapes=[pltpu.VMEM((IN_TM, D_MODEL), BF16)],
        compiler_params=pltpu.CompilerParams(
            dimension_semantics=("parallel", "arbitrary"), vmem_limit_bytes=VMEM_LIMIT),
        name="inproj",
    )(x2, gain, w_main, w_if)


def _mlstm_kernel(qk_ref, vo_ref, g_ref, cw_ref, cb_ref, gb_ref, hg_ref, out_ref,
                  c_sc, n_sc, m_sc, prev_sc):
    L = M_CHUNK

    @pl.when(pl.program_id(1) == 0)
    def _():
        c_sc[...] = jnp.zeros_like(c_sc)
        n_sc[...] = jnp.zeros_like(n_sc)
        m_sc[...] = jnp.zeros_like(m_sc)
        prev_sc[...] = jnp.zeros_like(prev_sc)

    x = qk_ref[...].astype(F32)
    prev = prev_sc[...]
    row1 = lax.broadcasted_iota(I32, (L, 1), 0)
    cw = cw_ref[...]
    y = cb_ref[...] + cw[M_CONV - 1:M_CONV, :] * x
    for s in range(1, M_CONV):
        shifted = jnp.where(row1 < s, pltpu.roll(prev, s, 0), pltpu.roll(x, s, 0))
        y = y + cw[M_CONV - 1 - s:M_CONV - s, :] * shifted
    prev_sc[...] = x
    qk = y * _sigmoid(y)

    g = g_ref[...] + gb_ref[...]
    lane = lax.broadcasted_iota(I32, (L, LANES), 1)
    rowl = lax.broadcasted_iota(I32, (L, LANES), 0)
    logf = jnp.minimum(g, 0.0) - jnp.log(1.0 + jnp.exp(-jnp.abs(g)))
    is_f = (lane >= M_HEADS) & (lane < 2 * M_HEADS)
    gc = jnp.where(is_f, logf, jnp.where(lane < M_HEADS, g, 0.0))
    bcum = jnp.where(is_f, logf, 0.0)
    k = 1
    while k < L:
        bcum = bcum + jnp.where(rowl >= k, pltpu.roll(bcum, k, 0), 0.0)
        k *= 2
    gc_t = gc.T
    bcum_t = bcum.T

    rr = lax.broadcasted_iota(I32, (L, L), 0)
    cc = lax.broadcasted_iota(I32, (L, L), 1)
    causal = rr >= cc
    hg = hg_ref[...]

    for h in range(M_HEADS):
        sl = slice(h * M_DK, (h + 1) * M_DK)
        b_col = bcum[:, M_HEADS + h:M_HEADS + h + 1]
        i_col = gc[:, h:h + 1]
        b_row = bcum_t[M_HEADS + h:M_HEADS + h + 1, :]
        i_row = gc_t[h:h + 1, :]
        m_prev = m_sc[h:h + 1, 0:1]

        log_d = jnp.where(causal, b_col - b_row + i_row, NEG_BIG)
        inter = b_col + m_prev
        m_t = jnp.maximum(jnp.max(log_d, axis=-1, keepdims=True), inter)
        d_w = jnp.exp(log_d - m_t)
        w_inter = jnp.exp(inter - m_t)

        qh = qk[:, sl]
        kh = qk[:, BRANCH_WIDTH + h * M_DK:BRANCH_WIDTH + (h + 1) * M_DK] * (M_DK ** -0.5)
        qb = qh.astype(BF16)
        vb = vo_ref[:, sl]
        c_h = c_sc[h]
        n_row = n_sc[h:h + 1, :]

        s = _dot_nt(qb, kh.astype(BF16)) * d_w
        num = _dot(s.astype(BF16), vb) + w_inter * _dot(qb, c_h.astype(BF16))
        den = jnp.sum(s, axis=-1, keepdims=True) + w_inter * jnp.sum(qh * n_row, axis=-1, keepdims=True)
        hh = num / jnp.maximum(jnp.abs(den), jnp.exp(-m_t))

        b_last = b_col[L - 1:L, :]
        log_w = b_last - b_col + i_col
        m_new = jnp.maximum(b_last + m_prev, jnp.max(log_w, axis=0, keepdims=True))
        decay = jnp.exp(b_last + m_prev - m_new)
        kw = kh * jnp.exp(log_w - m_new)
        c_sc[h] = decay * c_h + _dot_tn(kw.astype(BF16), vb)
        n_sc[h:h + 1, :] = decay * n_row + jnp.sum(kw, axis=0, keepdims=True)
        m_sc[h:h + 1, :] = jnp.broadcast_to(m_new, (1, LANES))

        hn = _rms(hh, hg[:, sl])
        o_pre = vo_ref[:, BRANCH_WIDTH + h * M_DK:BRANCH_WIDTH + (h + 1) * M_DK].astype(F32)
        out_ref[:, sl] = (hn * _sigmoid(o_pre)).astype(BF16)


def _mlstm(proj3, gates3, conv_w, conv_b, gate_bias, head_gain):
    b, s, _ = proj3.shape
    L = M_CHUNK
    return pl.pallas_call(
        _mlstm_kernel,
        grid=(b, s // L),
        in_specs=[
            pl.BlockSpec((None, L, 1024), lambda i, c: (i, c, COL_MQK // 1024)),
            pl.BlockSpec((None, L, 1024), lambda i, c: (i, c, COL_MV // 1024)),
            pl.BlockSpec((None, L, LANES), lambda i, c: (i, c, 0)),
            pl.BlockSpec((M_CONV, 1024), lambda i, c: (0, 0)),
            pl.BlockSpec((1, 1024), lambda i, c: (0, 0)),
            pl.BlockSpec((1, LANES), lambda i, c: (0, 0)),
            pl.BlockSpec((1, BRANCH_WIDTH), lambda i, c: (0, 0)),
        ],
        out_specs=pl.BlockSpec((None, L, BRANCH_WIDTH), lambda i, c: (i, c, 0)),
        out_shape=jax.ShapeDtypeStruct((b, s, BRANCH_WIDTH), BF16),
        scratch_shapes=[
            pltpu.VMEM((M_HEADS, M_DK, M_DK), F32),
            pltpu.VMEM((8, LANES), F32),
            pltpu.VMEM((8, LANES), F32),
            pltpu.VMEM((L, 1024), F32),
        ],
        compiler_params=pltpu.CompilerParams(
            dimension_semantics=("parallel", "arbitrary"), vmem_limit_bytes=VMEM_LIMIT),
        name="mlstm",
    )(proj3, proj3, gates3, conv_w, conv_b, gate_bias, head_gain)


AT_T = 256
AT_TK = 512


def _halfnorm(x, gain):
    lane = lax.broadcasted_iota(I32, (1, LANES), 1)
    lo = lane < A_DQK
    x2 = x * x
    ms_lo = jnp.sum(jnp.where(lo, x2, 0.0), axis=-1, keepdims=True)
    ms_hi = jnp.sum(jnp.where(lo, 0.0, x2), axis=-1, keepdims=True)
    ms = jnp.where(lo, ms_lo, ms_hi) * (1.0 / A_DQK)
    return x * lax.rsqrt(ms + NORM_EPS) * gain


AV_ROWS = A_DV + 16


def _attn_kernel(slopes_ref, q_ref, k_ref, v_ref, qg_ref, kg_ref, lam_ref, hg_ref, out_ref,
                 k1_sc, k2_sc, vt_sc, s_sc, p_sc, acc_sc, *, lam_init):
    T = AT_T
    h = pl.program_id(1)
    qi = pl.program_id(2)
    lane = lax.broadcasted_iota(I32, (1, LANES), 1)
    slope = slopes_ref[h]

    @pl.when(qi == 0)
    def _():
        s_len = k_ref.shape[0]
        kn = _halfnorm(k_ref[...].astype(F32), kg_ref[...])
        pos = lax.broadcasted_iota(I32, (s_len, 1), 0)
        c_loc = (pos & (T - 1)).astype(F32) * slope
        c_blk = (pos >> int(math.log2(T))).astype(F32) * (slope * T)
        k1_sc[...] = jnp.where(lane < A_DQK, kn, jnp.where(lane == A_DQK, c_loc, jnp.where(
            lane == A_DQK + 1, c_blk, 0.0))).astype(BF16)
        k2_sc[...] = jnp.where(lane >= A_DQK, kn, jnp.where(lane == 0, c_loc, jnp.where(
            lane == 1, c_blk, 0.0))).astype(BF16)
        vt_sc[:A_DV, :] = v_ref[...].astype(F32).T.astype(BF16)
        vt_sc[A_DV:, :] = jnp.ones((AV_ROWS - A_DV, s_len), BF16)

    lp = lam_ref[...]
    lam = (jnp.exp(jnp.sum(lp[0:1, :] * lp[1:2, :], axis=-1, keepdims=True))
           - jnp.exp(jnp.sum(lp[2:3, :] * lp[3:4, :], axis=-1, keepdims=True)) + lam_init)

    qn = _halfnorm(q_ref[...].astype(F32), qg_ref[...]) * (A_DQK ** -0.5)
    q1 = jnp.where(lane < A_DQK, qn, jnp.where(lane < A_DQK + 2, 1.0, 0.0)).astype(BF16)
    q2 = jnp.where(lane >= A_DQK, qn, jnp.where(lane < 2, 1.0, 0.0)).astype(BF16)

    TK = AT_TK
    CH = 128

    def scores_to_scratch(j):
        start = pl.multiple_of(j * TK, TK)
        s_sc[0] = _dot_nt(k1_sc[pl.ds(start, TK), :], q1)
        s_sc[1] = _dot_nt(k2_sc[pl.ds(start, TK), :], q2)

    def softmax_to_scratch(j, m, masked):
        m_out, alpha_out = [], []
        for mp in range(2):
            for hh in range(T // LANES):
                idx = mp * (T // LANES) + hh
                lanes = slice(hh * LANES, (hh + 1) * LANES)

                def chunk(c):
                    sc = s_sc[mp, c * CH:(c + 1) * CH, lanes]
                    if masked:
                        key = j * TK + c * CH + lax.broadcasted_iota(I32, (CH, LANES), 0)
                        qry = qi * T + hh * LANES + lax.broadcasted_iota(I32, (CH, LANES), 1)
                        sc = jnp.where(key <= qry, sc, NEG_BIG)
                    return sc

                m_new = m[idx]
                for c in range(TK // CH):
                    m_new = jnp.maximum(m_new, jnp.max(chunk(c), axis=0, keepdims=True))
                for c in range(TK // CH):
                    p_sc[c * CH:(c + 1) * CH, idx * LANES:(idx + 1) * LANES] = (
                        jnp.exp(chunk(c) - m_new).astype(BF16))
                alpha_out.append(jnp.exp(m[idx] - m_new))
                m_out.append(m_new)
        return tuple(m_out), tuple(alpha_out)

    def values_from_scratch(j, alpha):
        vt = vt_sc[:, pl.ds(pl.multiple_of(j * TK, TK), TK)]
        pv = _dot(vt, p_sc[...])
        for idx in range(2 * T // LANES):
            sl = slice(idx * LANES, (idx + 1) * LANES)
            acc_sc[:, sl] = alpha[idx] * acc_sc[:, sl] + pv[:, sl]

    def body(j, carry):
        m, alpha = carry
        values_from_scratch(jnp.maximum(j - 1, 0), alpha)
        m, alpha = softmax_to_scratch(j, m, False)
        scores_to_scratch(j + 1)
        return m, alpha

    n_slab = 2 * T // LANES
    last = (qi * T) >> int(math.log2(TK))
    p_sc[...] = jnp.zeros_like(p_sc)
    acc_sc[...] = jnp.zeros_like(acc_sc)
    scores_to_scratch(0)
    m, alpha = lax.fori_loop(0, last, body, ((jnp.full((1, LANES), NEG_BIG, F32),) * n_slab,
                                             (jnp.ones((1, LANES), F32),) * n_slab))
    values_from_scratch(jnp.maximum(last - 1, 0), alpha)
    _, alpha = softmax_to_scratch(last, m, True)
    values_from_scratch(last, alpha)

    a1 = acc_sc[:, :T]
    a2 = acc_sc[:, T:]
    o_t = a1[:A_DV, :] / a1[A_DV:A_DV + 1, :] - lam * (a2[:A_DV, :] / a2[A_DV:A_DV + 1, :])
    out_ref[...] = (_rms(o_t.T, hg_ref[...]) * (1.0 - lam_init)).astype(BF16)


def _attention(proj3, q_gain, k_gain, lam_params, head_gain, lam_init):
    b, s, _ = proj3.shape
    T = AT_T
    slopes = jnp.asarray(ALIBI_SLOPES, F32)
    return pl.pallas_call(
        functools.partial(_attn_kernel, lam_init=lam_init),
        grid=(b, A_HEADS, s // T),
        in_specs=[
            pl.BlockSpec(memory_space=pltpu.SMEM),
            pl.BlockSpec((None, T, LANES), lambda i, h, q: (i, q, COL_AQ // LANES + h)),
            pl.BlockSpec((None, s, LANES), lambda i, h, q: (i, 0, COL_AK // LANES + h)),
            pl.BlockSpec((None, s, LANES), lambda i, h, q: (i, 0, COL_AV // LANES + h)),
            pl.BlockSpec((1, LANES), lambda i, h, q: (0, 0)),
            pl.BlockSpec((1, LANES), lambda i, h, q: (0, 0)),
            pl.BlockSpec((4, A_DQK), lambda i, h, q: (0, 0)),
            pl.BlockSpec((1, LANES), lambda i, h, q: (0, h)),
        ],
        out_specs=pl.BlockSpec((None, T, LANES), lambda i, h, q: (i, q, h)),
        out_shape=jax.ShapeDtypeStruct((b, s, BRANCH_WIDTH), BF16),
        scratch_shapes=[pltpu.VMEM((s, LANES), BF16), pltpu.VMEM((s, LANES), BF16),
                        pltpu.VMEM((AV_ROWS, s), BF16), pltpu.VMEM((2, AT_TK, T), F32),
                        pltpu.VMEM((AT_TK, 2 * T), BF16), pltpu.VMEM((AV_ROWS, 2 * T), F32)],
        compiler_params=pltpu.CompilerParams(
            dimension_semantics=("parallel", "parallel", "arbitrary"), vmem_limit_bytes=VMEM_LIMIT),
        name="diff_attn",
    )(slopes, proj3, proj3, proj3, q_gain, k_gain, lam_params, head_gain)


MG_TM = 512
MG_HALO = 128


def _merge_kernel(x_ref, hm_ref, ha_ref, u_ref, halo_ref, gm_ref, ga_ref, gp_ref, pw_ref, ps_ref,
                  wb_ref, wo_ref, fg_ref, *rest, seq_tiles, routed):
    if routed:
        wr_ref, br_ref, xo_ref, hf_ref, route_ref = rest
    else:
        xo_ref, hf_ref = rest
    TM = MG_TM
    i = pl.program_id(0)
    seq_tile = i % seq_tiles

    u = u_ref[...]
    halo = jnp.where(seq_tile > 0, halo_ref[...], jnp.zeros_like(halo_ref))
    d_main = (lax.broadcasted_iota(I32, (TM, TM), 0) - lax.broadcasted_iota(I32, (TM, TM), 1))
    d_halo = (lax.broadcasted_iota(I32, (TM, MG_HALO), 0) + MG_HALO
              - lax.broadcasted_iota(I32, (TM, MG_HALO), 1))
    t_pos = seq_tile * TM + lax.broadcasted_iota(I32, (TM, 1), 0)
    ps = ps_ref[...]
    hp_parts = []
    for g, w in enumerate(P_WINDOWS):
        sl = slice(g * P_GC, (g + 1) * P_GC)
        band = jnp.where((d_main >= 0) & (d_main < w), 1.0, 0.0).astype(BF16)
        band_h = jnp.where(d_halo < w, 1.0, 0.0).astype(BF16)
        ug = u[:, sl]
        sums = _dot(band, ug) + _dot(band_h, halo[:, sl])
        cnt = jnp.minimum(t_pos + 1, w).astype(F32)
        pooled = sums / cnt - ug.astype(F32)
        hp_parts.append((_dot(pooled.astype(BF16), pw_ref[g]) * ps[:, sl]).astype(BF16))
    hp = jnp.concatenate(hp_parts, axis=1)

    merged = (_sigmoid(gm_ref[...].astype(F32)) * _dot(hm_ref[...], wb_ref[0])
              + _sigmoid(ga_ref[...].astype(F32)) * _dot(ha_ref[...], wb_ref[1])
              + _sigmoid(gp_ref[...].astype(F32)) * _dot(hp, wb_ref[2]))
    x_new = x_ref[...] + _dot(merged.astype(BF16), wo_ref[...])
    xo_ref[...] = x_new
    hf = _rms(x_new, fg_ref[...])
    hf_ref[...] = hf.astype(BF16)

    if routed:
        logits = jnp.dot(hf, wr_ref[...], preferred_element_type=F32,
                         precision=lax.Precision.HIGHEST) + br_ref[...]
        lane = lax.broadcasted_iota(I32, (TM, LANES), 1)
        lanef = lane.astype(F32)
        lg = jnp.where(lane < N_EXPERTS, logits, NEG_BIG)
        m1 = jnp.max(lg, axis=-1, keepdims=True)
        i1 = jnp.min(jnp.where(lg == m1, lanef, float(LANES)), axis=-1, keepdims=True)
        lg2 = jnp.where(lanef == i1, NEG_BIG, lg)
        m2 = jnp.max(lg2, axis=-1, keepdims=True)
        i2 = jnp.min(jnp.where(lg2 == m2, lanef, float(LANES)), axis=-1, keepdims=True)
        g1 = 1.0 / (1.0 + jnp.exp(m2 - m1))
        g2 = 1.0 - g1
        route_ref[...] = jnp.where(lane == 0, g1, jnp.where(lane == 1, g2, jnp.where(
            lane == 2, i1, jnp.where(lane == 3, i2, 0.0))))


def _merge(x2, hm2, ha2, proj2, pool_w, pool_scale, w_branch, w_out, ffn_gain, seq, router=None):
    tok = x2.shape[0]
    TM = MG_TM
    routed = router is not None
    hb = TM // MG_HALO
    in_specs = [
        pl.BlockSpec((TM, D_MODEL), lambda i: (i, 0)),
        pl.BlockSpec((TM, BRANCH_WIDTH), lambda i: (i, 0)),
        pl.BlockSpec((TM, BRANCH_WIDTH), lambda i: (i, 0)),
        pl.BlockSpec((TM, BRANCH_WIDTH), lambda i: (i, COL_PU // BRANCH_WIDTH)),
        pl.BlockSpec((MG_HALO, BRANCH_WIDTH), lambda i: (jnp.maximum(i * hb - 1, 0), COL_PU // BRANCH_WIDTH)),
        pl.BlockSpec((TM, D_MODEL), lambda i: (i, COL_G // D_MODEL)),
        pl.BlockSpec((TM, D_MODEL), lambda i: (i, COL_G // D_MODEL + 1)),
        pl.BlockSpec((TM, D_MODEL), lambda i: (i, COL_G // D_MODEL + 2)),
        pl.BlockSpec((len(P_WINDOWS), P_GC, P_GC), lambda i: (0, 0, 0)),
        pl.BlockSpec((1, BRANCH_WIDTH), lambda i: (0, 0)),
        pl.BlockSpec((3, BRANCH_WIDTH, D_MODEL), lambda i: (0, 0, 0)),
        pl.BlockSpec((D_MODEL, D_MODEL), lambda i: (0, 0)),
        pl.BlockSpec((1, D_MODEL), lambda i: (0, 0)),
    ]
    args = [x2, hm2, ha2, proj2, proj2, proj2, proj2, proj2, pool_w, pool_scale, w_branch, w_out, ffn_gain]
    out_specs = [pl.BlockSpec((TM, D_MODEL), lambda i: (i, 0)),
                 pl.BlockSpec((TM, D_MODEL), lambda i: (i, 0))]
    out_shape = [jax.ShapeDtypeStruct((tok, D_MODEL), F32), jax.ShapeDtypeStruct((tok, D_MODEL), BF16)]
    if routed:
        in_specs += [pl.BlockSpec((D_MODEL, LANES), lambda i: (0, 0)),
                     pl.BlockSpec((1, LANES), lambda i: (0, 0))]
        args += list(router)
        out_specs.append(pl.BlockSpec((TM, LANES), lambda i: (i, 0)))
        out_shape.append(jax.ShapeDtypeStruct((tok, LANES), F32))
    return pl.pallas_call(
        functools.partial(_merge_kernel, seq_tiles=seq // TM, routed=routed),
        grid=(tok // TM,),
        in_specs=in_specs,
        out_specs=out_specs,
        out_shape=out_shape,
        compiler_params=pltpu.CompilerParams(
            dimension_semantics=("parallel",), vmem_limit_bytes=VMEM_LIMIT),
        name="merge_routed" if routed else "merge",
    )(*args)


def _ple_epilogue(x_new, p, pg_gain, wpg, wpp):
    gate = _sigmoid(_dot(_rms(x_new, pg_gain).astype(BF16), wpg))
    return x_new + gate * _dot(p.astype(BF16), wpp)


FF_TM = 512
FF_TF = 1408


def _ffn_kernel(hf_ref, x_ref, p_ref, wg_ref, wu_ref, wd_ref, pgn_ref, wpg_ref, wpp_ref, out_ref, acc_sc):
    j = pl.program_id(1)
    hf = hf_ref[...]
    g = _dot(hf, wg_ref[...])
    u = _dot(hf, wu_ref[...])
    part = _dot((g * _sigmoid(g) * u).astype(BF16), wd_ref[...])

    @pl.when(j == 0)
    def _():
        acc_sc[...] = part

    @pl.when(j > 0)
    def _():
        acc_sc[...] += part

    @pl.when(j == pl.num_programs(1) - 1)
    def _():
        out_ref[...] = _ple_epilogue(x_ref[...] + acc_sc[...], p_ref[...], pgn_ref[...],
                                     wpg_ref[...], wpp_ref[...])


def _ffn(hf2, x2, p2, w_gu, w_down, ple_gain, wpg, wpp):
    tok = x2.shape[0]
    TM, TF = FF_TM, FF_TF
    nf = D_FF // TF
    return pl.pallas_call(
        _ffn_kernel,
        grid=(tok // TM, nf),
        in_specs=[
            pl.BlockSpec((TM, D_MODEL), lambda i, j: (i, 0)),
            pl.BlockSpec((TM, D_MODEL), lambda i, j: (i, 0)),
            pl.BlockSpec((TM, PLE_DIM), lambda i, j: (i, 0)),
            pl.BlockSpec((D_MODEL, TF), lambda i, j: (0, j)),
            pl.BlockSpec((D_MODEL, TF), lambda i, j: (0, j + nf)),
            pl.BlockSpec((TF, D_MODEL), lambda i, j: (j, 0)),
            pl.BlockSpec((1, D_MODEL), lambda i, j: (0, 0)),
            pl.BlockSpec((D_MODEL, D_MODEL), lambda i, j: (0, 0)),
            pl.BlockSpec((PLE_DIM, D_MODEL), lambda i, j: (0, 0)),
        ],
        out_specs=pl.BlockSpec((TM, D_MODEL), lambda i, j: (i, 0)),
        out_shape=jax.ShapeDtypeStruct((tok, D_MODEL), F32),
        scratch_shapes=[pltpu.VMEM((TM, D_MODEL), F32)],
        compiler_params=pltpu.CompilerParams(
            dimension_semantics=("parallel", "arbitrary"), vmem_limit_bytes=VMEM_LIMIT),
        name="ffn_dense",
    )(hf2, x2, p2, w_gu, w_gu, w_down, ple_gain, wpg, wpp)


MOE_TM = 512
MOE_TF = 512
MOE_CH = 256
CB_TB = 256
CB_ALIGN = 16
CB_W = CB_TB + CB_ALIGN


def _moe_kernel(te_ref, tv_ref, clo_ref, chi_ref, tok_ref, gate_ref, hf_ref, wg_ref, wu_ref, wd_ref,
                y_ref, xb_sc, acc_sc):
    i = pl.program_id(0)
    j = pl.program_id(1)
    last = pl.num_programs(1) - 1
    valid = tv_ref[i] > 0

    @pl.when(valid & (j == 0))
    def _():
        tok_col = tok_ref[...]
        lane = lax.broadcasted_iota(I32, (1, MOE_CH), 1)
        acc_sc[...] = jnp.zeros_like(acc_sc)

        def chunk(c, carry):
            start = pl.multiple_of(c * MOE_CH, MOE_CH)
            onehot = jnp.where(tok_col == lane + start, 1.0, 0.0).astype(BF16)
            acc_sc[...] += _dot(onehot, hf_ref[pl.ds(start, MOE_CH), :])
            return carry

        lax.fori_loop(clo_ref[i], chi_ref[i] + 1, chunk, 0)
        xb_sc[...] = acc_sc[...].astype(BF16)

    @pl.when(valid)
    def _():
        xb = xb_sc[...]
        g = _dot(xb, wg_ref[...])
        u = _dot(xb, wu_ref[...])
        part = _dot((g * _sigmoid(g) * u).astype(BF16), wd_ref[...])

        @pl.when(j == 0)
        def _():
            acc_sc[...] = part

        @pl.when(j > 0)
        def _():
            acc_sc[...] += part

        @pl.when(j == last)
        def _():
            y_ref[...] = (acc_sc[...] * gate_ref[...]).astype(BF16)

    @pl.when(jnp.logical_not(valid) & (j == last))
    def _():
        y_ref[...] = jnp.zeros_like(y_ref)


def _moe(hf2, w_gu, w_down, tile_expert, tile_valid, chunk_lo, chunk_hi, tok_sorted, gate_sorted):
    tok = hf2.shape[0]
    TM, TF = MOE_TM, MOE_TF
    nt = tile_expert.shape[0]
    nf = D_FF_EXPERT // TF
    grid_spec = pltpu.PrefetchScalarGridSpec(
        num_scalar_prefetch=4,
        grid=(nt, nf),
        in_specs=[
            pl.BlockSpec((TM, 1), lambda i, j, te, tv, lo, hi: (i, 0)),
            pl.BlockSpec((TM, 1), lambda i, j, te, tv, lo, hi: (i, 0)),
            pl.BlockSpec((tok, D_MODEL), lambda i, j, te, tv, lo, hi: (0, 0),
                         pipeline_mode=pl.Buffered(1)),
            pl.BlockSpec((None, D_MODEL, TF), lambda i, j, te, tv, lo, hi: (te[i], 0, j)),
            pl.BlockSpec((None, D_MODEL, TF), lambda i, j, te, tv, lo, hi: (te[i], 0, j + nf)),
            pl.BlockSpec((None, TF, D_MODEL), lambda i, j, te, tv, lo, hi: (te[i], j, 0)),
        ],
        out_specs=pl.BlockSpec((TM, D_MODEL), lambda i, j, te, tv, lo, hi: (i, 0)),
        scratch_shapes=[pltpu.VMEM((TM, D_MODEL), BF16), pltpu.VMEM((TM, D_MODEL), F32)],
    )
    return pl.pallas_call(
        _moe_kernel,
        grid_spec=grid_spec,
        out_shape=jax.ShapeDtypeStruct((nt * TM, D_MODEL), BF16),
        compiler_params=pltpu.CompilerParams(
            dimension_semantics=("arbitrary", "arbitrary"), vmem_limit_bytes=VMEM_LIMIT),
        name="moe_experts",
    )(tile_expert, tile_valid, chunk_lo, chunk_hi, tok_sorted, gate_sorted, hf2, w_gu, w_gu, w_down)


def _combine_kernel(w0_ref, x_ref, p_ref, pos_ref, y_hbm, pgn_ref, wpg_ref, wpp_ref, out_ref, win_sc, sem):
    i = pl.program_id(0)

    def window_copy(e):
        start = pl.multiple_of(w0_ref[i * N_EXPERTS + e], CB_ALIGN)
        return pltpu.make_async_copy(y_hbm.at[pl.ds(start, CB_W), :], win_sc.at[e], sem.at[e])

    for e in range(N_EXPERTS):
        window_copy(e).start()

    pos = pos_ref[...]
    pos1 = pos[:, 0:1]
    pos2 = pos[:, 1:2]
    lane = lax.broadcasted_iota(I32, (1, CB_W), 1)
    moe_out = jnp.zeros((CB_TB, D_MODEL), F32)
    for e in range(N_EXPERTS):
        window_copy(e).wait()
        rows = lane + w0_ref[i * N_EXPERTS + e]
        onehot = jnp.where((pos1 == rows) | (pos2 == rows), 1.0, 0.0).astype(BF16)
        moe_out = moe_out + _dot(onehot, win_sc[e])

    out_ref[...] = _ple_epilogue(x_ref[...] + moe_out, p_ref[...], pgn_ref[...], wpg_ref[...], wpp_ref[...])


def _combine(x2, p2, pos, y_sorted, win_start, ple_gain, wpg, wpp):
    tok = x2.shape[0]
    TB = CB_TB
    grid_spec = pltpu.PrefetchScalarGridSpec(
        num_scalar_prefetch=1,
        grid=(tok // TB,),
        in_specs=[
            pl.BlockSpec((TB, D_MODEL), lambda i, w0: (i, 0)),
            pl.BlockSpec((TB, PLE_DIM), lambda i, w0: (i, 0)),
            pl.BlockSpec((TB, 2), lambda i, w0: (i, 0)),
            pl.BlockSpec(memory_space=pl.ANY),
            pl.BlockSpec((1, D_MODEL), lambda i, w0: (0, 0)),
            pl.BlockSpec((D_MODEL, D_MODEL), lambda i, w0: (0, 0)),
            pl.BlockSpec((PLE_DIM, D_MODEL), lambda i, w0: (0, 0)),
        ],
        out_specs=pl.BlockSpec((TB, D_MODEL), lambda i, w0: (i, 0)),
        scratch_shapes=[pltpu.VMEM((N_EXPERTS, CB_W, D_MODEL), BF16),
                        pltpu.SemaphoreType.DMA((N_EXPERTS,))],
    )
    return pl.pallas_call(
        _combine_kernel,
        grid_spec=grid_spec,
        out_shape=jax.ShapeDtypeStruct((tok, D_MODEL), F32),
        compiler_params=pltpu.CompilerParams(
            dimension_semantics=("arbitrary",), vmem_limit_bytes=VMEM_LIMIT),
        name="moe_combine",
    )(win_start, x2, p2, pos, y_sorted, ple_gain, wpg, wpp)


def _route_metadata(route, tok):
    TM = MOE_TM
    nt = (2 * tok) // TM + N_EXPERTS
    rows = nt * TM
    g1, g2 = route[:, 0], route[:, 1]
    i1, i2 = route[:, 2].astype(I32), route[:, 3].astype(I32)
    experts = jnp.arange(N_EXPERTS, dtype=I32)
    member = ((i1[:, None] == experts) | (i2[:, None] == experts)).astype(I32)
    rank = jnp.cumsum(member, axis=0) - member
    counts = jnp.sum(member, axis=0)
    tiles_e = (counts + TM - 1) // TM
    tile_end = jnp.cumsum(tiles_e)
    tile_start = tile_end - tiles_e
    seg_start = tile_start * TM
    pos1 = seg_start[i1] + jnp.take_along_axis(rank, i1[:, None], axis=1)[:, 0]
    pos2 = seg_start[i2] + jnp.take_along_axis(rank, i2[:, None], axis=1)[:, 0]
    t_ids = jnp.arange(tok, dtype=I32)
    both = jnp.concatenate([pos1, pos2])
    tok_sorted = jnp.full((rows,), -1, I32).at[both].set(jnp.concatenate([t_ids, t_ids]))
    gate_sorted = jnp.zeros((rows,), F32).at[both].set(jnp.concatenate([g1, g2]))

    tiles = jnp.arange(nt, dtype=I32)
    n_used = tile_end[-1]
    tile_valid = (tiles < n_used).astype(I32)
    te = jnp.sum((tiles[:, None] >= tile_end[None, :]).astype(I32), axis=1)
    te_last = jnp.sum(((n_used - 1) >= tile_end).astype(I32))
    tile_expert = jnp.where(tile_valid > 0, jnp.minimum(te, N_EXPERTS - 1), te_last).astype(I32)
    real = jnp.clip(counts[tile_expert] - (tiles - tile_start[tile_expert]) * TM, 1, TM)
    first_tok = tok_sorted[tiles * TM]
    last_tok = tok_sorted[tiles * TM + real - 1]
    chunk_lo = jnp.where(tile_valid > 0, first_tok // MOE_CH, 0).astype(I32)
    chunk_hi = jnp.where(tile_valid > 0, last_tok // MOE_CH, -1).astype(I32)

    blk_start = seg_start[None, :] + rank[::CB_TB, :]
    win_start = jnp.minimum((blk_start // CB_ALIGN) * CB_ALIGN, rows - CB_W).astype(I32).reshape(-1)
    pos = jnp.stack([pos1, pos2], axis=1).astype(I32)
    return (tile_expert, tile_valid, chunk_lo, chunk_hi, tok_sorted.reshape(rows, 1),
            gate_sorted.reshape(rows, 1), pos, win_start)


def _pack_w_in(w):
    o = [0, 1024, 1536, 2048, 2052, 2056, 2568, 3080, 3592, 4104, 7176]
    main = jnp.concatenate([w[:, o[0]:o[3]], w[:, o[5]:o[10]]], axis=1).astype(BF16)
    gates = jnp.pad(w[:, o[3]:o[5]], ((0, 0), (0, LANES - 2 * M_HEADS))).astype(BF16)
    return main, gates


def kernel(x, p, attn_norm, w_in, m_conv_w, m_conv_b, m_gate_bias, m_head_norm, a_q_norm, a_k_norm,
           a_lambda, a_head_norm, pool_w, pool_scale, w_branch, w_out, ffn_norm, dense_w_gu, dense_w_down,
           router_w, router_b, moe_w_gu, moe_w_down, ple_norm, ple_w_gate, ple_w_proj):
    b, s, d = x.shape
    depth = w_in.shape[0]
    tok = b * s
    x2 = x.reshape(tok, d)
    for layer in range(depth):
        w_main, w_if = _pack_w_in(w_in[layer])
        proj, gates = _inproj(x2, attn_norm[layer].reshape(1, d), w_main, w_if)
        proj3 = proj.reshape(b, s, PROJ_WIDTH)
        gate_bias = jnp.pad(m_gate_bias[layer], (0, LANES - 2 * M_HEADS)).reshape(1, LANES)
        h_m = _mlstm(proj3, gates.reshape(b, s, LANES), m_conv_w[layer], m_conv_b[layer].reshape(1, -1),
                     gate_bias, m_head_norm[layer].reshape(1, -1))
        lam_init = 0.8 - 0.6 * math.exp(-0.3 * layer)
        h_a = _attention(proj3, jnp.tile(a_q_norm[layer], 2).reshape(1, LANES),
                         jnp.tile(a_k_norm[layer], 2).reshape(1, LANES), a_lambda[layer],
                         a_head_norm[layer].reshape(1, -1), lam_init)
        p2 = p[layer].reshape(tok, PLE_DIM)
        ple_args = (ple_norm[layer].reshape(1, d), ple_w_gate[layer].astype(BF16), ple_w_proj[layer].astype(BF16))
        merge_args = (x2, h_m.reshape(tok, BRANCH_WIDTH), h_a.reshape(tok, BRANCH_WIDTH), proj,
                      pool_w[layer].astype(BF16), pool_scale[layer].reshape(1, -1),
                      w_branch[layer].astype(BF16), w_out[layer].astype(BF16), ffn_norm[layer].reshape(1, d), s)
        j = layer // 2
        if layer % 2 == 0:
            x_mid, hf = _merge(*merge_args)
            x2 = _ffn(hf, x_mid, p2, dense_w_gu[j].astype(BF16), dense_w_down[j].astype(BF16), *ple_args)
        else:
            wr = jnp.pad(router_w[j], ((0, 0), (0, LANES - N_EXPERTS)))
            br = jnp.pad(router_b[j], (0, LANES - N_EXPERTS)).reshape(1, LANES)
            x_mid, hf, route = _merge(*merge_args, router=(wr, br))
            (tile_expert, tile_valid, chunk_lo, chunk_hi, tok_sorted, gate_sorted, pos,
             win_start) = _route_metadata(route, tok)
            y_sorted = _moe(hf, moe_w_gu[j].astype(BF16), moe_w_down[j].astype(BF16), tile_expert,
                            tile_valid, chunk_lo, chunk_hi, tok_sorted, gate_sorted)
            x2 = _combine(x_mid, p2, pos, y_sorted, win_start, *ple_args)
    return x2.reshape(b, s, d)
```

```python
import functools
import math

import jax
import jax.numpy as jnp
from jax import lax
from jax.experimental import pallas as pl
from jax.experimental.pallas import tpu as pltpu

F32 = jnp.float32
BF16 = jnp.bfloat16
I32 = jnp.int32

D_MODEL = 1024
PLE_DIM = 256
NORM_EPS = 1e-6
BRANCH_WIDTH = 512

M_HEADS = 4
M_DK = 128
M_CONV = 4
M_CHUNK = 128

A_HEADS = 4
A_DV = 128
A_DQK = 64
ALIBI_SLOPES = tuple(2.0 ** (-8.0 * (h + 1) / A_HEADS) for h in range(A_HEADS))

P_WINDOWS = (2, 4, 8, 16)
P_GC = 128

D_FF = 2816
N_EXPERTS = 8
D_FF_EXPERT = 3584

PROJ_WIDTH = 7168
COL_MQK, COL_MV, COL_MO, COL_AQ, COL_AK, COL_AV, COL_PU, COL_G = 0, 1024, 1536, 2048, 2560, 3072, 3584, 4096

LANES = 128
NEG_BIG = -1e30

VMEM_LIMIT = 56 * 1024 * 1024


def _sigmoid(x):
    return 0.5 * jnp.tanh(0.5 * x) + 0.5


def _rms(x, gain):
    return x * lax.rsqrt(jnp.mean(x * x, axis=-1, keepdims=True) + NORM_EPS) * gain


def _dot(a, b):
    return jnp.dot(a, b, preferred_element_type=F32)


def _dot_nt(a, b):
    return lax.dot_general(a, b, (((1,), (1,)), ((), ())), preferred_element_type=F32)


def _dot_tn(a, b):
    return lax.dot_general(a, b, (((0,), (0,)), ((), ())), preferred_element_type=F32)


IN_TM = 1024
IN_TN = 1024


def _inproj_kernel(x_ref, gain_ref, w_ref, wif_ref, out_ref, gates_ref, hn_sc):
    @pl.when(pl.program_id(1) == 0)
    def _():
        hn = _rms(x_ref[...], gain_ref[...]).astype(BF16)
        hn_sc[...] = hn
        gates_ref[...] = _dot(hn, wif_ref[...])

    out_ref[...] = _dot(hn_sc[...], w_ref[...]).astype(BF16)


def _inproj(x2, gain, w_main, w_if):
    tok = x2.shape[0]
    return pl.pallas_call(
        _inproj_kernel,
        grid=(tok // IN_TM, PROJ_WIDTH // IN_TN),
        in_specs=[
            pl.BlockSpec((IN_TM, D_MODEL), lambda i, j: (i, 0)),
            pl.BlockSpec((1, D_MODEL), lambda i, j: (0, 0)),
            pl.BlockSpec((D_MODEL, IN_TN), lambda i, j: (0, j)),
            pl.BlockSpec((D_MODEL, LANES), lambda i, j: (0, 0)),
        ],
        out_specs=[
            pl.BlockSpec((IN_TM, IN_TN), lambda i, j: (i, j)),
            pl.BlockSpec((IN_TM, LANES), lambda i, j: (i, 0)),
        ],
        out_shape=[
            jax.ShapeDtypeStruct((tok, PROJ_WIDTH), BF16),
            jax.ShapeDtypeStruct((tok, LANES), F32),
        ],
        scratch_shapes=[pltpu.VMEM((IN_TM, D_MODEL), BF16)],
        compiler_params=pltpu.CompilerParams(
            dimension_semantics=("parallel", "arbitrary"), vmem_limit_bytes=VMEM_LIMIT),
        name="inproj",
    )(x2, gain, w_main, w_if)


def _mlstm_kernel(qk_ref, vo_ref, g_ref, cw_ref, cb_ref, gb_ref, hg_ref, out_ref,
                  c_sc, n_sc, m_sc, prev_sc):
    L = M_CHUNK

    @pl.when(pl.program_id(1) == 0)
    def _():
        c_sc[...] = jnp.zeros_like(c_sc)
        n_sc[...] = jnp.zeros_like(n_sc)
        m_sc[...] = jnp.zeros_like(m_sc)
        prev_sc[...] = jnp.zeros_like(prev_sc)

    x = qk_ref[...].astype(F32)
    prev = prev_sc[...]
    row1 = lax.broadcasted_iota(I32, (L, 1), 0)
    cw = cw_ref[...]
    y = cb_ref[...] + cw[M_CONV - 1:M_CONV, :] * x
    for s in range(1, M_CONV):
        shifted = jnp.where(row1 < s, pltpu.roll(prev, s, 0), pltpu.roll(x, s, 0))
        y = y + cw[M_CONV - 1 - s:M_CONV - s, :] * shifted
    prev_sc[...] = x
    qk = y * _sigmoid(y)

    g = g_ref[...] + gb_ref[...]
    lane = lax.broadcasted_iota(I32, (L, LANES), 1)
    rowl = lax.broadcasted_iota(I32, (L, LANES), 0)
    logf = jnp.minimum(g, 0.0) - jnp.log(1.0 + jnp.exp(-jnp.abs(g)))
    is_f = (lane >= M_HEADS) & (lane < 2 * M_HEADS)
    gc = jnp.where(is_f, logf, jnp.where(lane < M_HEADS, g, 0.0))
    bcum = jnp.where(is_f, logf, 0.0)
    k = 1
    while k < L:
        bcum = bcum + jnp.where(rowl >= k, pltpu.roll(bcum, k, 0), 0.0)
        k *= 2
    gc_t = gc.T
    bcum_t = bcum.T

    rr = lax.broadcasted_iota(I32, (L, L), 0)
    cc = lax.broadcasted_iota(I32, (L, L), 1)
    causal = rr >= cc
    hg = hg_ref[...]

    for h in range(M_HEADS):
        sl = slice(h * M_DK, (h + 1) * M_DK)
        b_col = bcum[:, M_HEADS + h:M_HEADS + h + 1]
        i_col = gc[:, h:h + 1]
        b_row = bcum_t[M_HEADS + h:M_HEADS + h + 1, :]
        i_row = gc_t[h:h + 1, :]
        m_prev = m_sc[h:h + 1, 0:1]

        log_d = jnp.where(causal, b_col - b_row + i_row, NEG_BIG)
        inter = b_col + m_prev
        m_t = jnp.maximum(jnp.max(log_d, axis=-1, keepdims=True), inter)
        d_w = jnp.exp(log_d - m_t)
        w_inter = jnp.exp(inter - m_t)

        qh = qk[:, sl]
        kh = qk[:, BRANCH_WIDTH + h * M_DK:BRANCH_WIDTH + (h + 1) * M_DK] * (M_DK ** -0.5)
        qb = qh.astype(BF16)
        vb = vo_ref[:, sl]
        c_h = c_sc[h]
        n_row = n_sc[h:h + 1, :]

        s = _dot_nt(qb, kh.astype(BF16)) * d_w
        num = _dot(s.astype(BF16), vb) + w_inter * _dot(qb, c_h.astype(BF16))
        den = jnp.sum(s, axis=-1, keepdims=True) + w_inter * jnp.sum(qh * n_row, axis=-1, keepdims=True)
        hh = num / jnp.maximum(jnp.abs(den), jnp.exp(-m_t))

        b_last = b_col[L - 1:L, :]
        log_w = b_last - b_col + i_col
        m_new = jnp.maximum(b_last + m_prev, jnp.max(log_w, axis=0, keepdims=True))
        decay = jnp.exp(b_last + m_prev - m_new)
        kw = kh * jnp.exp(log_w - m_new)
        c_sc[h] = decay * c_h + _dot_tn(kw.astype(BF16), vb)
        n_sc[h:h + 1, :] = decay * n_row + jnp.sum(kw, axis=0, keepdims=True)
        m_sc[h:h + 1, :] = jnp.broadcast_to(m_new, (1, LANES))

        hn = _rms(hh, hg[:, sl])
        o_pre = vo_ref[:, BRANCH_WIDTH + h * M_DK:BRANCH_WIDTH + (h + 1) * M_DK].astype(F32)
        out_ref[:, sl] = (hn * _sigmoid(o_pre)).astype(BF16)


def _mlstm(proj3, gates3, conv_w, conv_b, gate_bias, head_gain):
    b, s, _ = proj3.shape
    L = M_CHUNK
    return pl.pallas_call(
        _mlstm_kernel,
        grid=(b, s // L),
        in_specs=[
            pl.BlockSpec((None, L, 1024), lambda i, c: (i, c, COL_MQK // 1024)),
            pl.BlockSpec((None, L, 1024), lambda i, c: (i, c, COL_MV // 1024)),
            pl.BlockSpec((None, L, LANES), lambda i, c: (i, c, 0)),
            pl.BlockSpec((M_CONV, 1024), lambda i, c: (0, 0)),
            pl.BlockSpec((1, 1024), lambda i, c: (0, 0)),
            pl.BlockSpec((1, LANES), lambda i, c: (0, 0)),
            pl.BlockSpec((1, BRANCH_WIDTH), lambda i, c: (0, 0)),
        ],
        out_specs=pl.BlockSpec((None, L, BRANCH_WIDTH), lambda i, c: (i, c, 0)),
        out_shape=jax.ShapeDtypeStruct((b, s, BRANCH_WIDTH), BF16),
        scratch_shapes=[
            pltpu.VMEM((M_HEADS, M_DK, M_DK), F32),
            pltpu.VMEM((8, LANES), F32),
            pltpu.VMEM((8, LANES), F32),
            pltpu.VMEM((L, 1024), F32),
        ],
        compiler_params=pltpu.CompilerParams(
            dimension_semantics=("parallel", "arbitrary"), vmem_limit_bytes=VMEM_LIMIT),
        name="mlstm",
    )(proj3, proj3, gates3, conv_w, conv_b, gate_bias, head_gain)


AT_T = 256
AT_TK = 512


def _halfnorm(x, gain):
    lane = lax.broadcasted_iota(I32, (1, LANES), 1)
    lo = lane < A_DQK
    x2 = x * x
    ms_lo = jnp.sum(jnp.where(lo, x2, 0.0), axis=-1, keepdims=True)
    ms_hi = jnp.sum(jnp.where(lo, 0.0, x2), axis=-1, keepdims=True)
    ms = jnp.where(lo, ms_lo, ms_hi) * (1.0 / A_DQK)
    return x * lax.rsqrt(ms + NORM_EPS) * gain


AV_ROWS = A_DV + 16


def _attn_kernel(slopes_ref, q_ref, k_ref, v_ref, qg_ref, kg_ref, lam_ref, hg_ref, out_ref,
                 k1_sc, k2_sc, vt_sc, s_sc, p_sc, acc_sc, *, lam_init):
    T = AT_T
    h = pl.program_id(1)
    qi = pl.program_id(2)
    lane = lax.broadcasted_iota(I32, (1, LANES), 1)
    slope = slopes_ref[h]

    @pl.when(qi == 0)
    def _():
        s_len = k_ref.shape[0]
        kn = _halfnorm(k_ref[...].astype(F32), kg_ref[...])
        pos = lax.broadcasted_iota(I32, (s_len, 1), 0)
        c_loc = (pos & (T - 1)).astype(F32) * slope
        c_blk = (pos >> int(math.log2(T))).astype(F32) * (slope * T)
        k1_sc[...] = jnp.where(lane < A_DQK, kn, jnp.where(lane == A_DQK, c_loc, jnp.where(
            lane == A_DQK + 1, c_blk, 0.0))).astype(BF16)
        k2_sc[...] = jnp.where(lane >= A_DQK, kn, jnp.where(lane == 0, c_loc, jnp.where(
            lane == 1, c_blk, 0.0))).astype(BF16)
        vt_sc[:A_DV, :] = v_ref[...].astype(F32).T.astype(BF16)
        vt_sc[A_DV:, :] = jnp.ones((AV_ROWS - A_DV, s_len), BF16)

    lp = lam_ref[...]
    lam = (jnp.exp(jnp.sum(lp[0:1, :] * lp[1:2, :], axis=-1, keepdims=True))
           - jnp.exp(jnp.sum(lp[2:3, :] * lp[3:4, :], axis=-1, keepdims=True)) + lam_init)

    qn = _halfnorm(q_ref[...].astype(F32), qg_ref[...]) * (A_DQK ** -0.5)
    q1 = jnp.where(lane < A_DQK, qn, jnp.where(lane < A_DQK + 2, 1.0, 0.0)).astype(BF16)
    q2 = jnp.where(lane >= A_DQK, qn, jnp.where(lane < 2, 1.0, 0.0)).astype(BF16)

    TK = AT_TK
    CH = 128

    def scores_to_scratch(j):
        start = pl.multiple_of(j * TK, TK)
        s_sc[0] = _dot_nt(k1_sc[pl.ds(start, TK), :], q1)
        s_sc[1] = _dot_nt(k2_sc[pl.ds(start, TK), :], q2)

    def softmax_to_scratch(j, m, masked):
        m_out, alpha_out = [], []
        for mp in range(2):
            for hh in range(T // LANES):
                idx = mp * (T // LANES) + hh
                lanes = slice(hh * LANES, (hh + 1) * LANES)

                def chunk(c):
                    sc = s_sc[mp, c * CH:(c + 1) * CH, lanes]
                    if masked:
                        key = j * TK + c * CH + lax.broadcasted_iota(I32, (CH, LANES), 0)
                        qry = qi * T + hh * LANES + lax.broadcasted_iota(I32, (CH, LANES), 1)
                        sc = jnp.where(key <= qry, sc, NEG_BIG)
                    return sc

                m_new = m[idx]
                for c in range(TK // CH):
                    m_new = jnp.maximum(m_new, jnp.max(chunk(c), axis=0, keepdims=True))
                for c in range(TK // CH):
                    p_sc[c * CH:(c + 1) * CH, idx * LANES:(idx + 1) * LANES] = (
                        jnp.exp(chunk(c) - m_new).astype(BF16))
                alpha_out.append(jnp.exp(m[idx] - m_new))
                m_out.append(m_new)
        return tuple(m_out), tuple(alpha_out)

    def values_from_scratch(j, alpha):
        vt = vt_sc[:, pl.ds(pl.multiple_of(j * TK, TK), TK)]
        pv = _dot(vt, p_sc[...])
        for idx in range(2 * T // LANES):
            sl = slice(idx * LANES, (idx + 1) * LANES)
            acc_sc[:, sl] = alpha[idx] * acc_sc[:, sl] + pv[:, sl]

    def body(j, carry):
        m, alpha = carry
        values_from_scratch(jnp.maximum(j - 1, 0), alpha)
        m, alpha = softmax_to_scratch(j, m, False)
        scores_to_scratch(j + 1)
        return m, alpha

    n_slab = 2 * T // LANES
    last = (qi * T) >> int(math.log2(TK))
    p_sc[...] = jnp.zeros_like(p_sc)
    acc_sc[...] = jnp.zeros_like(acc_sc)
    scores_to_scratch(0)
    m, alpha = lax.fori_loop(0, last, body, ((jnp.full((1, LANES), NEG_BIG, F32),) * n_slab,
                                             (jnp.ones((1, LANES), F32),) * n_slab))
    values_from_scratch(jnp.maximum(last - 1, 0), alpha)
    _, alpha = softmax_to_scratch(last, m, True)
    values_from_scratch(last, alpha)

    a1 = acc_sc[:, :T]
    a2 = acc_sc[:, T:]
    o_t = a1[:A_DV, :] / a1[A_DV:A_DV + 1, :] - lam * (a2[:A_DV, :] / a2[A_DV:A_DV + 1, :])
    out_ref[...] = (_rms(o_t.T, hg_ref[...]) * (1.0 - lam_init)).astype(BF16)


def _attention(proj3, q_gain, k_gain, lam_params, head_gain, lam_init):
    b, s, _ = proj3.shape
    T = AT_T
    slopes = jnp.asarray(ALIBI_SLOPES, F32)
    return pl.pallas_call(
        functools.partial(_attn_kernel, lam_init=lam_init),
        grid=(b, A_HEADS, s // T),
        in_specs=[
            pl.BlockSpec(memory_space=pltpu.SMEM),
            pl.BlockSpec((None, T, LANES), lambda i, h, q: (i, q, COL_AQ // LANES + h)),
            pl.BlockSpec((None, s, LANES), lambda i, h, q: (i, 0, COL_AK // LANES + h)),
            pl.BlockSpec((None, s, LANES), lambda i, h, q: (i, 0, COL_AV // LANES + h)),
            pl.BlockSpec((1, LANES), lambda i, h, q: (0, 0)),
            pl.BlockSpec((1, LANES), lambda i, h, q: (0, 0)),
            pl.BlockSpec((4, A_DQK), lambda i, h, q: (0, 0)),
            pl.BlockSpec((1, LANES), lambda i, h, q: (0, h)),
        ],
        out_specs=pl.BlockSpec((None, T, LANES), lambda i, h, q: (i, q, h)),
        out_shape=jax.ShapeDtypeStruct((b, s, BRANCH_WIDTH), BF16),
        scratch_shapes=[pltpu.VMEM((s, LANES), BF16), pltpu.VMEM((s, LANES), BF16),
                        pltpu.VMEM((AV_ROWS, s), BF16), pltpu.VMEM((2, AT_TK, T), F32),
                        pltpu.VMEM((AT_TK, 2 * T), BF16), pltpu.VMEM((AV_ROWS, 2 * T), F32)],
        compiler_params=pltpu.CompilerParams(
            dimension_semantics=("parallel", "parallel", "arbitrary"), vmem_limit_bytes=VMEM_LIMIT),
        name="diff_attn",
    )(slopes, proj3, proj3, proj3, q_gain, k_gain, lam_params, head_gain)


MG_TM = 512
MG_HALO = 128
ROUTER_ROWS = 16
ROUTE_ROWS = 8


def _merge_kernel(x_ref, hm_ref, ha_ref, u_ref, halo_ref, gm_ref, ga_ref, gp_ref, pw_ref, ps_ref,
                  wb_ref, wo_ref, fg_ref, *rest, seq_tiles, routed):
    if routed:
        wr_ref, br_ref, xo_ref, hf_ref, route_ref = rest
    else:
        xo_ref, hf_ref = rest
    TM = MG_TM
    i = pl.program_id(0)
    seq_tile = i % seq_tiles

    u = u_ref[...]
    halo = jnp.where(seq_tile > 0, halo_ref[...], jnp.zeros_like(halo_ref))
    d_main = (lax.broadcasted_iota(I32, (TM, TM), 0) - lax.broadcasted_iota(I32, (TM, TM), 1))
    d_halo = (lax.broadcasted_iota(I32, (TM, MG_HALO), 0) + MG_HALO
              - lax.broadcasted_iota(I32, (TM, MG_HALO), 1))
    t_pos = seq_tile * TM + lax.broadcasted_iota(I32, (TM, 1), 0)
    ps = ps_ref[...]
    hp_parts = []
    for g, w in enumerate(P_WINDOWS):
        sl = slice(g * P_GC, (g + 1) * P_GC)
        band = jnp.where((d_main >= 0) & (d_main < w), 1.0, 0.0).astype(BF16)
        band_h = jnp.where(d_halo < w, 1.0, 0.0).astype(BF16)
        ug = u[:, sl]
        sums = _dot(band, ug) + _dot(band_h, halo[:, sl])
        cnt = jnp.minimum(t_pos + 1, w).astype(F32)
        pooled = sums / cnt - ug.astype(F32)
        hp_parts.append((_dot(pooled.astype(BF16), pw_ref[g]) * ps[:, sl]).astype(BF16))
    hp = jnp.concatenate(hp_parts, axis=1)

    merged = (_sigmoid(gm_ref[...].astype(F32)) * _dot(hm_ref[...], wb_ref[0])
              + _sigmoid(ga_ref[...].astype(F32)) * _dot(ha_ref[...], wb_ref[1])
              + _sigmoid(gp_ref[...].astype(F32)) * _dot(hp, wb_ref[2]))
    x_new = x_ref[...] + _dot(merged.astype(BF16), wo_ref[...])
    xo_ref[...] = x_new
    hf = _rms(x_new, fg_ref[...])
    hf_ref[...] = hf.astype(BF16)

    if routed:
        hf_hi = hf.astype(BF16)
        hf_lo = (hf - hf_hi.astype(F32)).astype(BF16)
        wr = wr_ref[...]
        wr_hi = wr.astype(BF16)
        wr_lo = (wr - wr_hi.astype(F32)).astype(BF16)
        logits = (_dot_nt(wr_hi, hf_hi) + _dot_nt(wr_lo, hf_hi) + _dot_nt(wr_hi, hf_lo))[:N_EXPERTS, :]
        logits = logits + br_ref[...]
        expert = lax.broadcasted_iota(I32, (N_EXPERTS, TM), 0).astype(F32)
        m1 = jnp.max(logits, axis=0, keepdims=True)
        i1 = jnp.min(jnp.where(logits == m1, expert, float(N_EXPERTS)), axis=0, keepdims=True)
        rest_l = jnp.where(expert == i1, NEG_BIG, logits)
        m2 = jnp.max(rest_l, axis=0, keepdims=True)
        i2 = jnp.min(jnp.where(rest_l == m2, expert, float(N_EXPERTS)), axis=0, keepdims=True)
        g1 = 1.0 / (1.0 + jnp.exp(m2 - m1))
        route_ref[...] = jnp.concatenate(
            [g1, 1.0 - g1, i1, i2, jnp.zeros((ROUTE_ROWS - 4, TM), F32)], axis=0)


def _merge(x2, hm2, ha2, proj2, pool_w, pool_scale, w_branch, w_out, ffn_gain, seq, router=None):
    tok = x2.shape[0]
    TM = MG_TM
    routed = router is not None
    hb = TM // MG_HALO
    in_specs = [
        pl.BlockSpec((TM, D_MODEL), lambda i: (i, 0)),
        pl.BlockSpec((TM, BRANCH_WIDTH), lambda i: (i, 0)),
        pl.BlockSpec((TM, BRANCH_WIDTH), lambda i: (i, 0)),
        pl.BlockSpec((TM, BRANCH_WIDTH), lambda i: (i, COL_PU // BRANCH_WIDTH)),
        pl.BlockSpec((MG_HALO, BRANCH_WIDTH), lambda i: (jnp.maximum(i * hb - 1, 0), COL_PU // BRANCH_WIDTH)),
        pl.BlockSpec((TM, D_MODEL), lambda i: (i, COL_G // D_MODEL)),
        pl.BlockSpec((TM, D_MODEL), lambda i: (i, COL_G // D_MODEL + 1)),
        pl.BlockSpec((TM, D_MODEL), lambda i: (i, COL_G // D_MODEL + 2)),
        pl.BlockSpec((len(P_WINDOWS), P_GC, P_GC), lambda i: (0, 0, 0)),
        pl.BlockSpec((1, BRANCH_WIDTH), lambda i: (0, 0)),
        pl.BlockSpec((3, BRANCH_WIDTH, D_MODEL), lambda i: (0, 0, 0)),
        pl.BlockSpec((D_MODEL, D_MODEL), lambda i: (0, 0)),
        pl.BlockSpec((1, D_MODEL), lambda i: (0, 0)),
    ]
    args = [x2, hm2, ha2, proj2, proj2, proj2, proj2, proj2, pool_w, pool_scale, w_branch, w_out, ffn_gain]
    out_specs = [pl.BlockSpec((TM, D_MODEL), lambda i: (i, 0)),
                 pl.BlockSpec((TM, D_MODEL), lambda i: (i, 0))]
    out_shape = [jax.ShapeDtypeStruct((tok, D_MODEL), F32), jax.ShapeDtypeStruct((tok, D_MODEL), BF16)]
    if routed:
        in_specs += [pl.BlockSpec((ROUTER_ROWS, D_MODEL), lambda i: (0, 0)),
                     pl.BlockSpec((N_EXPERTS, 1), lambda i: (0, 0))]
        args += list(router)
        out_specs.append(pl.BlockSpec((ROUTE_ROWS, TM), lambda i: (0, i)))
        out_shape.append(jax.ShapeDtypeStruct((ROUTE_ROWS, tok), F32))
    return pl.pallas_call(
        functools.partial(_merge_kernel, seq_tiles=seq // TM, routed=routed),
        grid=(tok // TM,),
        in_specs=in_specs,
        out_specs=out_specs,
        out_shape=out_shape,
        compiler_params=pltpu.CompilerParams(
            dimension_semantics=("parallel",), vmem_limit_bytes=VMEM_LIMIT),
        name="merge_routed" if routed else "merge",
    )(*args)


def _ple_epilogue(x_new, p, pg_gain, wpg, wpp):
    gate = _sigmoid(_dot(_rms(x_new, pg_gain).astype(BF16), wpg))
    return x_new + gate * _dot(p.astype(BF16), wpp)


FF_TM = 512
FF_TF = 1408


def _ffn_kernel(hf_ref, x_ref, p_ref, wg_ref, wu_ref, wd_ref, pgn_ref, wpg_ref, wpp_ref, out_ref, acc_sc):
    j = pl.program_id(1)

    @pl.when(j == 0)
    def _():
        acc_sc[...] = jnp.zeros_like(acc_sc)

    hf = hf_ref[...]
    g = _dot(hf, wg_ref[...])
    u = _dot(hf, wu_ref[...])
    acc_sc[...] += _dot((g * _sigmoid(g) * u).astype(BF16), wd_ref[...])

    @pl.when(j == pl.num_programs(1) - 1)
    def _():
        out_ref[...] = _ple_epilogue(x_ref[...] + acc_sc[...], p_ref[...], pgn_ref[...],
                                     wpg_ref[...], wpp_ref[...])


def _ffn(hf2, x2, p2, w_gu, w_down, ple_gain, wpg, wpp):
    tok = x2.shape[0]
    TM, TF = FF_TM, FF_TF
    nf = D_FF // TF
    return pl.pallas_call(
        _ffn_kernel,
        grid=(tok // TM, nf),
        in_specs=[
            pl.BlockSpec((TM, D_MODEL), lambda i, j: (i, 0)),
            pl.BlockSpec((TM, D_MODEL), lambda i, j: (i, 0)),
            pl.BlockSpec((TM, PLE_DIM), lambda i, j: (i, 0)),
            pl.BlockSpec((D_MODEL, TF), lambda i, j: (0, j)),
            pl.BlockSpec((D_MODEL, TF), lambda i, j: (0, j + nf)),
            pl.BlockSpec((TF, D_MODEL), lambda i, j: (j, 0)),
            pl.BlockSpec((1, D_MODEL), lambda i, j: (0, 0)),
            pl.BlockSpec((D_MODEL, D_MODEL), lambda i, j: (0, 0)),
            pl.BlockSpec((PLE_DIM, D_MODEL), lambda i, j: (0, 0)),
        ],
        out_specs=pl.BlockSpec((TM, D_MODEL), lambda i, j: (i, 0)),
        out_shape=jax.ShapeDtypeStruct((tok, D_MODEL), F32),
        scratch_shapes=[pltpu.VMEM((TM, D_MODEL), F32)],
        compiler_params=pltpu.CompilerParams(
            dimension_semantics=("parallel", "arbitrary"), vmem_limit_bytes=VMEM_LIMIT),
        name="ffn_dense",
    )(hf2, x2, p2, w_gu, w_gu, w_down, ple_gain, wpg, wpp)


MOE_TM = 1024
MOE_TF = 512
DP_TM = 256
DP_CH = 256
CB_TB = 256
CB_ALIGN = 16
CB_W = CB_TB + CB_ALIGN


def _dispatch_kernel(clo_ref, chi_ref, pos_ref, gate_ref, hf_ref, x_ref, g_ref, acc_sc, gacc_sc):
    i = pl.program_id(0)
    rows = i * DP_TM + lax.broadcasted_iota(I32, (DP_TM, 1), 0)
    acc_sc[...] = jnp.zeros_like(acc_sc)
    gacc_sc[...] = jnp.zeros_like(gacc_sc)

    def chunk(c, carry):
        start = pl.multiple_of(c * DP_CH, DP_CH)
        pos = pos_ref[:, pl.ds(start, DP_CH)]
        gate = gate_ref[:, pl.ds(start, DP_CH)]
        hit1 = pos[0:1, :] == rows
        hit2 = pos[1:2, :] == rows
        onehot = jnp.where(hit1, 1.0, jnp.where(hit2, 1.0, 0.0)).astype(BF16)
        acc_sc[...] += _dot(onehot, hf_ref[pl.ds(start, DP_CH), :])
        gacc_sc[...] += jnp.sum(jnp.where(hit1, gate[0:1, :], jnp.where(hit2, gate[1:2, :], 0.0)),
                                axis=-1, keepdims=True)
        return carry

    lax.fori_loop(clo_ref[i], chi_ref[i] + 1, chunk, 0)
    x_ref[...] = acc_sc[...].astype(BF16)
    g_ref[...] = gacc_sc[...]


def _dispatch(hf2, pos_t, gate_t, chunk_lo, chunk_hi):
    tok = hf2.shape[0]
    ns = chunk_lo.shape[0]
    grid_spec = pltpu.PrefetchScalarGridSpec(
        num_scalar_prefetch=2,
        grid=(ns,),
        in_specs=[
            pl.BlockSpec((2, tok), lambda i, lo, hi: (0, 0)),
            pl.BlockSpec((2, tok), lambda i, lo, hi: (0, 0)),
            pl.BlockSpec((tok, D_MODEL), lambda i, lo, hi: (0, 0), pipeline_mode=pl.Buffered(1)),
        ],
        out_specs=[pl.BlockSpec((DP_TM, D_MODEL), lambda i, lo, hi: (i, 0)),
                   pl.BlockSpec((DP_TM, 1), lambda i, lo, hi: (i, 0))],
        scratch_shapes=[pltpu.VMEM((DP_TM, D_MODEL), F32), pltpu.VMEM((DP_TM, 1), F32)],
    )
    return pl.pallas_call(
        _dispatch_kernel,
        grid_spec=grid_spec,
        out_shape=[jax.ShapeDtypeStruct((ns * DP_TM, D_MODEL), BF16),
                   jax.ShapeDtypeStruct((ns * DP_TM, 1), F32)],
        compiler_params=pltpu.CompilerParams(
            dimension_semantics=("arbitrary",), vmem_limit_bytes=VMEM_LIMIT),
        name="moe_dispatch",
    )(chunk_lo, chunk_hi, pos_t, gate_t, hf2)


def _moe_kernel(te_ref, tv_ref, x_ref, gate_ref, wg_ref, wu_ref, wd_ref, y_ref, acc_sc):
    i = pl.program_id(0)
    j = pl.program_id(1)
    last = pl.num_programs(1) - 1
    valid = tv_ref[i] > 0

    @pl.when(valid & (j == 0))
    def _():
        acc_sc[...] = jnp.zeros_like(acc_sc)

    @pl.when(valid)
    def _():
        xb = x_ref[...]
        g = _dot(xb, wg_ref[...].astype(BF16))
        u = _dot(xb, wu_ref[...].astype(BF16))
        acc_sc[...] += _dot((g * _sigmoid(g) * u).astype(BF16), wd_ref[...].astype(BF16))

        @pl.when(j == last)
        def _():
            y_ref[...] = (acc_sc[...] * gate_ref[...]).astype(BF16)

    @pl.when(jnp.logical_not(valid) & (j == last))
    def _():
        y_ref[...] = jnp.zeros_like(y_ref)


def _moe(x_sorted, gate_sorted, w_gu, w_down, tile_expert, tile_valid):
    TM, TF = MOE_TM, MOE_TF
    nt = tile_expert.shape[0]
    nf = D_FF_EXPERT // TF

    def col(j, tv, i):
        return jnp.where(tv[i] > 0, j, nf - 1)

    grid_spec = pltpu.PrefetchScalarGridSpec(
        num_scalar_prefetch=2,
        grid=(nt, nf),
        in_specs=[
            pl.BlockSpec((TM, D_MODEL), lambda i, j, te, tv: (i, 0)),
            pl.BlockSpec((TM, 1), lambda i, j, te, tv: (i, 0)),
            pl.BlockSpec((None, D_MODEL, TF), lambda i, j, te, tv: (te[i], 0, col(j, tv, i))),
            pl.BlockSpec((None, D_MODEL, TF), lambda i, j, te, tv: (te[i], 0, col(j, tv, i) + nf)),
            pl.BlockSpec((None, TF, D_MODEL), lambda i, j, te, tv: (te[i], col(j, tv, i), 0)),
        ],
        out_specs=pl.BlockSpec((TM, D_MODEL), lambda i, j, te, tv: (i, 0)),
        scratch_shapes=[pltpu.VMEM((TM, D_MODEL), F32)],
    )
    return pl.pallas_call(
        _moe_kernel,
        grid_spec=grid_spec,
        out_shape=jax.ShapeDtypeStruct((nt * TM, D_MODEL), BF16),
        compiler_params=pltpu.CompilerParams(
            dimension_semantics=("arbitrary", "arbitrary"), vmem_limit_bytes=VMEM_LIMIT),
        name="moe_experts",
    )(tile_expert, tile_valid, x_sorted, gate_sorted, w_gu, w_gu, w_down)


def _combine_kernel(w0_ref, x_ref, p_ref, pos_ref, y_hbm, pgn_ref, wpg_ref, wpp_ref, out_ref, win_sc, sem):
    i = pl.program_id(0)
    slot = i & 1

    def window_copy(step, buf, e):
        start = pl.multiple_of(w0_ref[step * N_EXPERTS + e], CB_ALIGN)
        return pltpu.make_async_copy(y_hbm.at[pl.ds(start, CB_W), :], win_sc.at[buf, e], sem.at[buf, e])

    @pl.when(i == 0)
    def _():
        for e in range(N_EXPERTS):
            window_copy(0, 0, e).start()

    @pl.when(i + 1 < pl.num_programs(0))
    def _():
        for e in range(N_EXPERTS):
            window_copy(i + 1, 1 - slot, e).start()

    pos = pos_ref[...]
    pos1 = pos[:, 0:1]
    pos2 = pos[:, 1:2]
    lane = lax.broadcasted_iota(I32, (1, CB_W), 1)
    moe_out = jnp.zeros((CB_TB, D_MODEL), F32)
    for e in range(N_EXPERTS):
        window_copy(i, slot, e).wait()
        rows = lane + w0_ref[i * N_EXPERTS + e]
        onehot = jnp.where(pos1 == rows, 1.0, jnp.where(pos2 == rows, 1.0, 0.0)).astype(BF16)
        moe_out = moe_out + _dot(onehot, win_sc[slot, e])

    out_ref[...] = _ple_epilogue(x_ref[...] + moe_out, p_ref[...], pgn_ref[...], wpg_ref[...], wpp_ref[...])


def _combine(x2, p2, pos, y_sorted, win_start, ple_gain, wpg, wpp):
    tok = x2.shape[0]
    TB = CB_TB
    grid_spec = pltpu.PrefetchScalarGridSpec(
        num_scalar_prefetch=1,
        grid=(tok // TB,),
        in_specs=[
            pl.BlockSpec((TB, D_MODEL), lambda i, w0: (i, 0)),
            pl.BlockSpec((TB, PLE_DIM), lambda i, w0: (i, 0)),
            pl.BlockSpec((TB, 2), lambda i, w0: (i, 0)),
            pl.BlockSpec(memory_space=pl.ANY),
            pl.BlockSpec((1, D_MODEL), lambda i, w0: (0, 0)),
            pl.BlockSpec((D_MODEL, D_MODEL), lambda i, w0: (0, 0)),
            pl.BlockSpec((PLE_DIM, D_MODEL), lambda i, w0: (0, 0)),
        ],
        out_specs=pl.BlockSpec((TB, D_MODEL), lambda i, w0: (i, 0)),
        scratch_shapes=[pltpu.VMEM((2, N_EXPERTS, CB_W, D_MODEL), BF16),
                        pltpu.SemaphoreType.DMA((2, N_EXPERTS))],
    )
    return pl.pallas_call(
        _combine_kernel,
        grid_spec=grid_spec,
        out_shape=jax.ShapeDtypeStruct((tok, D_MODEL), F32),
        compiler_params=pltpu.CompilerParams(
            dimension_semantics=("arbitrary",), vmem_limit_bytes=VMEM_LIMIT),
        name="moe_combine",
    )(win_start, x2, p2, pos, y_sorted, ple_gain, wpg, wpp)


def _route_metadata(route, tok):
    TM = MOE_TM
    nt = (2 * tok) // TM + N_EXPERTS
    rows = nt * TM
    g1, g2 = route[0], route[1]
    i1, i2 = route[2].astype(I32), route[3].astype(I32)
    experts = jnp.arange(N_EXPERTS, dtype=I32)
    member = ((i1[:, None] == experts) | (i2[:, None] == experts)).astype(I32)
    rank = jnp.cumsum(member, axis=0) - member
    counts = jnp.sum(member, axis=0)
    tiles_e = (counts + TM - 1) // TM
    tile_end = jnp.cumsum(tiles_e)
    tile_start = tile_end - tiles_e
    seg_start = tile_start * TM
    sel1 = i1[:, None] == experts
    sel2 = i2[:, None] == experts
    pos1 = jnp.sum(jnp.where(sel1, seg_start[None, :] + rank, 0), axis=1).astype(I32)
    pos2 = jnp.sum(jnp.where(sel2, seg_start[None, :] + rank, 0), axis=1).astype(I32)

    tiles = jnp.arange(nt, dtype=I32)
    n_used = tile_end[-1]
    tile_valid = (tiles < n_used).astype(I32)
    te = jnp.sum((tiles[:, None] >= tile_end[None, :]).astype(I32), axis=1)
    te_last = jnp.sum(((n_used - 1) >= tile_end).astype(I32))
    tile_expert = jnp.where(tile_valid > 0, jnp.minimum(te, N_EXPERTS - 1), te_last).astype(I32)

    steps = jnp.arange(rows // DP_TM, dtype=I32)
    step_tile = steps // (TM // DP_TM)
    step_valid = step_tile < n_used
    step_expert = jnp.minimum(jnp.sum((step_tile[:, None] >= tile_end[None, :]).astype(I32), axis=1),
                              N_EXPERTS - 1)
    pick = step_expert[:, None] == experts[None, :]
    rank_lo = steps * DP_TM - jnp.sum(jnp.where(pick, seg_start[None, :], 0), axis=1)
    rank_hi = jnp.minimum(rank_lo + DP_TM, jnp.sum(jnp.where(pick, counts[None, :], 0), axis=1)) - 1
    before = jnp.sum(jnp.where(pick[:, :, None], rank[::DP_CH, :].T[None, :, :], 0), axis=1)
    chunk_lo = jnp.sum((before <= rank_lo[:, None]).astype(I32), axis=1) - 1
    chunk_hi = jnp.sum((before <= rank_hi[:, None]).astype(I32), axis=1) - 1
    live = step_valid & (rank_hi >= rank_lo)
    chunk_lo = jnp.where(live, chunk_lo, 0).astype(I32)
    chunk_hi = jnp.where(live, chunk_hi, -1).astype(I32)

    blk_start = seg_start[None, :] + rank[::CB_TB, :]
    win_start = jnp.minimum((blk_start // CB_ALIGN) * CB_ALIGN, rows - CB_W).astype(I32).reshape(-1)
    pos = jnp.stack([pos1, pos2], axis=1)
    return (tile_expert, tile_valid, chunk_lo, chunk_hi, jnp.stack([pos1, pos2]), jnp.stack([g1, g2]),
            pos, win_start)


def _pack_w_in(w):
    o = [0, 1024, 1536, 2048, 2052, 2056, 2568, 3080, 3592, 4104, 7176]
    main = jnp.concatenate([w[:, o[0]:o[3]], w[:, o[5]:o[10]]], axis=1).astype(BF16)
    gates = jnp.pad(w[:, o[3]:o[5]], ((0, 0), (0, LANES - 2 * M_HEADS))).astype(BF16)
    return main, gates


def kernel(x, p, attn_norm, w_in, m_conv_w, m_conv_b, m_gate_bias, m_head_norm, a_q_norm, a_k_norm,
           a_lambda, a_head_norm, pool_w, pool_scale, w_branch, w_out, ffn_norm, dense_w_gu, dense_w_down,
           router_w, router_b, moe_w_gu, moe_w_down, ple_norm, ple_w_gate, ple_w_proj):
    b, s, d = x.shape
    depth = w_in.shape[0]
    tok = b * s
    x2 = x.reshape(tok, d)
    for layer in range(depth):
        w_main, w_if = _pack_w_in(w_in[layer])
        proj, gates = _inproj(x2, attn_norm[layer].reshape(1, d), w_main, w_if)
        proj3 = proj.reshape(b, s, PROJ_WIDTH)
        gate_bias = jnp.pad(m_gate_bias[layer], (0, LANES - 2 * M_HEADS)).reshape(1, LANES)
        h_m = _mlstm(proj3, gates.reshape(b, s, LANES), m_conv_w[layer], m_conv_b[layer].reshape(1, -1),
                     gate_bias, m_head_norm[layer].reshape(1, -1))
        lam_init = 0.8 - 0.6 * math.exp(-0.3 * layer)
        h_a = _attention(proj3, jnp.tile(a_q_norm[layer], 2).reshape(1, LANES),
                         jnp.tile(a_k_norm[layer], 2).reshape(1, LANES), a_lambda[layer],
                         a_head_norm[layer].reshape(1, -1), lam_init)
        p2 = p[layer].reshape(tok, PLE_DIM)
        ple_args = (ple_norm[layer].reshape(1, d), ple_w_gate[layer].astype(BF16), ple_w_proj[layer].astype(BF16))
        merge_args = (x2, h_m.reshape(tok, BRANCH_WIDTH), h_a.reshape(tok, BRANCH_WIDTH), proj,
                      pool_w[layer].astype(BF16), pool_scale[layer].reshape(1, -1),
                      w_branch[layer].astype(BF16), w_out[layer].astype(BF16), ffn_norm[layer].reshape(1, d), s)
        j = layer // 2
        if layer % 2 == 0:
            x_mid, hf = _merge(*merge_args)
            x2 = _ffn(hf, x_mid, p2, dense_w_gu[j].astype(BF16), dense_w_down[j].astype(BF16), *ple_args)
        else:
            wr = jnp.pad(router_w[j].T, ((0, ROUTER_ROWS - N_EXPERTS), (0, 0)))
            br = router_b[j].reshape(N_EXPERTS, 1)
            x_mid, hf, route = _merge(*merge_args, router=(wr, br))
            (tile_expert, tile_valid, chunk_lo, chunk_hi, pos_t, gate_t, pos,
             win_start) = _route_metadata(route, tok)
            x_sorted, gate_sorted = _dispatch(hf, pos_t, gate_t, chunk_lo, chunk_hi)
            y_sorted = _moe(x_sorted, gate_sorted, moe_w_gu[j], moe_w_down[j], tile_expert, tile_valid)
            x2 = _combine(x_mid, p2, pos, y_sorted, win_start, *ple_args)
    return x2.reshape(b, s, d)
```

```python
import functools
import math

import jax
import jax.numpy as jnp
from jax import lax
from jax.experimental import pallas as pl
from jax.experimental.pallas import tpu as pltpu

F32 = jnp.float32
BF16 = jnp.bfloat16
I32 = jnp.int32

D_MODEL = 1024
PLE_DIM = 256
NORM_EPS = 1e-6
BRANCH_WIDTH = 512

M_HEADS = 4
M_DK = 128
M_CONV = 4
M_CHUNK = 128

A_HEADS = 4
A_DV = 128
A_DQK = 64
ALIBI_SLOPES = tuple(2.0 ** (-8.0 * (h + 1) / A_HEADS) for h in range(A_HEADS))

P_WINDOWS = (2, 4, 8, 16)
P_GC = 128

D_FF = 2816
N_EXPERTS = 8
D_FF_EXPERT = 3584

PROJ_WIDTH = 7168
COL_MQK, COL_MV, COL_MO, COL_AQ, COL_AK, COL_AV, COL_PU, COL_G = 0, 1024, 1536, 2048, 2560, 3072, 3584, 4096

LANES = 128
NEG_BIG = -1e30

VMEM_LIMIT = 56 * 1024 * 1024


def _sigmoid(x):
    return 0.5 * jnp.tanh(0.5 * x) + 0.5


def _rms(x, gain):
    return x * lax.rsqrt(jnp.mean(x * x, axis=-1, keepdims=True) + NORM_EPS) * gain


def _dot(a, b):
    return jnp.dot(a, b, preferred_element_type=F32)


def _dot_nt(a, b):
    return lax.dot_general(a, b, (((1,), (1,)), ((), ())), preferred_element_type=F32)


def _dot_tn(a, b):
    return lax.dot_general(a, b, (((0,), (0,)), ((), ())), preferred_element_type=F32)


IN_TM = 1024
IN_TN = 1024


def _inproj_kernel(x_ref, gain_ref, w_ref, wif_ref, out_ref, gates_ref, hn_sc):
    @pl.when(pl.program_id(1) == 0)
    def _():
        hn = _rms(x_ref[...], gain_ref[...]).astype(BF16)
        hn_sc[...] = hn
        gates_ref[...] = _dot(hn, wif_ref[...])

    out_ref[...] = _dot(hn_sc[...], w_ref[...]).astype(BF16)


def _inproj(x2, gain, w_main, w_if):
    tok = x2.shape[0]
    return pl.pallas_call(
        _inproj_kernel,
        grid=(tok // IN_TM, PROJ_WIDTH // IN_TN),
        in_specs=[
            pl.BlockSpec((IN_TM, D_MODEL), lambda i, j: (i, 0)),
            pl.BlockSpec((1, D_MODEL), lambda i, j: (0, 0)),
            pl.BlockSpec((D_MODEL, IN_TN), lambda i, j: (0, j)),
            pl.BlockSpec((D_MODEL, LANES), lambda i, j: (0, 0)),
        ],
        out_specs=[
            pl.BlockSpec((IN_TM, IN_TN), lambda i, j: (i, j)),
            pl.BlockSpec((IN_TM, LANES), lambda i, j: (i, 0)),
        ],
        out_shape=[
            jax.ShapeDtypeStruct((tok, PROJ_WIDTH), BF16),
            jax.ShapeDtypeStruct((tok, LANES), F32),
        ],
        scratch_shapes=[pltpu.VMEM((IN_TM, D_MODEL), BF16)],
        compiler_params=pltpu.CompilerParams(
            dimension_semantics=("parallel", "arbitrary"), vmem_limit_bytes=VMEM_LIMIT),
        name="inproj",
    )(x2, gain, w_main, w_if)


ML_NB = 2


def _mlstm_kernel(qk_ref, vo_ref, g_ref, cw_ref, cb_ref, gb_ref, hg_ref, out_ref,
                  c_sc, m_sc, prev_sc):
    @pl.when(pl.program_id(1) == 0)
    def _():
        c_sc[...] = jnp.zeros_like(c_sc)
        m_sc[...] = jnp.zeros_like(m_sc)
        prev_sc[...] = jnp.zeros_like(prev_sc)

    L = M_CHUNK
    row1 = lax.broadcasted_iota(I32, (L, 1), 0)
    lane = lax.broadcasted_iota(I32, (L, LANES), 1)
    rowl = lax.broadcasted_iota(I32, (L, LANES), 0)
    causal = lax.broadcasted_iota(I32, (L, L), 0) >= lax.broadcasted_iota(I32, (L, L), 1)
    is_f = (lane >= M_HEADS) & (lane < 2 * M_HEADS)
    cw = cw_ref[...]
    hg = hg_ref[...]
    items = [(bb, h) for bb in range(ML_NB) for h in range(M_HEADS)]

    qk, gc, bcum, gc_t, bcum_t = [], [], [], [], []
    for bb in range(ML_NB):
        x = qk_ref[bb].astype(F32)
        tail = prev_sc[bb]
        y = cb_ref[...] + cw[M_CONV - 1:M_CONV, :] * x
        for s in range(1, M_CONV):
            rolled = pltpu.roll(x, s, 0)
            first = jnp.where(row1[:8] < s, pltpu.roll(tail, s, 0), rolled[:8])
            y = y + cw[M_CONV - 1 - s:M_CONV - s, :] * jnp.concatenate([first, rolled[8:]], axis=0)
        prev_sc[bb] = x[L - 8:]
        qk.append(y * _sigmoid(y))

        g = g_ref[bb] + gb_ref[...]
        logf = jnp.minimum(g, 0.0) - jnp.log(1.0 + jnp.exp(-jnp.abs(g)))
        gcb = jnp.where(is_f, logf, jnp.where(lane < M_HEADS, g, 0.0))
        acc = jnp.where(is_f, logf, 0.0)
        k = 1
        while k < L:
            acc = acc + jnp.where(rowl >= k, pltpu.roll(acc, k, 0), 0.0)
            k *= 2
        gc.append(gcb)
        bcum.append(acc)
        gc_t.append(gcb.T)
        bcum_t.append(acc.T)

    def each(fn):
        return {it: fn(*it) for it in items}

    def head(h):
        return slice(h * M_DK, (h + 1) * M_DK)

    b_col = each(lambda bb, h: bcum[bb][:, M_HEADS + h:M_HEADS + h + 1])
    i_col = each(lambda bb, h: gc[bb][:, h:h + 1])
    m_prev = each(lambda bb, h: m_sc[bb, h:h + 1, 0:1])
    log_d = each(lambda bb, h: jnp.where(
        causal, b_col[bb, h] - bcum_t[bb][M_HEADS + h:M_HEADS + h + 1, :] + gc_t[bb][h:h + 1, :], NEG_BIG))
    inter = each(lambda bb, h: b_col[bb, h] + m_prev[bb, h])
    m_t = each(lambda bb, h: jnp.maximum(jnp.max(log_d[bb, h], axis=-1, keepdims=True), inter[bb, h]))
    d_w = each(lambda bb, h: jnp.exp(log_d[bb, h] - m_t[bb, h]))
    w_inter = each(lambda bb, h: jnp.exp(inter[bb, h] - m_t[bb, h]))

    kh = each(lambda bb, h: qk[bb][:, BRANCH_WIDTH + h * M_DK:BRANCH_WIDTH + (h + 1) * M_DK] * (M_DK ** -0.5))
    qb = each(lambda bb, h: qk[bb][:, head(h)].astype(BF16))
    ones = jnp.ones((L, M_DK), BF16)
    v_aug = each(lambda bb, h: jnp.concatenate([vo_ref[bb, :, head(h)], ones], axis=1))
    sc = each(lambda bb, h: (_dot_nt(qb[bb, h], kh[bb, h].astype(BF16)) * d_w[bb, h]).astype(BF16))

    intra = each(lambda bb, h: _dot(sc[bb, h], v_aug[bb, h]))
    carried = each(lambda bb, h: _dot(qb[bb, h], c_sc[bb, h].astype(BF16)))
    both = each(lambda bb, h: intra[bb, h] + w_inter[bb, h] * carried[bb, h])
    floor = each(lambda bb, h: jnp.exp(-m_t[bb, h]))
    hh = each(lambda bb, h: both[bb, h][:, :M_DK] / jnp.maximum(jnp.abs(both[bb, h][:, M_DK:]), floor[bb, h]))
    ms = each(lambda bb, h: jnp.mean(hh[bb, h] * hh[bb, h], axis=-1, keepdims=True))
    for bb, h in items:
        hn = hh[bb, h] * lax.rsqrt(ms[bb, h] + NORM_EPS) * hg[:, head(h)]
        o_pre = vo_ref[bb, :, BRANCH_WIDTH + h * M_DK:BRANCH_WIDTH + (h + 1) * M_DK].astype(F32)
        out_ref[bb, :, head(h)] = (hn * _sigmoid(o_pre)).astype(BF16)

    b_last = each(lambda bb, h: b_col[bb, h][L - 1:L, :])
    log_w = each(lambda bb, h: b_last[bb, h] - b_col[bb, h] + i_col[bb, h])
    m_new = each(lambda bb, h: jnp.maximum(b_last[bb, h] + m_prev[bb, h],
                                           jnp.max(log_w[bb, h], axis=0, keepdims=True)))
    decay = each(lambda bb, h: jnp.exp(b_last[bb, h] + m_prev[bb, h] - m_new[bb, h]))
    kw = each(lambda bb, h: (kh[bb, h] * jnp.exp(log_w[bb, h] - m_new[bb, h])).astype(BF16))
    upd = each(lambda bb, h: _dot_tn(kw[bb, h], v_aug[bb, h]))
    for bb, h in items:
        c_sc[bb, h] = decay[bb, h] * c_sc[bb, h] + upd[bb, h]
        m_sc[bb, h:h + 1, :] = jnp.broadcast_to(m_new[bb, h], (1, LANES))


def _mlstm(proj3, gates3, conv_w, conv_b, gate_bias, head_gain):
    b, s, _ = proj3.shape
    L = M_CHUNK
    return pl.pallas_call(
        _mlstm_kernel,
        grid=(b // ML_NB, s // L),
        in_specs=[
            pl.BlockSpec((ML_NB, L, 1024), lambda i, c: (i, c, COL_MQK // 1024)),
            pl.BlockSpec((ML_NB, L, 1024), lambda i, c: (i, c, COL_MV // 1024)),
            pl.BlockSpec((ML_NB, L, LANES), lambda i, c: (i, c, 0)),
            pl.BlockSpec((M_CONV, 1024), lambda i, c: (0, 0)),
            pl.BlockSpec((1, 1024), lambda i, c: (0, 0)),
            pl.BlockSpec((1, LANES), lambda i, c: (0, 0)),
            pl.BlockSpec((1, BRANCH_WIDTH), lambda i, c: (0, 0)),
        ],
        out_specs=pl.BlockSpec((ML_NB, L, BRANCH_WIDTH), lambda i, c: (i, c, 0)),
        out_shape=jax.ShapeDtypeStruct((b, s, BRANCH_WIDTH), BF16),
        scratch_shapes=[
            pltpu.VMEM((ML_NB, M_HEADS, M_DK, 2 * M_DK), F32),
            pltpu.VMEM((ML_NB, 8, LANES), F32),
            pltpu.VMEM((ML_NB, 8, 1024), F32),
        ],
        compiler_params=pltpu.CompilerParams(
            dimension_semantics=("parallel", "arbitrary"), vmem_limit_bytes=VMEM_LIMIT),
        name="mlstm",
    )(proj3, proj3, gates3, conv_w, conv_b, gate_bias, head_gain)


AT_T = 256
AT_TK = 512


def _halfnorm(x, gain):
    lane = lax.broadcasted_iota(I32, (1, LANES), 1)
    lo = lane < A_DQK
    x2 = x * x
    ms_lo = jnp.sum(jnp.where(lo, x2, 0.0), axis=-1, keepdims=True)
    ms_hi = jnp.sum(jnp.where(lo, 0.0, x2), axis=-1, keepdims=True)
    ms = jnp.where(lo, ms_lo, ms_hi) * (1.0 / A_DQK)
    return x * lax.rsqrt(ms + NORM_EPS) * gain


AV_ROWS = A_DV + 16


def _attn_kernel(slopes_ref, q_ref, k_ref, v_ref, qg_ref, kg_ref, lam_ref, hg_ref, out_ref,
                 k1_sc, k2_sc, vt_sc, s_sc, p_sc, acc_sc, *, lam_init):
    T = AT_T
    h = pl.program_id(1)
    qi = pl.program_id(2)
    lane = lax.broadcasted_iota(I32, (1, LANES), 1)
    slope = slopes_ref[h]

    @pl.when(qi == 0)
    def _():
        s_len = k_ref.shape[0]
        kn = _halfnorm(k_ref[...].astype(F32), kg_ref[...])
        pos = lax.broadcasted_iota(I32, (s_len, 1), 0)
        c_loc = (pos & (T - 1)).astype(F32) * slope
        c_blk = (pos >> int(math.log2(T))).astype(F32) * (slope * T)
        k1_sc[...] = jnp.where(lane < A_DQK, kn, jnp.where(lane == A_DQK, c_loc, jnp.where(
            lane == A_DQK + 1, c_blk, 0.0))).astype(BF16)
        k2_sc[...] = jnp.where(lane >= A_DQK, kn, jnp.where(lane == 0, c_loc, jnp.where(
            lane == 1, c_blk, 0.0))).astype(BF16)
        vt_sc[:A_DV, :] = v_ref[...].astype(F32).T.astype(BF16)
        vt_sc[A_DV:, :] = jnp.ones((AV_ROWS - A_DV, s_len), BF16)

    lp = lam_ref[...]
    lam = (jnp.exp(jnp.sum(lp[0:1, :] * lp[1:2, :], axis=-1, keepdims=True))
           - jnp.exp(jnp.sum(lp[2:3, :] * lp[3:4, :], axis=-1, keepdims=True)) + lam_init)

    qn = _halfnorm(q_ref[...].astype(F32), qg_ref[...]) * (A_DQK ** -0.5)
    q1 = jnp.where(lane < A_DQK, qn, jnp.where(lane < A_DQK + 2, 1.0, 0.0)).astype(BF16)
    q2 = jnp.where(lane >= A_DQK, qn, jnp.where(lane < 2, 1.0, 0.0)).astype(BF16)

    TK = AT_TK
    CH = 128

    def scores_to_scratch(j):
        start = pl.multiple_of(j * TK, TK)
        s_sc[0] = _dot_nt(k1_sc[pl.ds(start, TK), :], q1)
        s_sc[1] = _dot_nt(k2_sc[pl.ds(start, TK), :], q2)

    def softmax_to_scratch(j, m, masked):
        m_out, alpha_out = [], []
        for mp in range(2):
            for hh in range(T // LANES):
                idx = mp * (T // LANES) + hh
                lanes = slice(hh * LANES, (hh + 1) * LANES)

                def chunk(c):
                    sc = s_sc[mp, c * CH:(c + 1) * CH, lanes]
                    if masked:
                        key = j * TK + c * CH + lax.broadcasted_iota(I32, (CH, LANES), 0)
                        qry = qi * T + hh * LANES + lax.broadcasted_iota(I32, (CH, LANES), 1)
                        sc = jnp.where(key <= qry, sc, NEG_BIG)
                    return sc

                m_new = m[idx]
                for c in range(TK // CH):
                    m_new = jnp.maximum(m_new, jnp.max(chunk(c), axis=0, keepdims=True))
                for c in range(TK // CH):
                    p_sc[c * CH:(c + 1) * CH, idx * LANES:(idx + 1) * LANES] = (
                        jnp.exp(chunk(c) - m_new).astype(BF16))
                alpha_out.append(jnp.exp(m[idx] - m_new))
                m_out.append(m_new)
        return tuple(m_out), tuple(alpha_out)

    def values_from_scratch(j, alpha):
        vt = vt_sc[:, pl.ds(pl.multiple_of(j * TK, TK), TK)]
        pv = _dot(vt, p_sc[...])
        for idx in range(2 * T // LANES):
            sl = slice(idx * LANES, (idx + 1) * LANES)
            acc_sc[:, sl] = alpha[idx] * acc_sc[:, sl] + pv[:, sl]

    def body(j, carry):
        m, alpha = carry
        values_from_scratch(jnp.maximum(j - 1, 0), alpha)
        m, alpha = softmax_to_scratch(j, m, False)
        scores_to_scratch(j + 1)
        return m, alpha

    n_slab = 2 * T // LANES
    last = (qi * T) >> int(math.log2(TK))
    p_sc[...] = jnp.zeros_like(p_sc)
    acc_sc[...] = jnp.zeros_like(acc_sc)
    scores_to_scratch(0)
    m, alpha = lax.fori_loop(0, last, body, ((jnp.full((1, LANES), NEG_BIG, F32),) * n_slab,
                                             (jnp.ones((1, LANES), F32),) * n_slab))
    values_from_scratch(jnp.maximum(last - 1, 0), alpha)
    _, alpha = softmax_to_scratch(last, m, True)
    values_from_scratch(last, alpha)

    a1 = acc_sc[:, :T]
    a2 = acc_sc[:, T:]
    o_t = a1[:A_DV, :] / a1[A_DV:A_DV + 1, :] - lam * (a2[:A_DV, :] / a2[A_DV:A_DV + 1, :])
    out_ref[...] = (_rms(o_t.T, hg_ref[...]) * (1.0 - lam_init)).astype(BF16)


def _attention(proj3, q_gain, k_gain, lam_params, head_gain, lam_init):
    b, s, _ = proj3.shape
    T = AT_T
    slopes = jnp.asarray(ALIBI_SLOPES, F32)
    return pl.pallas_call(
        functools.partial(_attn_kernel, lam_init=lam_init),
        grid=(b, A_HEADS, s // T),
        in_specs=[
            pl.BlockSpec(memory_space=pltpu.SMEM),
            pl.BlockSpec((None, T, LANES), lambda i, h, q: (i, q, COL_AQ // LANES + h)),
            pl.BlockSpec((None, s, LANES), lambda i, h, q: (i, 0, COL_AK // LANES + h)),
            pl.BlockSpec((None, s, LANES), lambda i, h, q: (i, 0, COL_AV // LANES + h)),
            pl.BlockSpec((1, LANES), lambda i, h, q: (0, 0)),
            pl.BlockSpec((1, LANES), lambda i, h, q: (0, 0)),
            pl.BlockSpec((4, A_DQK), lambda i, h, q: (0, 0)),
            pl.BlockSpec((1, LANES), lambda i, h, q: (0, h)),
        ],
        out_specs=pl.BlockSpec((None, T, LANES), lambda i, h, q: (i, q, h)),
        out_shape=jax.ShapeDtypeStruct((b, s, BRANCH_WIDTH), BF16),
        scratch_shapes=[pltpu.VMEM((s, LANES), BF16), pltpu.VMEM((s, LANES), BF16),
                        pltpu.VMEM((AV_ROWS, s), BF16), pltpu.VMEM((2, AT_TK, T), F32),
                        pltpu.VMEM((AT_TK, 2 * T), BF16), pltpu.VMEM((AV_ROWS, 2 * T), F32)],
        compiler_params=pltpu.CompilerParams(
            dimension_semantics=("parallel", "parallel", "arbitrary"), vmem_limit_bytes=VMEM_LIMIT),
        name="diff_attn",
    )(slopes, proj3, proj3, proj3, q_gain, k_gain, lam_params, head_gain)


MG_TM = 512
MG_HALO = 128
ROUTER_ROWS = 16
ROUTE_ROWS = 8


def _merge_kernel(x_ref, hm_ref, ha_ref, u_ref, halo_ref, gm_ref, ga_ref, gp_ref, pw_ref, ps_ref,
                  wb_ref, wo_ref, fg_ref, *rest, seq_tiles, routed):
    if routed:
        wr_ref, br_ref, xo_ref, hf_ref, route_ref = rest
    else:
        xo_ref, hf_ref = rest
    TM = MG_TM
    i = pl.program_id(0)
    seq_tile = i % seq_tiles

    u = u_ref[...]
    halo = jnp.where(seq_tile > 0, halo_ref[...], jnp.zeros_like(halo_ref))
    d_main = (lax.broadcasted_iota(I32, (TM, TM), 0) - lax.broadcasted_iota(I32, (TM, TM), 1))
    d_halo = (lax.broadcasted_iota(I32, (TM, MG_HALO), 0) + MG_HALO
              - lax.broadcasted_iota(I32, (TM, MG_HALO), 1))
    t_pos = seq_tile * TM + lax.broadcasted_iota(I32, (TM, 1), 0)
    ps = ps_ref[...]
    hp_parts = []
    for g, w in enumerate(P_WINDOWS):
        sl = slice(g * P_GC, (g + 1) * P_GC)
        band = jnp.where((d_main >= 0) & (d_main < w), 1.0, 0.0).astype(BF16)
        band_h = jnp.where(d_halo < w, 1.0, 0.0).astype(BF16)
        ug = u[:, sl]
        sums = _dot(band, ug) + _dot(band_h, halo[:, sl])
        cnt = jnp.minimum(t_pos + 1, w).astype(F32)
        pooled = sums / cnt - ug.astype(F32)
        hp_parts.append((_dot(pooled.astype(BF16), pw_ref[g]) * ps[:, sl]).astype(BF16))
    hp = jnp.concatenate(hp_parts, axis=1)

    merged = (_sigmoid(gm_ref[...].astype(F32)) * _dot(hm_ref[...], wb_ref[0])
              + _sigmoid(ga_ref[...].astype(F32)) * _dot(ha_ref[...], wb_ref[1])
              + _sigmoid(gp_ref[...].astype(F32)) * _dot(hp, wb_ref[2]))
    x_new = x_ref[...] + _dot(merged.astype(BF16), wo_ref[...])
    xo_ref[...] = x_new
    hf = _rms(x_new, fg_ref[...])
    hf_ref[...] = hf.astype(BF16)

    if routed:
        hf_hi = hf.astype(BF16)
        hf_lo = (hf - hf_hi.astype(F32)).astype(BF16)
        wr = wr_ref[...]
        wr_hi = wr.astype(BF16)
        wr_lo = (wr - wr_hi.astype(F32)).astype(BF16)
        logits = (_dot_nt(wr_hi, hf_hi) + _dot_nt(wr_lo, hf_hi) + _dot_nt(wr_hi, hf_lo))[:N_EXPERTS, :]
        logits = logits + br_ref[...]
        expert = lax.broadcasted_iota(I32, (N_EXPERTS, TM), 0).astype(F32)
        m1 = jnp.max(logits, axis=0, keepdims=True)
        i1 = jnp.min(jnp.where(logits == m1, expert, float(N_EXPERTS)), axis=0, keepdims=True)
        rest_l = jnp.where(expert == i1, NEG_BIG, logits)
        m2 = jnp.max(rest_l, axis=0, keepdims=True)
        i2 = jnp.min(jnp.where(rest_l == m2, expert, float(N_EXPERTS)), axis=0, keepdims=True)
        g1 = 1.0 / (1.0 + jnp.exp(m2 - m1))
        route_ref[...] = jnp.concatenate(
            [g1, 1.0 - g1, i1, i2, jnp.zeros((ROUTE_ROWS - 4, TM), F32)], axis=0)


def _merge(x2, hm2, ha2, proj2, pool_w, pool_scale, w_branch, w_out, ffn_gain, seq, router=None):
    tok = x2.shape[0]
    TM = MG_TM
    routed = router is not None
    hb = TM // MG_HALO
    in_specs = [
        pl.BlockSpec((TM, D_MODEL), lambda i: (i, 0)),
        pl.BlockSpec((TM, BRANCH_WIDTH), lambda i: (i, 0)),
        pl.BlockSpec((TM, BRANCH_WIDTH), lambda i: (i, 0)),
        pl.BlockSpec((TM, BRANCH_WIDTH), lambda i: (i, COL_PU // BRANCH_WIDTH)),
        pl.BlockSpec((MG_HALO, BRANCH_WIDTH), lambda i: (jnp.maximum(i * hb - 1, 0), COL_PU // BRANCH_WIDTH)),
        pl.BlockSpec((TM, D_MODEL), lambda i: (i, COL_G // D_MODEL)),
        pl.BlockSpec((TM, D_MODEL), lambda i: (i, COL_G // D_MODEL + 1)),
        pl.BlockSpec((TM, D_MODEL), lambda i: (i, COL_G // D_MODEL + 2)),
        pl.BlockSpec((len(P_WINDOWS), P_GC, P_GC), lambda i: (0, 0, 0)),
        pl.BlockSpec((1, BRANCH_WIDTH), lambda i: (0, 0)),
        pl.BlockSpec((3, BRANCH_WIDTH, D_MODEL), lambda i: (0, 0, 0)),
        pl.BlockSpec((D_MODEL, D_MODEL), lambda i: (0, 0)),
        pl.BlockSpec((1, D_MODEL), lambda i: (0, 0)),
    ]
    args = [x2, hm2, ha2, proj2, proj2, proj2, proj2, proj2, pool_w, pool_scale, w_branch, w_out, ffn_gain]
    out_specs = [pl.BlockSpec((TM, D_MODEL), lambda i: (i, 0)),
                 pl.BlockSpec((TM, D_MODEL), lambda i: (i, 0))]
    out_shape = [jax.ShapeDtypeStruct((tok, D_MODEL), F32), jax.ShapeDtypeStruct((tok, D_MODEL), BF16)]
    if routed:
        in_specs += [pl.BlockSpec((ROUTER_ROWS, D_MODEL), lambda i: (0, 0)),
                     pl.BlockSpec((N_EXPERTS, 1), lambda i: (0, 0))]
        args += list(router)
        out_specs.append(pl.BlockSpec((ROUTE_ROWS, TM), lambda i: (0, i)))
        out_shape.append(jax.ShapeDtypeStruct((ROUTE_ROWS, tok), F32))
    return pl.pallas_call(
        functools.partial(_merge_kernel, seq_tiles=seq // TM, routed=routed),
        grid=(tok // TM,),
        in_specs=in_specs,
        out_specs=out_specs,
        out_shape=out_shape,
        compiler_params=pltpu.CompilerParams(
            dimension_semantics=("parallel",), vmem_limit_bytes=VMEM_LIMIT),
        name="merge_routed" if routed else "merge",
    )(*args)


def _ple_epilogue(x_new, p, pg_gain, wpg, wpp):
    gate = _sigmoid(_dot(_rms(x_new, pg_gain).astype(BF16), wpg))
    return x_new + gate * _dot(p.astype(BF16), wpp)


FF_TM = 512
FF_TF = 1408


def _ffn_kernel(hf_ref, x_ref, p_ref, wg_ref, wu_ref, wd_ref, pgn_ref, wpg_ref, wpp_ref, out_ref, acc_sc):
    j = pl.program_id(1)

    @pl.when(j == 0)
    def _():
        acc_sc[...] = jnp.zeros_like(acc_sc)

    hf = hf_ref[...]
    g = _dot(hf, wg_ref[...])
    u = _dot(hf, wu_ref[...])
    acc_sc[...] += _dot((g * _sigmoid(g) * u).astype(BF16), wd_ref[...])

    @pl.when(j == pl.num_programs(1) - 1)
    def _():
        out_ref[...] = _ple_epilogue(x_ref[...] + acc_sc[...], p_ref[...], pgn_ref[...],
                                     wpg_ref[...], wpp_ref[...])


def _ffn(hf2, x2, p2, w_gu, w_down, ple_gain, wpg, wpp):
    tok = x2.shape[0]
    TM, TF = FF_TM, FF_TF
    nf = D_FF // TF
    return pl.pallas_call(
        _ffn_kernel,
        grid=(tok // TM, nf),
        in_specs=[
            pl.BlockSpec((TM, D_MODEL), lambda i, j: (i, 0)),
            pl.BlockSpec((TM, D_MODEL), lambda i, j: (i, 0)),
            pl.BlockSpec((TM, PLE_DIM), lambda i, j: (i, 0)),
            pl.BlockSpec((D_MODEL, TF), lambda i, j: (0, j)),
            pl.BlockSpec((D_MODEL, TF), lambda i, j: (0, j + nf)),
            pl.BlockSpec((TF, D_MODEL), lambda i, j: (j, 0)),
            pl.BlockSpec((1, D_MODEL), lambda i, j: (0, 0)),
            pl.BlockSpec((D_MODEL, D_MODEL), lambda i, j: (0, 0)),
            pl.BlockSpec((PLE_DIM, D_MODEL), lambda i, j: (0, 0)),
        ],
        out_specs=pl.BlockSpec((TM, D_MODEL), lambda i, j: (i, 0)),
        out_shape=jax.ShapeDtypeStruct((tok, D_MODEL), F32),
        scratch_shapes=[pltpu.VMEM((TM, D_MODEL), F32)],
        compiler_params=pltpu.CompilerParams(
            dimension_semantics=("parallel", "arbitrary"), vmem_limit_bytes=VMEM_LIMIT),
        name="ffn_dense",
    )(hf2, x2, p2, w_gu, w_gu, w_down, ple_gain, wpg, wpp)


MOE_TM = 1024
MOE_TF = 512
DP_TM = 256
DP_CH = 256
CB_TB = 256
CB_ALIGN = 16
CB_W = CB_TB + CB_ALIGN


def _dispatch_kernel(clo_ref, chi_ref, pos_ref, gate_ref, hf_ref, x_ref, g_ref, acc_sc, gacc_sc):
    i = pl.program_id(0)
    rows = i * DP_TM + lax.broadcasted_iota(I32, (DP_TM, 1), 0)
    acc_sc[...] = jnp.zeros_like(acc_sc)
    gacc_sc[...] = jnp.zeros_like(gacc_sc)

    lo = clo_ref[i]
    hi = chi_ref[i]

    def chunk(c, live):
        start = pl.multiple_of(c * DP_CH, DP_CH)
        pos = pos_ref[:, pl.ds(start, DP_CH)]
        gate = gate_ref[:, pl.ds(start, DP_CH)]
        want = jnp.where(live, rows, -1)
        hit1 = pos[0:1, :] == want
        hit2 = pos[1:2, :] == want
        onehot = jnp.where(hit1, 1.0, jnp.where(hit2, 1.0, 0.0)).astype(BF16)
        gates = jnp.sum(jnp.where(hit1, gate[0:1, :], jnp.where(hit2, gate[1:2, :], 0.0)),
                        axis=-1, keepdims=True)
        return _dot(onehot, hf_ref[pl.ds(start, DP_CH), :]), gates

    def pair(t, carry):
        c = lo + 2 * t
        x0, g0 = chunk(c, True)
        x1, g1 = chunk(jnp.minimum(c + 1, hi), c + 1 <= hi)
        acc_sc[...] += x0 + x1
        gacc_sc[...] += g0 + g1
        return carry

    lax.fori_loop(0, (hi - lo + 2) >> 1, pair, 0)
    x_ref[...] = acc_sc[...].astype(BF16)
    g_ref[...] = gacc_sc[...]


def _dispatch(hf2, pos_t, gate_t, chunk_lo, chunk_hi):
    tok = hf2.shape[0]
    ns = chunk_lo.shape[0]
    grid_spec = pltpu.PrefetchScalarGridSpec(
        num_scalar_prefetch=2,
        grid=(ns,),
        in_specs=[
            pl.BlockSpec((2, tok), lambda i, lo, hi: (0, 0)),
            pl.BlockSpec((2, tok), lambda i, lo, hi: (0, 0)),
            pl.BlockSpec((tok, D_MODEL), lambda i, lo, hi: (0, 0), pipeline_mode=pl.Buffered(1)),
        ],
        out_specs=[pl.BlockSpec((DP_TM, D_MODEL), lambda i, lo, hi: (i, 0)),
                   pl.BlockSpec((DP_TM, 1), lambda i, lo, hi: (i, 0))],
        scratch_shapes=[pltpu.VMEM((DP_TM, D_MODEL), F32), pltpu.VMEM((DP_TM, 1), F32)],
    )
    return pl.pallas_call(
        _dispatch_kernel,
        grid_spec=grid_spec,
        out_shape=[jax.ShapeDtypeStruct((ns * DP_TM, D_MODEL), BF16),
                   jax.ShapeDtypeStruct((ns * DP_TM, 1), F32)],
        compiler_params=pltpu.CompilerParams(
            dimension_semantics=("arbitrary",), vmem_limit_bytes=VMEM_LIMIT),
        name="moe_dispatch",
    )(chunk_lo, chunk_hi, pos_t, gate_t, hf2)


def _moe_kernel(te_ref, tv_ref, x_ref, gate_ref, wg_ref, wu_ref, wd_ref, y_ref, acc_sc):
    i = pl.program_id(0)
    j = pl.program_id(1)
    last = pl.num_programs(1) - 1
    valid = tv_ref[i] > 0

    @pl.when(valid & (j == 0))
    def _():
        acc_sc[...] = jnp.zeros_like(acc_sc)

    @pl.when(valid)
    def _():
        xb = x_ref[...]
        g = _dot(xb, wg_ref[...].astype(BF16))
        u = _dot(xb, wu_ref[...].astype(BF16))
        acc_sc[...] += _dot((g * _sigmoid(g) * u).astype(BF16), wd_ref[...].astype(BF16))

        @pl.when(j == last)
        def _():
            y_ref[...] = (acc_sc[...] * gate_ref[...]).astype(BF16)

    @pl.when(jnp.logical_not(valid) & (j == last))
    def _():
        y_ref[...] = jnp.zeros_like(y_ref)


def _moe(x_sorted, gate_sorted, w_gu, w_down, tile_expert, tile_valid):
    TM, TF = MOE_TM, MOE_TF
    nt = tile_expert.shape[0]
    nf = D_FF_EXPERT // TF

    def col(j, tv, i):
        return jnp.where(tv[i] > 0, j, nf - 1)

    grid_spec = pltpu.PrefetchScalarGridSpec(
        num_scalar_prefetch=2,
        grid=(nt, nf),
        in_specs=[
            pl.BlockSpec((TM, D_MODEL), lambda i, j, te, tv: (i, 0)),
            pl.BlockSpec((TM, 1), lambda i, j, te, tv: (i, 0)),
            pl.BlockSpec((None, D_MODEL, TF), lambda i, j, te, tv: (te[i], 0, col(j, tv, i))),
            pl.BlockSpec((None, D_MODEL, TF), lambda i, j, te, tv: (te[i], 0, col(j, tv, i) + nf)),
            pl.BlockSpec((None, TF, D_MODEL), lambda i, j, te, tv: (te[i], col(j, tv, i), 0)),
        ],
        out_specs=pl.BlockSpec((TM, D_MODEL), lambda i, j, te, tv: (i, 0)),
        scratch_shapes=[pltpu.VMEM((TM, D_MODEL), F32)],
    )
    return pl.pallas_call(
        _moe_kernel,
        grid_spec=grid_spec,
        out_shape=jax.ShapeDtypeStruct((nt * TM, D_MODEL), BF16),
        compiler_params=pltpu.CompilerParams(
            dimension_semantics=("arbitrary", "arbitrary"), vmem_limit_bytes=VMEM_LIMIT),
        name="moe_experts",
    )(tile_expert, tile_valid, x_sorted, gate_sorted, w_gu, w_gu, w_down)


def _combine_kernel(w0_ref, rng_ref, x_ref, p_ref, pos_ref, y_hbm, pgn_ref, wpg_ref, wpp_ref, out_ref,
                    win_sc, acc_sc, sem):
    i = pl.program_id(0)
    slot = i & 1

    def window_copy(step, buf, e):
        start = pl.multiple_of(w0_ref[step * N_EXPERTS + e], CB_ALIGN)
        return pltpu.make_async_copy(y_hbm.at[pl.ds(start, CB_W), :], win_sc.at[buf, e], sem.at[buf, e])

    @pl.when(i == 0)
    def _():
        for e in range(N_EXPERTS):
            window_copy(0, 0, e).start()

    @pl.when(i + 1 < pl.num_programs(0))
    def _():
        for e in range(N_EXPERTS):
            window_copy(i + 1, 1 - slot, e).start()

    pos = pos_ref[...]
    pos1 = pos[:, 0:1]
    pos2 = pos[:, 1:2]

    n_win = pl.num_programs(0) * N_EXPERTS

    def row_begin(e):
        return rng_ref[i * N_EXPERTS + e]

    def row_end(e):
        return rng_ref[n_win + i * N_EXPERTS + e]

    def onehot(e, first, width):
        rows = lax.broadcasted_iota(I32, (1, width), 1) + (first + w0_ref[i * N_EXPERTS + e])
        rows = jnp.where((rows >= row_begin(e)) & (rows < row_end(e)), rows, -1)
        return jnp.where(pos1 == rows, 1.0, jnp.where(pos2 == rows, 1.0, 0.0)).astype(BF16)

    moe_out = jnp.zeros((CB_TB, D_MODEL), F32)
    for e in range(N_EXPERTS):
        window_copy(i, slot, e).wait()
        moe_out = moe_out + _dot(onehot(e, 0, CB_TB), win_sc[slot, e, :CB_TB, :])
    acc_sc[...] = moe_out

    for e in range(N_EXPERTS):
        @pl.when(row_end(e) - w0_ref[i * N_EXPERTS + e] > CB_TB)
        def _():
            acc_sc[...] += _dot(onehot(e, CB_TB, CB_ALIGN), win_sc[slot, e, CB_TB:, :])

    out_ref[...] = _ple_epilogue(x_ref[...] + acc_sc[...], p_ref[...], pgn_ref[...], wpg_ref[...],
                                 wpp_ref[...])


def _combine(x2, p2, pos, y_sorted, win_start, win_rows, ple_gain, wpg, wpp):
    tok = x2.shape[0]
    TB = CB_TB
    grid_spec = pltpu.PrefetchScalarGridSpec(
        num_scalar_prefetch=2,
        grid=(tok // TB,),
        in_specs=[
            pl.BlockSpec((TB, D_MODEL), lambda i, w0, wt: (i, 0)),
            pl.BlockSpec((TB, PLE_DIM), lambda i, w0, wt: (i, 0)),
            pl.BlockSpec((TB, 2), lambda i, w0, wt: (i, 0)),
            pl.BlockSpec(memory_space=pl.ANY),
            pl.BlockSpec((1, D_MODEL), lambda i, w0, wt: (0, 0)),
            pl.BlockSpec((D_MODEL, D_MODEL), lambda i, w0, wt: (0, 0)),
            pl.BlockSpec((PLE_DIM, D_MODEL), lambda i, w0, wt: (0, 0)),
        ],
        out_specs=pl.BlockSpec((TB, D_MODEL), lambda i, w0, wt: (i, 0)),
        scratch_shapes=[pltpu.VMEM((2, N_EXPERTS, CB_W, D_MODEL), BF16), pltpu.VMEM((TB, D_MODEL), F32),
                        pltpu.SemaphoreType.DMA((2, N_EXPERTS))],
    )
    return pl.pallas_call(
        _combine_kernel,
        grid_spec=grid_spec,
        out_shape=jax.ShapeDtypeStruct((tok, D_MODEL), F32),
        compiler_params=pltpu.CompilerParams(
            dimension_semantics=("arbitrary",), vmem_limit_bytes=VMEM_LIMIT),
        name="moe_combine",
    )(win_start, win_rows, x2, p2, pos, y_sorted, ple_gain, wpg, wpp)


def _route_metadata(route, tok):
    TM = MOE_TM
    nt = (2 * tok) // TM + N_EXPERTS
    rows = nt * TM
    g1, g2 = route[0], route[1]
    i1, i2 = route[2].astype(I32), route[3].astype(I32)
    experts = jnp.arange(N_EXPERTS, dtype=I32)
    member = ((i1[:, None] == experts) | (i2[:, None] == experts)).astype(I32)
    rank = jnp.cumsum(member, axis=0) - member
    counts = jnp.sum(member, axis=0)
    tiles_e = (counts + TM - 1) // TM
    tile_end = jnp.cumsum(tiles_e)
    tile_start = tile_end - tiles_e
    seg_start = tile_start * TM
    sel1 = i1[:, None] == experts
    sel2 = i2[:, None] == experts
    pos1 = jnp.sum(jnp.where(sel1, seg_start[None, :] + rank, 0), axis=1).astype(I32)
    pos2 = jnp.sum(jnp.where(sel2, seg_start[None, :] + rank, 0), axis=1).astype(I32)

    tiles = jnp.arange(nt, dtype=I32)
    n_used = tile_end[-1]
    tile_valid = (tiles < n_used).astype(I32)
    te = jnp.sum((tiles[:, None] >= tile_end[None, :]).astype(I32), axis=1)
    te_last = jnp.sum(((n_used - 1) >= tile_end).astype(I32))
    tile_expert = jnp.where(tile_valid > 0, jnp.minimum(te, N_EXPERTS - 1), te_last).astype(I32)

    steps = jnp.arange(rows // DP_TM, dtype=I32)
    step_tile = steps // (TM // DP_TM)
    step_valid = step_tile < n_used
    step_expert = jnp.minimum(jnp.sum((step_tile[:, None] >= tile_end[None, :]).astype(I32), axis=1),
                              N_EXPERTS - 1)
    pick = step_expert[:, None] == experts[None, :]
    rank_lo = steps * DP_TM - jnp.sum(jnp.where(pick, seg_start[None, :], 0), axis=1)
    rank_hi = jnp.minimum(rank_lo + DP_TM, jnp.sum(jnp.where(pick, counts[None, :], 0), axis=1)) - 1
    before = jnp.sum(jnp.where(pick[:, :, None], rank[::DP_CH, :].T[None, :, :], 0), axis=1)
    chunk_lo = jnp.sum((before <= rank_lo[:, None]).astype(I32), axis=1) - 1
    chunk_hi = jnp.sum((before <= rank_hi[:, None]).astype(I32), axis=1) - 1
    live = step_valid & (rank_hi >= rank_lo)
    chunk_lo = jnp.where(live, chunk_lo, 0).astype(I32)
    chunk_hi = jnp.where(live, chunk_hi, -1).astype(I32)

    blk_start = seg_start[None, :] + rank[::CB_TB, :]
    win_start = jnp.minimum((blk_start // CB_ALIGN) * CB_ALIGN, rows - CB_W).astype(I32)
    blk_count = jnp.concatenate([rank[CB_TB::CB_TB, :], counts[None, :]], axis=0) - rank[::CB_TB, :]
    win_rows = jnp.concatenate([blk_start.reshape(-1), (blk_start + blk_count).reshape(-1)]).astype(I32)
    win_start = win_start.reshape(-1)
    pos = jnp.stack([pos1, pos2], axis=1)
    return (tile_expert, tile_valid, chunk_lo, chunk_hi, jnp.stack([pos1, pos2]), jnp.stack([g1, g2]),
            pos, win_start, win_rows)


def _pack_w_in(w):
    o = [0, 1024, 1536, 2048, 2052, 2056, 2568, 3080, 3592, 4104, 7176]
    main = jnp.concatenate([w[:, o[0]:o[3]], w[:, o[5]:o[10]]], axis=1).astype(BF16)
    gates = jnp.pad(w[:, o[3]:o[5]], ((0, 0), (0, LANES - 2 * M_HEADS))).astype(BF16)
    return main, gates


def kernel(x, p, attn_norm, w_in, m_conv_w, m_conv_b, m_gate_bias, m_head_norm, a_q_norm, a_k_norm,
           a_lambda, a_head_norm, pool_w, pool_scale, w_branch, w_out, ffn_norm, dense_w_gu, dense_w_down,
           router_w, router_b, moe_w_gu, moe_w_down, ple_norm, ple_w_gate, ple_w_proj):
    b, s, d = x.shape
    depth = w_in.shape[0]
    tok = b * s
    x2 = x.reshape(tok, d)
    for layer in range(depth):
        w_main, w_if = _pack_w_in(w_in[layer])
        proj, gates = _inproj(x2, attn_norm[layer].reshape(1, d), w_main, w_if)
        proj3 = proj.reshape(b, s, PROJ_WIDTH)
        gate_bias = jnp.pad(m_gate_bias[layer], (0, LANES - 2 * M_HEADS)).reshape(1, LANES)
        h_m = _mlstm(proj3, gates.reshape(b, s, LANES), m_conv_w[layer], m_conv_b[layer].reshape(1, -1),
                     gate_bias, m_head_norm[layer].reshape(1, -1))
        lam_init = 0.8 - 0.6 * math.exp(-0.3 * layer)
        h_a = _attention(proj3, jnp.tile(a_q_norm[layer], 2).reshape(1, LANES),
                         jnp.tile(a_k_norm[layer], 2).reshape(1, LANES), a_lambda[layer],
                         a_head_norm[layer].reshape(1, -1), lam_init)
        p2 = p[layer].reshape(tok, PLE_DIM)
        ple_args = (ple_norm[layer].reshape(1, d), ple_w_gate[layer].astype(BF16), ple_w_proj[layer].astype(BF16))
        merge_args = (x2, h_m.reshape(tok, BRANCH_WIDTH), h_a.reshape(tok, BRANCH_WIDTH), proj,
                      pool_w[layer].astype(BF16), pool_scale[layer].reshape(1, -1),
                      w_branch[layer].astype(BF16), w_out[layer].astype(BF16), ffn_norm[layer].reshape(1, d), s)
        j = layer // 2
        if layer % 2 == 0:
            x_mid, hf = _merge(*merge_args)
            x2 = _ffn(hf, x_mid, p2, dense_w_gu[j].astype(BF16), dense_w_down[j].astype(BF16), *ple_args)
        else:
            wr = jnp.pad(router_w[j].T, ((0, ROUTER_ROWS - N_EXPERTS), (0, 0)))
            br = router_b[j].reshape(N_EXPERTS, 1)
            x_mid, hf, route = _merge(*merge_args, router=(wr, br))
            (tile_expert, tile_valid, chunk_lo, chunk_hi, pos_t, gate_t, pos,
             win_start, win_rows) = _route_metadata(route, tok)
            x_sorted, gate_sorted = _dispatch(hf, pos_t, gate_t, chunk_lo, chunk_hi)
            y_sorted = _moe(x_sorted, gate_sorted, moe_w_gu[j], moe_w_down[j], tile_expert, tile_valid)
            x2 = _combine(x_mid, p2, pos, y_sorted, win_start, win_rows, *ple_args)
    return x2.reshape(b, s, d)
```

```python
import functools
import math

import jax
import jax.numpy as jnp
from jax import lax
from jax.experimental import pallas as pl
from jax.experimental.pallas import tpu as pltpu

F32 = jnp.float32
BF16 = jnp.bfloat16
I32 = jnp.int32

D_MODEL = 1024
PLE_DIM = 256
NORM_EPS = 1e-6
BRANCH_WIDTH = 512

M_HEADS = 4
M_DK = 128
M_CONV = 4
M_CHUNK = 128

A_HEADS = 4
A_DV = 128
A_DQK = 64
ALIBI_SLOPES = tuple(2.0 ** (-8.0 * (h + 1) / A_HEADS) for h in range(A_HEADS))

P_WINDOWS = (2, 4, 8, 16)
P_GC = 128

D_FF = 2816
N_EXPERTS = 8
D_FF_EXPERT = 3584

PROJ_WIDTH = 7168
COL_MQK, COL_MV, COL_MO, COL_AQ, COL_AK, COL_AV, COL_PU, COL_G = 0, 1024, 1536, 2048, 2560, 3072, 3584, 4096

LANES = 128
NEG_BIG = -1e30

VMEM_LIMIT = 56 * 1024 * 1024


def _sigmoid(x):
    return 0.5 * jnp.tanh(0.5 * x) + 0.5


def _rms(x, gain):
    return x * lax.rsqrt(jnp.mean(x * x, axis=-1, keepdims=True) + NORM_EPS) * gain


def _dot(a, b):
    return jnp.dot(a, b, preferred_element_type=F32)


def _dot_nt(a, b):
    return lax.dot_general(a, b, (((1,), (1,)), ((), ())), preferred_element_type=F32)


def _dot_tn(a, b):
    return lax.dot_general(a, b, (((0,), (0,)), ((), ())), preferred_element_type=F32)


IN_TM = 512
IN_TN = 1024


def _inproj_kernel(x_ref, gain_ref, w_ref, wif_ref, out_ref, gates_ref):
    hn = _rms(x_ref[...], gain_ref[...]).astype(BF16)
    gates_ref[...] = _dot(hn, wif_ref[...])
    for n in range(PROJ_WIDTH // IN_TN):
        cols = slice(n * IN_TN, (n + 1) * IN_TN)
        out_ref[:, cols] = _dot(hn, w_ref[:, cols]).astype(BF16)


def _inproj(x2, gain, w_main, w_if):
    tok = x2.shape[0]
    return pl.pallas_call(
        _inproj_kernel,
        grid=(tok // IN_TM,),
        in_specs=[
            pl.BlockSpec((IN_TM, D_MODEL), lambda i: (i, 0)),
            pl.BlockSpec((1, D_MODEL), lambda i: (0, 0)),
            pl.BlockSpec((D_MODEL, PROJ_WIDTH), lambda i: (0, 0), pipeline_mode=pl.Buffered(1)),
            pl.BlockSpec((D_MODEL, LANES), lambda i: (0, 0)),
        ],
        out_specs=[
            pl.BlockSpec((IN_TM, PROJ_WIDTH), lambda i: (i, 0)),
            pl.BlockSpec((IN_TM, LANES), lambda i: (i, 0)),
        ],
        out_shape=[
            jax.ShapeDtypeStruct((tok, PROJ_WIDTH), BF16),
            jax.ShapeDtypeStruct((tok, LANES), F32),
        ],
        compiler_params=pltpu.CompilerParams(
            dimension_semantics=("parallel",), vmem_limit_bytes=VMEM_LIMIT),
        name="inproj",
    )(x2, gain, w_main, w_if)


ML_NB = 2


def _mlstm_kernel(qk_ref, vo_ref, g_ref, cw_ref, cb_ref, gb_ref, hg_ref, out_ref,
                  c_sc, m_sc, prev_sc):
    @pl.when(pl.program_id(1) == 0)
    def _():
        c_sc[...] = jnp.zeros_like(c_sc)
        m_sc[...] = jnp.zeros_like(m_sc)
        prev_sc[...] = jnp.zeros_like(prev_sc)

    L = M_CHUNK
    row1 = lax.broadcasted_iota(I32, (L, 1), 0)
    lane = lax.broadcasted_iota(I32, (L, LANES), 1)
    rowl = lax.broadcasted_iota(I32, (L, LANES), 0)
    causal = lax.broadcasted_iota(I32, (L, L), 0) >= lax.broadcasted_iota(I32, (L, L), 1)
    is_f = (lane >= M_HEADS) & (lane < 2 * M_HEADS)
    cw = cw_ref[...]
    hg = hg_ref[...]
    items = [(bb, h) for bb in range(ML_NB) for h in range(M_HEADS)]

    qk, gc, bcum, gc_t, bcum_t = [], [], [], [], []
    for bb in range(ML_NB):
        x = qk_ref[bb].astype(F32)
        tail = prev_sc[bb]
        y = cb_ref[...] + cw[M_CONV - 1:M_CONV, :] * x
        for s in range(1, M_CONV):
            rolled = pltpu.roll(x, s, 0)
            first = jnp.where(row1[:8] < s, pltpu.roll(tail, s, 0), rolled[:8])
            y = y + cw[M_CONV - 1 - s:M_CONV - s, :] * jnp.concatenate([first, rolled[8:]], axis=0)
        prev_sc[bb] = x[L - 8:]
        qk.append(y * _sigmoid(y))

        g = g_ref[bb] + gb_ref[...]
        logf = jnp.minimum(g, 0.0) - jnp.log(1.0 + jnp.exp(-jnp.abs(g)))
        gcb = jnp.where(is_f, logf, jnp.where(lane < M_HEADS, g, 0.0))
        acc = jnp.where(is_f, logf, 0.0)
        k = 1
        while k < L:
            acc = acc + jnp.where(rowl >= k, pltpu.roll(acc, k, 0), 0.0)
            k *= 2
        gc.append(gcb)
        bcum.append(acc)
        gc_t.append(gcb.T)
        bcum_t.append(acc.T)

    def each(fn):
        return {it: fn(*it) for it in items}

    def head(h):
        return slice(h * M_DK, (h + 1) * M_DK)

    b_col = each(lambda bb, h: bcum[bb][:, M_HEADS + h:M_HEADS + h + 1])
    i_col = each(lambda bb, h: gc[bb][:, h:h + 1])
    m_prev = each(lambda bb, h: m_sc[bb, h:h + 1, 0:1])
    log_d = each(lambda bb, h: jnp.where(
        causal, b_col[bb, h] - bcum_t[bb][M_HEADS + h:M_HEADS + h + 1, :] + gc_t[bb][h:h + 1, :], NEG_BIG))
    inter = each(lambda bb, h: b_col[bb, h] + m_prev[bb, h])
    m_t = each(lambda bb, h: jnp.maximum(jnp.max(log_d[bb, h], axis=-1, keepdims=True), inter[bb, h]))
    d_w = each(lambda bb, h: jnp.exp(log_d[bb, h] - m_t[bb, h]))
    w_inter = each(lambda bb, h: jnp.exp(inter[bb, h] - m_t[bb, h]))

    kh = each(lambda bb, h: qk[bb][:, BRANCH_WIDTH + h * M_DK:BRANCH_WIDTH + (h + 1) * M_DK] * (M_DK ** -0.5))
    qb = each(lambda bb, h: qk[bb][:, head(h)].astype(BF16))
    ones = jnp.ones((L, M_DK), BF16)
    v_aug = each(lambda bb, h: jnp.concatenate([vo_ref[bb, :, head(h)], ones], axis=1))
    sc = each(lambda bb, h: (_dot_nt(qb[bb, h], kh[bb, h].astype(BF16)) * d_w[bb, h]).astype(BF16))

    intra = each(lambda bb, h: _dot(sc[bb, h], v_aug[bb, h]))
    carried = each(lambda bb, h: _dot(qb[bb, h], c_sc[bb, h].astype(BF16)))
    both = each(lambda bb, h: intra[bb, h] + w_inter[bb, h] * carried[bb, h])
    floor = each(lambda bb, h: jnp.exp(-m_t[bb, h]))
    hh = each(lambda bb, h: both[bb, h][:, :M_DK] / jnp.maximum(jnp.abs(both[bb, h][:, M_DK:]), floor[bb, h]))
    ms = each(lambda bb, h: jnp.mean(hh[bb, h] * hh[bb, h], axis=-1, keepdims=True))
    for bb, h in items:
        hn = hh[bb, h] * lax.rsqrt(ms[bb, h] + NORM_EPS) * hg[:, head(h)]
        o_pre = vo_ref[bb, :, BRANCH_WIDTH + h * M_DK:BRANCH_WIDTH + (h + 1) * M_DK].astype(F32)
        out_ref[bb, :, head(h)] = (hn * _sigmoid(o_pre)).astype(BF16)

    b_last = each(lambda bb, h: b_col[bb, h][L - 1:L, :])
    log_w = each(lambda bb, h: b_last[bb, h] - b_col[bb, h] + i_col[bb, h])
    m_new = each(lambda bb, h: jnp.maximum(b_last[bb, h] + m_prev[bb, h],
                                           jnp.max(log_w[bb, h], axis=0, keepdims=True)))
    decay = each(lambda bb, h: jnp.exp(b_last[bb, h] + m_prev[bb, h] - m_new[bb, h]))
    kw = each(lambda bb, h: (kh[bb, h] * jnp.exp(log_w[bb, h] - m_new[bb, h])).astype(BF16))
    upd = each(lambda bb, h: _dot_tn(kw[bb, h], v_aug[bb, h]))
    for bb, h in items:
        c_sc[bb, h] = decay[bb, h] * c_sc[bb, h] + upd[bb, h]
        m_sc[bb, h:h + 1, :] = jnp.broadcast_to(m_new[bb, h], (1, LANES))


def _mlstm(proj3, gates3, conv_w, conv_b, gate_bias, head_gain):
    b, s, _ = proj3.shape
    L = M_CHUNK
    return pl.pallas_call(
        _mlstm_kernel,
        grid=(b // ML_NB, s // L),
        in_specs=[
            pl.BlockSpec((ML_NB, L, 1024), lambda i, c: (i, c, COL_MQK // 1024)),
            pl.BlockSpec((ML_NB, L, 1024), lambda i, c: (i, c, COL_MV // 1024)),
            pl.BlockSpec((ML_NB, L, LANES), lambda i, c: (i, c, 0)),
            pl.BlockSpec((M_CONV, 1024), lambda i, c: (0, 0)),
            pl.BlockSpec((1, 1024), lambda i, c: (0, 0)),
            pl.BlockSpec((1, LANES), lambda i, c: (0, 0)),
            pl.BlockSpec((1, BRANCH_WIDTH), lambda i, c: (0, 0)),
        ],
        out_specs=pl.BlockSpec((ML_NB, L, BRANCH_WIDTH), lambda i, c: (i, c, 0)),
        out_shape=jax.ShapeDtypeStruct((b, s, BRANCH_WIDTH), BF16),
        scratch_shapes=[
            pltpu.VMEM((ML_NB, M_HEADS, M_DK, 2 * M_DK), F32),
            pltpu.VMEM((ML_NB, 8, LANES), F32),
            pltpu.VMEM((ML_NB, 8, 1024), F32),
        ],
        compiler_params=pltpu.CompilerParams(
            dimension_semantics=("parallel", "arbitrary"), vmem_limit_bytes=VMEM_LIMIT),
        name="mlstm",
    )(proj3, proj3, gates3, conv_w, conv_b, gate_bias, head_gain)


AT_T = 256
AT_TK = 512


def _halfnorm(x, gain):
    lane = lax.broadcasted_iota(I32, (1, LANES), 1)
    lo = lane < A_DQK
    x2 = x * x
    ms_lo = jnp.sum(jnp.where(lo, x2, 0.0), axis=-1, keepdims=True)
    ms_hi = jnp.sum(jnp.where(lo, 0.0, x2), axis=-1, keepdims=True)
    ms = jnp.where(lo, ms_lo, ms_hi) * (1.0 / A_DQK)
    return x * lax.rsqrt(ms + NORM_EPS) * gain


AV_ROWS = A_DV + 16


AT_NH = 2


def _attn_kernel(slopes_ref, q_ref, k_ref, v_ref, qg_ref, kg_ref, lam_ref, hg_ref, out_ref,
                 k1_sc, k2_sc, vt_sc, s_sc, p_sc, acc_sc, *, lam_init):
    T = AT_T
    TK = AT_TK
    CH = 128
    heads = range(AT_NH)
    qi = pl.program_id(2)
    lane = lax.broadcasted_iota(I32, (1, LANES), 1)
    slope = [slopes_ref[pl.program_id(1) * AT_NH + hd] for hd in heads]

    def cols(hd):
        return slice(hd * LANES, (hd + 1) * LANES)

    @pl.when(qi == 0)
    def _():
        s_len = k_ref.shape[0]
        pos = lax.broadcasted_iota(I32, (s_len, 1), 0)
        for hd in heads:
            kn = _halfnorm(k_ref[:, cols(hd)].astype(F32), kg_ref[...])
            c_loc = (pos & (T - 1)).astype(F32) * slope[hd]
            c_blk = (pos >> int(math.log2(T))).astype(F32) * (slope[hd] * T)
            k1_sc[hd] = jnp.where(lane < A_DQK, kn, jnp.where(lane == A_DQK, c_loc, jnp.where(
                lane == A_DQK + 1, c_blk, 0.0))).astype(BF16)
            k2_sc[hd] = jnp.where(lane >= A_DQK, kn, jnp.where(lane == 0, c_loc, jnp.where(
                lane == 1, c_blk, 0.0))).astype(BF16)
            vt_sc[hd, :A_DV, :] = v_ref[:, cols(hd)].astype(F32).T.astype(BF16)
            vt_sc[hd, A_DV:, :] = jnp.ones((AV_ROWS - A_DV, s_len), BF16)

    lp = lam_ref[...]
    lam = (jnp.exp(jnp.sum(lp[0:1, :] * lp[1:2, :], axis=-1, keepdims=True))
           - jnp.exp(jnp.sum(lp[2:3, :] * lp[3:4, :], axis=-1, keepdims=True)) + lam_init)

    qn = [_halfnorm(q_ref[:, cols(hd)].astype(F32), qg_ref[...]) * (A_DQK ** -0.5) for hd in heads]
    q1 = [jnp.where(lane < A_DQK, qn[hd], jnp.where(lane < A_DQK + 2, 1.0, 0.0)).astype(BF16) for hd in heads]
    q2 = [jnp.where(lane >= A_DQK, qn[hd], jnp.where(lane < 2, 1.0, 0.0)).astype(BF16) for hd in heads]

    def scores_to_scratch(hd, j):
        start = pl.multiple_of(j * TK, TK)
        s_sc[hd, 0] = _dot_nt(k1_sc[hd, pl.ds(start, TK), :], q1[hd])
        s_sc[hd, 1] = _dot_nt(k2_sc[hd, pl.ds(start, TK), :], q2[hd])

    def softmax_to_scratch(hd, j, m, masked):
        m_out, alpha_out = [], []
        for mp in range(2):
            for hh in range(T // LANES):
                idx = mp * (T // LANES) + hh
                lanes = slice(hh * LANES, (hh + 1) * LANES)

                def chunk(c):
                    sc = s_sc[hd, mp, c * CH:(c + 1) * CH, lanes]
                    if masked:
                        key = j * TK + c * CH + lax.broadcasted_iota(I32, (CH, LANES), 0)
                        qry = qi * T + hh * LANES + lax.broadcasted_iota(I32, (CH, LANES), 1)
                        sc = jnp.where(key <= qry, sc, NEG_BIG)
                    return sc

                m_new = m[idx]
                for c in range(TK // CH):
                    m_new = jnp.maximum(m_new, jnp.max(chunk(c), axis=0, keepdims=True))
                for c in range(TK // CH):
                    p_sc[hd, c * CH:(c + 1) * CH, idx * LANES:(idx + 1) * LANES] = (
                        jnp.exp(chunk(c) - m_new).astype(BF16))
                alpha_out.append(jnp.exp(m[idx] - m_new))
                m_out.append(m_new)
        return tuple(m_out), tuple(alpha_out)

    def values_from_scratch(hd, j, alpha):
        vt = vt_sc[hd, :, pl.ds(pl.multiple_of(j * TK, TK), TK)]
        pv = _dot(vt, p_sc[hd])
        for idx in range(2 * T // LANES):
            sl = slice(idx * LANES, (idx + 1) * LANES)
            acc_sc[hd, :, sl] = alpha[idx] * acc_sc[hd, :, sl] + pv[:, sl]

    def body(j, carry):
        m, alpha = carry
        for hd in heads:
            values_from_scratch(hd, jnp.maximum(j - 1, 0), alpha[hd])
        stats = [softmax_to_scratch(hd, j, m[hd], False) for hd in heads]
        for hd in heads:
            scores_to_scratch(hd, j + 1)
        return tuple(st[0] for st in stats), tuple(st[1] for st in stats)

    n_slab = 2 * T // LANES
    last = (qi * T) >> int(math.log2(TK))
    p_sc[...] = jnp.zeros_like(p_sc)
    acc_sc[...] = jnp.zeros_like(acc_sc)
    for hd in heads:
        scores_to_scratch(hd, 0)
    m, alpha = lax.fori_loop(0, last, body, (((jnp.full((1, LANES), NEG_BIG, F32),) * n_slab,) * AT_NH,
                                             ((jnp.ones((1, LANES), F32),) * n_slab,) * AT_NH))
    for hd in heads:
        values_from_scratch(hd, jnp.maximum(last - 1, 0), alpha[hd])
    alpha = [softmax_to_scratch(hd, last, m[hd], True)[1] for hd in heads]
    for hd in heads:
        values_from_scratch(hd, last, alpha[hd])

    hg = hg_ref[...]
    for hd in heads:
        a1 = acc_sc[hd, :, :T]
        a2 = acc_sc[hd, :, T:]
        o_t = a1[:A_DV, :] / a1[A_DV:A_DV + 1, :] - lam * (a2[:A_DV, :] / a2[A_DV:A_DV + 1, :])
        out_ref[:, cols(hd)] = (_rms(o_t.T, hg[:, cols(hd)]) * (1.0 - lam_init)).astype(BF16)


def _attention(proj3, q_gain, k_gain, lam_params, head_gain, lam_init):
    b, s, _ = proj3.shape
    T = AT_T
    W = AT_NH * LANES
    slopes = jnp.asarray(ALIBI_SLOPES, F32)
    return pl.pallas_call(
        functools.partial(_attn_kernel, lam_init=lam_init),
        grid=(b, A_HEADS // AT_NH, s // T),
        in_specs=[
            pl.BlockSpec(memory_space=pltpu.SMEM),
            pl.BlockSpec((None, T, W), lambda i, h, q: (i, q, COL_AQ // W + h)),
            pl.BlockSpec((None, s, W), lambda i, h, q: (i, 0, COL_AK // W + h)),
            pl.BlockSpec((None, s, W), lambda i, h, q: (i, 0, COL_AV // W + h)),
            pl.BlockSpec((1, LANES), lambda i, h, q: (0, 0)),
            pl.BlockSpec((1, LANES), lambda i, h, q: (0, 0)),
            pl.BlockSpec((4, A_DQK), lambda i, h, q: (0, 0)),
            pl.BlockSpec((1, W), lambda i, h, q: (0, h)),
        ],
        out_specs=pl.BlockSpec((None, T, W), lambda i, h, q: (i, q, h)),
        out_shape=jax.ShapeDtypeStruct((b, s, BRANCH_WIDTH), BF16),
        scratch_shapes=[pltpu.VMEM((AT_NH, s, LANES), BF16), pltpu.VMEM((AT_NH, s, LANES), BF16),
                        pltpu.VMEM((AT_NH, AV_ROWS, s), BF16), pltpu.VMEM((AT_NH, 2, AT_TK, T), F32),
                        pltpu.VMEM((AT_NH, AT_TK, 2 * T), BF16), pltpu.VMEM((AT_NH, AV_ROWS, 2 * T), F32)],
        compiler_params=pltpu.CompilerParams(
            dimension_semantics=("parallel", "parallel", "arbitrary"), vmem_limit_bytes=VMEM_LIMIT),
        name="diff_attn",
    )(slopes, proj3, proj3, proj3, q_gain, k_gain, lam_params, head_gain)


MG_TM = 512
MG_HALO = 128
ROUTER_ROWS = 16
ROUTE_ROWS = 8


def _merge_kernel(x_ref, hm_ref, ha_ref, u_ref, halo_ref, gm_ref, ga_ref, gp_ref, pw_ref, ps_ref,
                  wb_ref, wo_ref, fg_ref, *rest, seq_tiles, routed):
    if routed:
        wr_ref, br_ref, xo_ref, hf_ref, route_ref = rest
    else:
        xo_ref, hf_ref = rest
    TM = MG_TM
    i = pl.program_id(0)
    seq_tile = i % seq_tiles

    u = u_ref[...]
    halo = jnp.where(seq_tile > 0, halo_ref[...], jnp.zeros_like(halo_ref))
    u_ext = jnp.concatenate([halo, u], axis=0)
    dist = (lax.broadcasted_iota(I32, (MG_HALO, 2 * MG_HALO), 0) + MG_HALO
            - lax.broadcasted_iota(I32, (MG_HALO, 2 * MG_HALO), 1))
    t_pos = seq_tile * TM + lax.broadcasted_iota(I32, (TM, 1), 0)
    ps = ps_ref[...]
    hp_parts = []
    for g, w in enumerate(P_WINDOWS):
        sl = slice(g * P_GC, (g + 1) * P_GC)
        band = jnp.where((dist >= 0) & (dist < w), 1.0, 0.0).astype(BF16)
        ug = u[:, sl]
        sums = jnp.concatenate([_dot(band, u_ext[r:r + 2 * MG_HALO, sl]) for r in range(0, TM, MG_HALO)],
                               axis=0)
        cnt = jnp.minimum(t_pos + 1, w).astype(F32)
        pooled = sums / cnt - ug.astype(F32)
        hp_parts.append((_dot(pooled.astype(BF16), pw_ref[g]) * ps[:, sl]).astype(BF16))
    hp = jnp.concatenate(hp_parts, axis=1)

    merged = (_sigmoid(gm_ref[...].astype(F32)) * _dot(hm_ref[...], wb_ref[0])
              + _sigmoid(ga_ref[...].astype(F32)) * _dot(ha_ref[...], wb_ref[1])
              + _sigmoid(gp_ref[...].astype(F32)) * _dot(hp, wb_ref[2]))
    x_new = x_ref[...] + _dot(merged.astype(BF16), wo_ref[...])
    xo_ref[...] = x_new
    hf = _rms(x_new, fg_ref[...])
    hf_ref[...] = hf.astype(BF16)

    if routed:
        hf_hi = hf.astype(BF16)
        hf_lo = (hf - hf_hi.astype(F32)).astype(BF16)
        wr = wr_ref[...]
        wr_hi = wr.astype(BF16)
        wr_lo = (wr - wr_hi.astype(F32)).astype(BF16)
        logits = (_dot_nt(wr_hi, hf_hi) + _dot_nt(wr_lo, hf_hi) + _dot_nt(wr_hi, hf_lo))[:N_EXPERTS, :]
        logits = logits + br_ref[...]
        expert = lax.broadcasted_iota(I32, (N_EXPERTS, TM), 0).astype(F32)
        m1 = jnp.max(logits, axis=0, keepdims=True)
        i1 = jnp.min(jnp.where(logits == m1, expert, float(N_EXPERTS)), axis=0, keepdims=True)
        rest_l = jnp.where(expert == i1, NEG_BIG, logits)
        m2 = jnp.max(rest_l, axis=0, keepdims=True)
        i2 = jnp.min(jnp.where(rest_l == m2, expert, float(N_EXPERTS)), axis=0, keepdims=True)
        g1 = 1.0 / (1.0 + jnp.exp(m2 - m1))
        route_ref[...] = jnp.concatenate(
            [g1, 1.0 - g1, i1, i2, jnp.zeros((ROUTE_ROWS - 4, TM), F32)], axis=0)


def _merge(x2, hm2, ha2, proj2, pool_w, pool_scale, w_branch, w_out, ffn_gain, seq, router=None):
    tok = x2.shape[0]
    TM = MG_TM
    routed = router is not None
    hb = TM // MG_HALO
    in_specs = [
        pl.BlockSpec((TM, D_MODEL), lambda i: (i, 0)),
        pl.BlockSpec((TM, BRANCH_WIDTH), lambda i: (i, 0)),
        pl.BlockSpec((TM, BRANCH_WIDTH), lambda i: (i, 0)),
        pl.BlockSpec((TM, BRANCH_WIDTH), lambda i: (i, COL_PU // BRANCH_WIDTH)),
        pl.BlockSpec((MG_HALO, BRANCH_WIDTH), lambda i: (jnp.maximum(i * hb - 1, 0), COL_PU // BRANCH_WIDTH)),
        pl.BlockSpec((TM, D_MODEL), lambda i: (i, COL_G // D_MODEL)),
        pl.BlockSpec((TM, D_MODEL), lambda i: (i, COL_G // D_MODEL + 1)),
        pl.BlockSpec((TM, D_MODEL), lambda i: (i, COL_G // D_MODEL + 2)),
        pl.BlockSpec((len(P_WINDOWS), P_GC, P_GC), lambda i: (0, 0, 0)),
        pl.BlockSpec((1, BRANCH_WIDTH), lambda i: (0, 0)),
        pl.BlockSpec((3, BRANCH_WIDTH, D_MODEL), lambda i: (0, 0, 0)),
        pl.BlockSpec((D_MODEL, D_MODEL), lambda i: (0, 0)),
        pl.BlockSpec((1, D_MODEL), lambda i: (0, 0)),
    ]
    args = [x2, hm2, ha2, proj2, proj2, proj2, proj2, proj2, pool_w, pool_scale, w_branch, w_out, ffn_gain]
    out_specs = [pl.BlockSpec((TM, D_MODEL), lambda i: (i, 0)),
                 pl.BlockSpec((TM, D_MODEL), lambda i: (i, 0))]
    out_shape = [jax.ShapeDtypeStruct((tok, D_MODEL), F32), jax.ShapeDtypeStruct((tok, D_MODEL), BF16)]
    if routed:
        in_specs += [pl.BlockSpec((ROUTER_ROWS, D_MODEL), lambda i: (0, 0)),
                     pl.BlockSpec((N_EXPERTS, 1), lambda i: (0, 0))]
        args += list(router)
        out_specs.append(pl.BlockSpec((ROUTE_ROWS, TM), lambda i: (0, i)))
        out_shape.append(jax.ShapeDtypeStruct((ROUTE_ROWS, tok), F32))
    return pl.pallas_call(
        functools.partial(_merge_kernel, seq_tiles=seq // TM, routed=routed),
        grid=(tok // TM,),
        in_specs=in_specs,
        out_specs=out_specs,
        out_shape=out_shape,
        compiler_params=pltpu.CompilerParams(
            dimension_semantics=("parallel",), vmem_limit_bytes=VMEM_LIMIT),
        name="merge_routed" if routed else "merge",
    )(*args)


def _ple_epilogue(x_new, p, pg_gain, wpg, wpp):
    gate = _sigmoid(_dot(_rms(x_new, pg_gain).astype(BF16), wpg))
    return x_new + gate * _dot(p.astype(BF16), wpp)


FF_TM = 512


def _ffn_kernel(hf_ref, x_ref, p_ref, wgu_ref, wd_ref, pgn_ref, wpg_ref, wpp_ref, out_ref):
    hf = hf_ref[...]
    g = _dot(hf, wgu_ref[:, :D_FF])
    u = _dot(hf, wgu_ref[:, D_FF:])
    y = _dot((g * _sigmoid(g) * u).astype(BF16), wd_ref[...])
    out_ref[...] = _ple_epilogue(x_ref[...] + y, p_ref[...], pgn_ref[...], wpg_ref[...], wpp_ref[...])


def _ffn(hf2, x2, p2, p_row0, w_gu, w_down, ple_gain, wpg, wpp):
    tok = x2.shape[0]
    TM = FF_TM
    once = pl.Buffered(1)
    return pl.pallas_call(
        _ffn_kernel,
        grid=(tok // TM,),
        in_specs=[
            pl.BlockSpec((TM, D_MODEL), lambda i: (i, 0)),
            pl.BlockSpec((TM, D_MODEL), lambda i: (i, 0)),
            pl.BlockSpec((TM, PLE_DIM), lambda i: (i + p_row0 // TM, 0)),
            pl.BlockSpec((D_MODEL, 2 * D_FF), lambda i: (0, 0), pipeline_mode=once),
            pl.BlockSpec((D_FF, D_MODEL), lambda i: (0, 0), pipeline_mode=once),
            pl.BlockSpec((1, D_MODEL), lambda i: (0, 0)),
            pl.BlockSpec((D_MODEL, D_MODEL), lambda i: (0, 0), pipeline_mode=once),
            pl.BlockSpec((PLE_DIM, D_MODEL), lambda i: (0, 0), pipeline_mode=once),
        ],
        out_specs=pl.BlockSpec((TM, D_MODEL), lambda i: (i, 0)),
        out_shape=jax.ShapeDtypeStruct((tok, D_MODEL), F32),
        compiler_params=pltpu.CompilerParams(
            dimension_semantics=("parallel",), vmem_limit_bytes=VMEM_LIMIT),
        name="ffn_dense",
    )(hf2, x2, p2, w_gu, w_down, ple_gain, wpg, wpp)


MOE_TM = 1024
MOE_TF = 512
DP_TM = 256
DP_CH = 256
CB_TB = 256
CB_ALIGN = 16
CB_W = CB_TB + CB_ALIGN


def _dispatch_kernel(clo_ref, chi_ref, pos_ref, gate_ref, hf_ref, x_ref, g_ref, acc_sc, gacc_sc):
    i = pl.program_id(0)
    rows = i * DP_TM + lax.broadcasted_iota(I32, (DP_TM, 1), 0)
    acc_sc[...] = jnp.zeros_like(acc_sc)
    gacc_sc[...] = jnp.zeros_like(gacc_sc)

    lo = clo_ref[i]
    hi = chi_ref[i]

    def chunk(c, live):
        start = pl.multiple_of(c * DP_CH, DP_CH)
        pos = pos_ref[:, pl.ds(start, DP_CH)]
        gate = gate_ref[:, pl.ds(start, DP_CH)]
        want = jnp.where(live, rows, -1)
        hit1 = pos[0:1, :] == want
        hit2 = pos[1:2, :] == want
        onehot = jnp.where(hit1, 1.0, jnp.where(hit2, 1.0, 0.0)).astype(BF16)
        gates = jnp.sum(jnp.where(hit1, gate[0:1, :], jnp.where(hit2, gate[1:2, :], 0.0)),
                        axis=-1, keepdims=True)
        return _dot(onehot, hf_ref[pl.ds(start, DP_CH), :]), gates

    def pair(t, carry):
        c = lo + 2 * t
        x0, g0 = chunk(c, True)
        x1, g1 = chunk(jnp.minimum(c + 1, hi), c + 1 <= hi)
        acc_sc[...] += x0 + x1
        gacc_sc[...] += g0 + g1
        return carry

    lax.fori_loop(0, (hi - lo + 2) >> 1, pair, 0)
    x_ref[...] = acc_sc[...].astype(BF16)
    g_ref[...] = gacc_sc[...]


def _dispatch(hf2, pos_t, gate_t, chunk_lo, chunk_hi):
    tok = hf2.shape[0]
    ns = chunk_lo.shape[0]
    grid_spec = pltpu.PrefetchScalarGridSpec(
        num_scalar_prefetch=2,
        grid=(ns,),
        in_specs=[
            pl.BlockSpec((2, tok), lambda i, lo, hi: (0, 0)),
            pl.BlockSpec((2, tok), lambda i, lo, hi: (0, 0)),
            pl.BlockSpec((tok, D_MODEL), lambda i, lo, hi: (0, 0), pipeline_mode=pl.Buffered(1)),
        ],
        out_specs=[pl.BlockSpec((DP_TM, D_MODEL), lambda i, lo, hi: (i, 0)),
                   pl.BlockSpec((DP_TM, 1), lambda i, lo, hi: (i, 0))],
        scratch_shapes=[pltpu.VMEM((DP_TM, D_MODEL), F32), pltpu.VMEM((DP_TM, 1), F32)],
    )
    return pl.pallas_call(
        _dispatch_kernel,
        grid_spec=grid_spec,
        out_shape=[jax.ShapeDtypeStruct((ns * DP_TM, D_MODEL), BF16),
                   jax.ShapeDtypeStruct((ns * DP_TM, 1), F32)],
        compiler_params=pltpu.CompilerParams(
            dimension_semantics=("arbitrary",), vmem_limit_bytes=VMEM_LIMIT),
        name="moe_dispatch",
    )(chunk_lo, chunk_hi, pos_t, gate_t, hf2)


def _moe_kernel(te_ref, tv_ref, x_ref, gate_ref, wg_ref, wu_ref, wd_ref, y_ref, acc_sc):
    i = pl.program_id(0)
    j = pl.program_id(1)
    last = pl.num_programs(1) - 1
    valid = tv_ref[i] > 0

    @pl.when(valid & (j == 0))
    def _():
        acc_sc[...] = jnp.zeros_like(acc_sc)

    @pl.when(valid)
    def _():
        xb = x_ref[...]
        g = _dot(xb, wg_ref[...].astype(BF16))
        u = _dot(xb, wu_ref[...].astype(BF16))
        acc_sc[...] += _dot((g * _sigmoid(g) * u).astype(BF16), wd_ref[...].astype(BF16))

        @pl.when(j == last)
        def _():
            y_ref[...] = (acc_sc[...] * gate_ref[...]).astype(BF16)

    @pl.when(jnp.logical_not(valid) & (j == last))
    def _():
        y_ref[...] = jnp.zeros_like(y_ref)


def _moe(x_sorted, gate_sorted, w_gu, w_down, tile_expert, tile_valid):
    TM, TF = MOE_TM, MOE_TF
    nt = tile_expert.shape[0]
    nf = D_FF_EXPERT // TF

    def col(j, tv, i):
        return jnp.where(tv[i] > 0, j, nf - 1)

    grid_spec = pltpu.PrefetchScalarGridSpec(
        num_scalar_prefetch=2,
        grid=(nt, nf),
        in_specs=[
            pl.BlockSpec((TM, D_MODEL), lambda i, j, te, tv: (i, 0)),
            pl.BlockSpec((TM, 1), lambda i, j, te, tv: (i, 0)),
            pl.BlockSpec((None, D_MODEL, TF), lambda i, j, te, tv: (te[i], 0, col(j, tv, i))),
            pl.BlockSpec((None, D_MODEL, TF), lambda i, j, te, tv: (te[i], 0, col(j, tv, i) + nf)),
            pl.BlockSpec((None, TF, D_MODEL), lambda i, j, te, tv: (te[i], col(j, tv, i), 0)),
        ],
        out_specs=pl.BlockSpec((TM, D_MODEL), lambda i, j, te, tv: (i, 0)),
        scratch_shapes=[pltpu.VMEM((TM, D_MODEL), F32)],
    )
    return pl.pallas_call(
        _moe_kernel,
        grid_spec=grid_spec,
        out_shape=jax.ShapeDtypeStruct((nt * TM, D_MODEL), BF16),
        compiler_params=pltpu.CompilerParams(
            dimension_semantics=("arbitrary", "arbitrary"), vmem_limit_bytes=VMEM_LIMIT),
        name="moe_experts",
    )(tile_expert, tile_valid, x_sorted, gate_sorted, w_gu, w_gu, w_down)


def _combine_kernel(w0_ref, rng_ref, x_ref, p_ref, pos_ref, y_hbm, pgn_ref, wpg_ref, wpp_ref, out_ref,
                    win_sc, acc_sc, sem):
    i = pl.program_id(0)
    slot = i & 1

    def window_copy(step, buf, e):
        start = pl.multiple_of(w0_ref[step * N_EXPERTS + e], CB_ALIGN)
        return pltpu.make_async_copy(y_hbm.at[pl.ds(start, CB_W), :], win_sc.at[buf, e], sem.at[buf, e])

    @pl.when(i == 0)
    def _():
        for e in range(N_EXPERTS):
            window_copy(0, 0, e).start()

    @pl.when(i + 1 < pl.num_programs(0))
    def _():
        for e in range(N_EXPERTS):
            window_copy(i + 1, 1 - slot, e).start()

    pos = pos_ref[...]
    pos1 = pos[:, 0:1]
    pos2 = pos[:, 1:2]

    n_win = pl.num_programs(0) * N_EXPERTS

    def row_begin(e):
        return rng_ref[i * N_EXPERTS + e]

    def row_end(e):
        return rng_ref[n_win + i * N_EXPERTS + e]

    def onehot(e, first, width):
        rows = lax.broadcasted_iota(I32, (1, width), 1) + (first + w0_ref[i * N_EXPERTS + e])
        rows = jnp.where((rows >= row_begin(e)) & (rows < row_end(e)), rows, -1)
        return jnp.where(pos1 == rows, 1.0, jnp.where(pos2 == rows, 1.0, 0.0)).astype(BF16)

    moe_out = jnp.zeros((CB_TB, D_MODEL), F32)
    for e in range(N_EXPERTS):
        window_copy(i, slot, e).wait()
        moe_out = moe_out + _dot(onehot(e, 0, CB_TB), win_sc[slot, e, :CB_TB, :])
    acc_sc[...] = moe_out

    for e in range(N_EXPERTS):
        @pl.when(row_end(e) - w0_ref[i * N_EXPERTS + e] > CB_TB)
        def _():
            acc_sc[...] += _dot(onehot(e, CB_TB, CB_ALIGN), win_sc[slot, e, CB_TB:, :])

    out_ref[...] = _ple_epilogue(x_ref[...] + acc_sc[...], p_ref[...], pgn_ref[...], wpg_ref[...],
                                 wpp_ref[...])


def _combine(x2, p2, p_row0, pos, y_sorted, win_start, win_rows, ple_gain, wpg, wpp):
    tok = x2.shape[0]
    TB = CB_TB
    grid_spec = pltpu.PrefetchScalarGridSpec(
        num_scalar_prefetch=2,
        grid=(tok // TB,),
        in_specs=[
            pl.BlockSpec((TB, D_MODEL), lambda i, w0, wt: (i, 0)),
            pl.BlockSpec((TB, PLE_DIM), lambda i, w0, wt: (i + p_row0 // TB, 0)),
            pl.BlockSpec((TB, 2), lambda i, w0, wt: (i, 0)),
            pl.BlockSpec(memory_space=pl.ANY),
            pl.BlockSpec((1, D_MODEL), lambda i, w0, wt: (0, 0)),
            pl.BlockSpec((D_MODEL, D_MODEL), lambda i, w0, wt: (0, 0)),
            pl.BlockSpec((PLE_DIM, D_MODEL), lambda i, w0, wt: (0, 0)),
        ],
        out_specs=pl.BlockSpec((TB, D_MODEL), lambda i, w0, wt: (i, 0)),
        scratch_shapes=[pltpu.VMEM((2, N_EXPERTS, CB_W, D_MODEL), BF16), pltpu.VMEM((TB, D_MODEL), F32),
                        pltpu.SemaphoreType.DMA((2, N_EXPERTS))],
    )
    return pl.pallas_call(
        _combine_kernel,
        grid_spec=grid_spec,
        out_shape=jax.ShapeDtypeStruct((tok, D_MODEL), F32),
        compiler_params=pltpu.CompilerParams(
            dimension_semantics=("arbitrary",), vmem_limit_bytes=VMEM_LIMIT),
        name="moe_combine",
    )(win_start, win_rows, x2, p2, pos, y_sorted, ple_gain, wpg, wpp)


def _route_metadata(route, tok):
    TM = MOE_TM
    nt = (2 * tok) // TM + N_EXPERTS
    rows = nt * TM
    g1, g2 = route[0], route[1]
    i1, i2 = route[2].astype(I32), route[3].astype(I32)
    experts = jnp.arange(N_EXPERTS, dtype=I32)
    member = ((i1[:, None] == experts) | (i2[:, None] == experts)).astype(I32)
    rank = jnp.cumsum(member, axis=0) - member
    counts = jnp.sum(member, axis=0)
    tiles_e = (counts + TM - 1) // TM
    tile_end = jnp.cumsum(tiles_e)
    tile_start = tile_end - tiles_e
    seg_start = tile_start * TM
    sel1 = i1[:, None] == experts
    sel2 = i2[:, None] == experts
    pos1 = jnp.sum(jnp.where(sel1, seg_start[None, :] + rank, 0), axis=1).astype(I32)
    pos2 = jnp.sum(jnp.where(sel2, seg_start[None, :] + rank, 0), axis=1).astype(I32)

    tiles = jnp.arange(nt, dtype=I32)
    n_used = tile_end[-1]
    tile_valid = (tiles < n_used).astype(I32)
    te = jnp.sum((tiles[:, None] >= tile_end[None, :]).astype(I32), axis=1)
    te_last = jnp.sum(((n_used - 1) >= tile_end).astype(I32))
    tile_expert = jnp.where(tile_valid > 0, jnp.minimum(te, N_EXPERTS - 1), te_last).astype(I32)

    steps = jnp.arange(rows // DP_TM, dtype=I32)
    step_tile = steps // (TM // DP_TM)
    step_valid = step_tile < n_used
    step_expert = jnp.minimum(jnp.sum((step_tile[:, None] >= tile_end[None, :]).astype(I32), axis=1),
                              N_EXPERTS - 1)
    pick = step_expert[:, None] == experts[None, :]
    rank_lo = steps * DP_TM - jnp.sum(jnp.where(pick, seg_start[None, :], 0), axis=1)
    rank_hi = jnp.minimum(rank_lo + DP_TM, jnp.sum(jnp.where(pick, counts[None, :], 0), axis=1)) - 1
    before = jnp.sum(jnp.where(pick[:, :, None], rank[::DP_CH, :].T[None, :, :], 0), axis=1)
    chunk_lo = jnp.sum((before <= rank_lo[:, None]).astype(I32), axis=1) - 1
    chunk_hi = jnp.sum((before <= rank_hi[:, None]).astype(I32), axis=1) - 1
    live = step_valid & (rank_hi >= rank_lo)
    chunk_lo = jnp.where(live, chunk_lo, 0).astype(I32)
    chunk_hi = jnp.where(live, chunk_hi, -1).astype(I32)

    blk_start = seg_start[None, :] + rank[::CB_TB, :]
    win_start = jnp.minimum((blk_start // CB_ALIGN) * CB_ALIGN, rows - CB_W).astype(I32)
    blk_count = jnp.concatenate([rank[CB_TB::CB_TB, :], counts[None, :]], axis=0) - rank[::CB_TB, :]
    win_rows = jnp.concatenate([blk_start.reshape(-1), (blk_start + blk_count).reshape(-1)]).astype(I32)
    win_start = win_start.reshape(-1)
    pos = jnp.stack([pos1, pos2], axis=1)
    return (tile_expert, tile_valid, chunk_lo, chunk_hi, jnp.stack([pos1, pos2]), jnp.stack([g1, g2]),
            pos, win_start, win_rows)


def _pack_w_in(w):
    o = [0, 1024, 1536, 2048, 2052, 2056, 2568, 3080, 3592, 4104, 7176]
    main = jnp.concatenate([w[:, o[0]:o[3]], w[:, o[5]:o[10]]], axis=1).astype(BF16)
    gates = jnp.pad(w[:, o[3]:o[5]], ((0, 0), (0, LANES - 2 * M_HEADS))).astype(BF16)
    return main, gates


def kernel(x, p, attn_norm, w_in, m_conv_w, m_conv_b, m_gate_bias, m_head_norm, a_q_norm, a_k_norm,
           a_lambda, a_head_norm, pool_w, pool_scale, w_branch, w_out, ffn_norm, dense_w_gu, dense_w_down,
           router_w, router_b, moe_w_gu, moe_w_down, ple_norm, ple_w_gate, ple_w_proj):
    b, s, d = x.shape
    depth = w_in.shape[0]
    tok = b * s
    x2 = x.reshape(tok, d)
    for layer in range(depth):
        w_main, w_if = _pack_w_in(w_in[layer])
        proj, gates = _inproj(x2, attn_norm[layer].reshape(1, d), w_main, w_if)
        proj3 = proj.reshape(b, s, PROJ_WIDTH)
        gate_bias = jnp.pad(m_gate_bias[layer], (0, LANES - 2 * M_HEADS)).reshape(1, LANES)
        h_m = _mlstm(proj3, gates.reshape(b, s, LANES), m_conv_w[layer], m_conv_b[layer].reshape(1, -1),
                     gate_bias, m_head_norm[layer].reshape(1, -1))
        lam_init = 0.8 - 0.6 * math.exp(-0.3 * layer)
        h_a = _attention(proj3, jnp.tile(a_q_norm[layer], 2).reshape(1, LANES),
                         jnp.tile(a_k_norm[layer], 2).reshape(1, LANES), a_lambda[layer],
                         a_head_norm[layer].reshape(1, -1), lam_init)
        p2 = p.reshape(depth * tok, PLE_DIM)
        ple_args = (ple_norm[layer].reshape(1, d), ple_w_gate[layer].astype(BF16), ple_w_proj[layer].astype(BF16))
        merge_args = (x2, h_m.reshape(tok, BRANCH_WIDTH), h_a.reshape(tok, BRANCH_WIDTH), proj,
                      pool_w[layer].astype(BF16), pool_scale[layer].reshape(1, -1),
                      w_branch[layer].astype(BF16), w_out[layer].astype(BF16), ffn_norm[layer].reshape(1, d), s)
        j = layer // 2
        if layer % 2 == 0:
            x_mid, hf = _merge(*merge_args)
            x2 = _ffn(hf, x_mid, p2, layer * tok, dense_w_gu[j].astype(BF16), dense_w_down[j].astype(BF16), *ple_args)
        else:
            wr = jnp.pad(router_w[j].T, ((0, ROUTER_ROWS - N_EXPERTS), (0, 0)))
            br = router_b[j].reshape(N_EXPERTS, 1)
            x_mid, hf, route = _merge(*merge_args, router=(wr, br))
            (tile_expert, tile_valid, chunk_lo, chunk_hi, pos_t, gate_t, pos,
             win_start, win_rows) = _route_metadata(route, tok)
            x_sorted, gate_sorted = _dispatch(hf, pos_t, gate_t, chunk_lo, chunk_hi)
            y_sorted = _moe(x_sorted, gate_sorted, moe_w_gu[j], moe_w_down[j], tile_expert, tile_valid)
            x2 = _combine(x_mid, p2, layer * tok, pos, y_sorted, win_start, win_rows, *ple_args)
    return x2.reshape(b, s, d)
```

```python
import functools
import math

import jax
import jax.numpy as jnp
from jax import lax
from jax.experimental import pallas as pl
from jax.experimental.pallas import tpu as pltpu

F32 = jnp.float32
BF16 = jnp.bfloat16
I32 = jnp.int32

D_MODEL = 1024
PLE_DIM = 256
NORM_EPS = 1e-6
BRANCH_WIDTH = 512

M_HEADS = 4
M_DK = 128
M_CONV = 4
M_CHUNK = 128

A_HEADS = 4
A_DV = 128
A_DQK = 64
ALIBI_SLOPES = tuple(2.0 ** (-8.0 * (h + 1) / A_HEADS) for h in range(A_HEADS))

P_WINDOWS = (2, 4, 8, 16)
P_GC = 128

D_FF = 2816
N_EXPERTS = 8
D_FF_EXPERT = 3584

PROJ_WIDTH = 7168
COL_MQK, COL_MV, COL_MO, COL_AQ, COL_AK, COL_AV, COL_PU, COL_G = 0, 1024, 1536, 2048, 2560, 3072, 3584, 4096

LANES = 128
NEG_BIG = -1e30

VMEM_LIMIT = 56 * 1024 * 1024


def _sigmoid(x):
    return 0.5 * jnp.tanh(0.5 * x) + 0.5


def _rms(x, gain):
    return x * lax.rsqrt(jnp.mean(x * x, axis=-1, keepdims=True) + NORM_EPS) * gain


def _dot(a, b):
    return jnp.dot(a, b, preferred_element_type=F32)


def _dot_nt(a, b):
    return lax.dot_general(a, b, (((1,), (1,)), ((), ())), preferred_element_type=F32)


def _dot_tn(a, b):
    return lax.dot_general(a, b, (((0,), (0,)), ((), ())), preferred_element_type=F32)


def _layer_spec(layer, shape, **kw):
    zeros = (0,) * len(shape)
    return pl.BlockSpec((None,) + tuple(shape), lambda *_: (layer,) + zeros, **kw)


IN_TM = 512
IN_TN = 1024


def _inproj_kernel(x_ref, gain_ref, w_ref, wif_ref, out_ref, gates_ref):
    hn = _rms(x_ref[...], gain_ref[...]).astype(BF16)
    gates_ref[...] = _dot(hn, wif_ref[...])
    for n in range(PROJ_WIDTH // IN_TN):
        cols = slice(n * IN_TN, (n + 1) * IN_TN)
        out_ref[:, cols] = _dot(hn, w_ref[:, cols]).astype(BF16)


def _inproj(x2, layer, gain, w_main, w_if):
    tok = x2.shape[0]
    return pl.pallas_call(
        _inproj_kernel,
        grid=(tok // IN_TM,),
        in_specs=[
            pl.BlockSpec((IN_TM, D_MODEL), lambda i: (i, 0)),
            _layer_spec(layer, (1, D_MODEL)),
            _layer_spec(layer, (D_MODEL, PROJ_WIDTH), pipeline_mode=pl.Buffered(1)),
            _layer_spec(layer, (D_MODEL, LANES)),
        ],
        out_specs=[
            pl.BlockSpec((IN_TM, PROJ_WIDTH), lambda i: (i, 0)),
            pl.BlockSpec((IN_TM, LANES), lambda i: (i, 0)),
        ],
        out_shape=[
            jax.ShapeDtypeStruct((tok, PROJ_WIDTH), BF16),
            jax.ShapeDtypeStruct((tok, LANES), F32),
        ],
        compiler_params=pltpu.CompilerParams(
            dimension_semantics=("parallel",), vmem_limit_bytes=VMEM_LIMIT),
        name="inproj",
    )(x2, gain, w_main, w_if)


ML_NB = 2


def _mlstm_kernel(qk_ref, vo_ref, g_ref, cw_ref, cb_ref, gb_ref, hg_ref, out_ref,
                  c_sc, m_sc, prev_sc):
    @pl.when(pl.program_id(1) == 0)
    def _():
        c_sc[...] = jnp.zeros_like(c_sc)
        m_sc[...] = jnp.zeros_like(m_sc)
        prev_sc[...] = jnp.zeros_like(prev_sc)

    L = M_CHUNK
    row1 = lax.broadcasted_iota(I32, (L, 1), 0)
    lane = lax.broadcasted_iota(I32, (L, LANES), 1)
    rowl = lax.broadcasted_iota(I32, (L, LANES), 0)
    causal = lax.broadcasted_iota(I32, (L, L), 0) >= lax.broadcasted_iota(I32, (L, L), 1)
    is_f = (lane >= M_HEADS) & (lane < 2 * M_HEADS)
    cw = cw_ref[...]
    hg = hg_ref[...]
    items = [(bb, h) for bb in range(ML_NB) for h in range(M_HEADS)]

    qk, gc, bcum, gc_t, bcum_t = [], [], [], [], []
    for bb in range(ML_NB):
        x = qk_ref[bb].astype(F32)
        tail = prev_sc[bb]
        y = cb_ref[...] + cw[M_CONV - 1:M_CONV, :] * x
        for s in range(1, M_CONV):
            rolled = pltpu.roll(x, s, 0)
            first = jnp.where(row1[:8] < s, pltpu.roll(tail, s, 0), rolled[:8])
            y = y + cw[M_CONV - 1 - s:M_CONV - s, :] * jnp.concatenate([first, rolled[8:]], axis=0)
        prev_sc[bb] = x[L - 8:]
        qk.append(y * _sigmoid(y))

        g = g_ref[bb] + gb_ref[...]
        logf = jnp.minimum(g, 0.0) - jnp.log(1.0 + jnp.exp(-jnp.abs(g)))
        gcb = jnp.where(is_f, logf, jnp.where(lane < M_HEADS, g, 0.0))
        acc = jnp.where(is_f, logf, 0.0)
        k = 1
        while k < L:
            acc = acc + jnp.where(rowl >= k, pltpu.roll(acc, k, 0), 0.0)
            k *= 2
        gc.append(gcb)
        bcum.append(acc)
        gc_t.append(gcb.T)
        bcum_t.append(acc.T)

    def each(fn):
        return {it: fn(*it) for it in items}

    def head(h):
        return slice(h * M_DK, (h + 1) * M_DK)

    b_col = each(lambda bb, h: bcum[bb][:, M_HEADS + h:M_HEADS + h + 1])
    i_col = each(lambda bb, h: gc[bb][:, h:h + 1])
    m_prev = each(lambda bb, h: m_sc[bb, h:h + 1, 0:1])
    log_d = each(lambda bb, h: jnp.where(
        causal, b_col[bb, h] - bcum_t[bb][M_HEADS + h:M_HEADS + h + 1, :] + gc_t[bb][h:h + 1, :], NEG_BIG))
    inter = each(lambda bb, h: b_col[bb, h] + m_prev[bb, h])
    m_t = each(lambda bb, h: jnp.maximum(jnp.max(log_d[bb, h], axis=-1, keepdims=True), inter[bb, h]))
    d_w = each(lambda bb, h: jnp.exp(log_d[bb, h] - m_t[bb, h]))
    w_inter = each(lambda bb, h: jnp.exp(inter[bb, h] - m_t[bb, h]))

    kh = each(lambda bb, h: qk[bb][:, BRANCH_WIDTH + h * M_DK:BRANCH_WIDTH + (h + 1) * M_DK] * (M_DK ** -0.5))
    qb = each(lambda bb, h: qk[bb][:, head(h)].astype(BF16))
    ones = jnp.ones((L, M_DK), BF16)
    v_aug = each(lambda bb, h: jnp.concatenate([vo_ref[bb, :, head(h)], ones], axis=1))
    sc = each(lambda bb, h: (_dot_nt(qb[bb, h], kh[bb, h].astype(BF16)) * d_w[bb, h]).astype(BF16))

    intra = each(lambda bb, h: _dot(sc[bb, h], v_aug[bb, h]))
    carried = each(lambda bb, h: _dot(qb[bb, h], c_sc[bb, h].astype(BF16)))
    both = each(lambda bb, h: intra[bb, h] + w_inter[bb, h] * carried[bb, h])
    floor = each(lambda bb, h: jnp.exp(-m_t[bb, h]))
    hh = each(lambda bb, h: both[bb, h][:, :M_DK] / jnp.maximum(jnp.abs(both[bb, h][:, M_DK:]), floor[bb, h]))
    ms = each(lambda bb, h: jnp.mean(hh[bb, h] * hh[bb, h], axis=-1, keepdims=True))
    for bb, h in items:
        hn = hh[bb, h] * lax.rsqrt(ms[bb, h] + NORM_EPS) * hg[:, head(h)]
        o_pre = vo_ref[bb, :, BRANCH_WIDTH + h * M_DK:BRANCH_WIDTH + (h + 1) * M_DK].astype(F32)
        out_ref[bb, :, head(h)] = (hn * _sigmoid(o_pre)).astype(BF16)

    b_last = each(lambda bb, h: b_col[bb, h][L - 1:L, :])
    log_w = each(lambda bb, h: b_last[bb, h] - b_col[bb, h] + i_col[bb, h])
    m_new = each(lambda bb, h: jnp.maximum(b_last[bb, h] + m_prev[bb, h],
                                           jnp.max(log_w[bb, h], axis=0, keepdims=True)))
    decay = each(lambda bb, h: jnp.exp(b_last[bb, h] + m_prev[bb, h] - m_new[bb, h]))
    kw = each(lambda bb, h: (kh[bb, h] * jnp.exp(log_w[bb, h] - m_new[bb, h])).astype(BF16))
    upd = each(lambda bb, h: _dot_tn(kw[bb, h], v_aug[bb, h]))
    for bb, h in items:
        c_sc[bb, h] = decay[bb, h] * c_sc[bb, h] + upd[bb, h]
        m_sc[bb, h:h + 1, :] = jnp.broadcast_to(m_new[bb, h], (1, LANES))


def _mlstm(proj3, gates3, layer, conv_w, conv_b, gate_bias, head_gain):
    b, s, _ = proj3.shape
    L = M_CHUNK
    return pl.pallas_call(
        _mlstm_kernel,
        grid=(b // ML_NB, s // L),
        in_specs=[
            pl.BlockSpec((ML_NB, L, 1024), lambda i, c: (i, c, COL_MQK // 1024)),
            pl.BlockSpec((ML_NB, L, 1024), lambda i, c: (i, c, COL_MV // 1024)),
            pl.BlockSpec((ML_NB, L, LANES), lambda i, c: (i, c, 0)),
            _layer_spec(layer, (M_CONV, 1024)),
            _layer_spec(layer, (1, 1024)),
            _layer_spec(layer, (1, LANES)),
            _layer_spec(layer, (1, BRANCH_WIDTH)),
        ],
        out_specs=pl.BlockSpec((ML_NB, L, BRANCH_WIDTH), lambda i, c: (i, c, 0)),
        out_shape=jax.ShapeDtypeStruct((b, s, BRANCH_WIDTH), BF16),
        scratch_shapes=[
            pltpu.VMEM((ML_NB, M_HEADS, M_DK, 2 * M_DK), F32),
            pltpu.VMEM((ML_NB, 8, LANES), F32),
            pltpu.VMEM((ML_NB, 8, 1024), F32),
        ],
        compiler_params=pltpu.CompilerParams(
            dimension_semantics=("parallel", "arbitrary"), vmem_limit_bytes=VMEM_LIMIT),
        name="mlstm",
    )(proj3, proj3, gates3, conv_w, conv_b, gate_bias, head_gain)


AT_T = 256
ALIBI_BASE = 256
AT_TK = 512


def _halfnorm(x, gain):
    lane = lax.broadcasted_iota(I32, (1, LANES), 1)
    lo = lane < A_DQK
    x2 = x * x
    ms_lo = jnp.sum(jnp.where(lo, x2, 0.0), axis=-1, keepdims=True)
    ms_hi = jnp.sum(jnp.where(lo, 0.0, x2), axis=-1, keepdims=True)
    ms = jnp.where(lo, ms_lo, ms_hi) * (1.0 / A_DQK)
    return x * lax.rsqrt(ms + NORM_EPS) * gain


AV_ROWS = A_DV + 16


AT_NH = 2


def _attn_kernel(slopes_ref, q_ref, k_ref, v_ref, qg_ref, kg_ref, lam_ref, hg_ref, out_ref,
                 k1_sc, k2_sc, vt_sc, s_sc, p_sc, acc_sc, *, lam_init):
    T = AT_T
    TK = AT_TK
    CH = 128
    heads = range(AT_NH)
    qi = pl.program_id(2)
    lane = lax.broadcasted_iota(I32, (1, LANES), 1)
    slope = [slopes_ref[pl.program_id(1) * AT_NH + hd] for hd in heads]

    def cols(hd):
        return slice(hd * LANES, (hd + 1) * LANES)

    @pl.when(qi == 0)
    def _():
        s_len = k_ref.shape[0]
        pos = lax.broadcasted_iota(I32, (s_len, 1), 0)
        for hd in heads:
            kn = _halfnorm(k_ref[:, cols(hd)].astype(F32), kg_ref[...])
            c_loc = (pos & (ALIBI_BASE - 1)).astype(F32) * slope[hd]
            c_blk = (pos >> int(math.log2(ALIBI_BASE))).astype(F32) * (slope[hd] * ALIBI_BASE)
            k1_sc[hd] = jnp.where(lane < A_DQK, kn, jnp.where(lane == A_DQK, c_loc, jnp.where(
                lane == A_DQK + 1, c_blk, 0.0))).astype(BF16)
            k2_sc[hd] = jnp.where(lane >= A_DQK, kn, jnp.where(lane == 0, c_loc, jnp.where(
                lane == 1, c_blk, 0.0))).astype(BF16)
            vt_sc[hd, :A_DV, :] = v_ref[:, cols(hd)].astype(F32).T.astype(BF16)
            vt_sc[hd, A_DV:, :] = jnp.ones((AV_ROWS - A_DV, s_len), BF16)

    lp = lam_ref[...]
    lam = (jnp.exp(jnp.sum(lp[0:1, :] * lp[1:2, :], axis=-1, keepdims=True))
           - jnp.exp(jnp.sum(lp[2:3, :] * lp[3:4, :], axis=-1, keepdims=True)) + lam_init)

    qn = [_halfnorm(q_ref[:, cols(hd)].astype(F32), qg_ref[...]) * (A_DQK ** -0.5) for hd in heads]
    q1 = [jnp.where(lane < A_DQK, qn[hd], jnp.where(lane < A_DQK + 2, 1.0, 0.0)).astype(BF16) for hd in heads]
    q2 = [jnp.where(lane >= A_DQK, qn[hd], jnp.where(lane < 2, 1.0, 0.0)).astype(BF16) for hd in heads]

    def scores_to_scratch(hd, j):
        start = pl.multiple_of(j * TK, TK)
        s_sc[hd, 0] = _dot_nt(k1_sc[hd, pl.ds(start, TK), :], q1[hd])
        s_sc[hd, 1] = _dot_nt(k2_sc[hd, pl.ds(start, TK), :], q2[hd])

    def softmax_to_scratch(hd, j, m, masked):
        m_out, alpha_out = [], []
        for mp in range(2):
            for hh in range(T // LANES):
                idx = mp * (T // LANES) + hh
                lanes = slice(hh * LANES, (hh + 1) * LANES)

                def chunk(c):
                    sc = s_sc[hd, mp, c * CH:(c + 1) * CH, lanes]
                    if masked:
                        key = j * TK + c * CH + lax.broadcasted_iota(I32, (CH, LANES), 0)
                        qry = qi * T + hh * LANES + lax.broadcasted_iota(I32, (CH, LANES), 1)
                        sc = jnp.where(key <= qry, sc, NEG_BIG)
                    return sc

                m_new = m[idx]
                for c in range(TK // CH):
                    m_new = jnp.maximum(m_new, jnp.max(chunk(c), axis=0, keepdims=True))
                for c in range(TK // CH):
                    p_sc[hd, c * CH:(c + 1) * CH, idx * LANES:(idx + 1) * LANES] = (
                        jnp.exp(chunk(c) - m_new).astype(BF16))
                alpha_out.append(jnp.exp(m[idx] - m_new))
                m_out.append(m_new)
        return tuple(m_out), tuple(alpha_out)

    def values_from_scratch(hd, j, alpha):
        vt = vt_sc[hd, :, pl.ds(pl.multiple_of(j * TK, TK), TK)]
        pv = _dot(vt, p_sc[hd])
        for idx in range(2 * T // LANES):
            sl = slice(idx * LANES, (idx + 1) * LANES)
            acc_sc[hd, :, sl] = alpha[idx] * acc_sc[hd, :, sl] + pv[:, sl]

    def body(j, carry):
        m, alpha = carry
        for hd in heads:
            values_from_scratch(hd, jnp.maximum(j - 1, 0), alpha[hd])
        stats = [softmax_to_scratch(hd, j, m[hd], False) for hd in heads]
        for hd in heads:
            scores_to_scratch(hd, j + 1)
        return tuple(st[0] for st in stats), tuple(st[1] for st in stats)

    n_slab = 2 * T // LANES
    last = (qi * T) >> int(math.log2(TK))
    p_sc[...] = jnp.zeros_like(p_sc)
    acc_sc[...] = jnp.zeros_like(acc_sc)
    for hd in heads:
        scores_to_scratch(hd, 0)
    m, alpha = lax.fori_loop(0, last, body, (((jnp.full((1, LANES), NEG_BIG, F32),) * n_slab,) * AT_NH,
                                             ((jnp.ones((1, LANES), F32),) * n_slab,) * AT_NH))
    for hd in heads:
        values_from_scratch(hd, jnp.maximum(last - 1, 0), alpha[hd])
    alpha = [softmax_to_scratch(hd, last, m[hd], True)[1] for hd in heads]
    for hd in heads:
        values_from_scratch(hd, last, alpha[hd])

    hg = hg_ref[...]
    for hd in heads:
        a1 = acc_sc[hd, :, :T]
        a2 = acc_sc[hd, :, T:]
        o_t = a1[:A_DV, :] / a1[A_DV:A_DV + 1, :] - lam * (a2[:A_DV, :] / a2[A_DV:A_DV + 1, :])
        out_ref[:, cols(hd)] = (_rms(o_t.T, hg[:, cols(hd)]) * (1.0 - lam_init)).astype(BF16)


def _attention(proj3, layer, q_gain, k_gain, lam_params, head_gain, lam_init):
    b, s, _ = proj3.shape
    T = AT_T
    W = AT_NH * LANES
    slopes = jnp.asarray(ALIBI_SLOPES, F32)
    return pl.pallas_call(
        functools.partial(_attn_kernel, lam_init=lam_init),
        grid=(b, A_HEADS // AT_NH, s // T),
        in_specs=[
            pl.BlockSpec(memory_space=pltpu.SMEM),
            pl.BlockSpec((None, T, W), lambda i, h, q: (i, q, COL_AQ // W + h)),
            pl.BlockSpec((None, s, W), lambda i, h, q: (i, 0, COL_AK // W + h)),
            pl.BlockSpec((None, s, W), lambda i, h, q: (i, 0, COL_AV // W + h)),
            _layer_spec(layer, (1, LANES)),
            _layer_spec(layer, (1, LANES)),
            _layer_spec(layer, (4, A_DQK)),
            pl.BlockSpec((None, 1, W), lambda i, h, q: (layer, 0, h)),
        ],
        out_specs=pl.BlockSpec((None, T, W), lambda i, h, q: (i, q, h)),
        out_shape=jax.ShapeDtypeStruct((b, s, BRANCH_WIDTH), BF16),
        scratch_shapes=[pltpu.VMEM((AT_NH, s, LANES), BF16), pltpu.VMEM((AT_NH, s, LANES), BF16),
                        pltpu.VMEM((AT_NH, AV_ROWS, s), BF16), pltpu.VMEM((AT_NH, 2, AT_TK, T), F32),
                        pltpu.VMEM((AT_NH, AT_TK, 2 * T), BF16), pltpu.VMEM((AT_NH, AV_ROWS, 2 * T), F32)],
        compiler_params=pltpu.CompilerParams(
            dimension_semantics=("parallel", "parallel", "arbitrary"), vmem_limit_bytes=VMEM_LIMIT),
        name="diff_attn",
    )(slopes, proj3, proj3, proj3, q_gain, k_gain, lam_params, head_gain)


MG_TM = 512
MG_HALO = 128
ROUTER_ROWS = 16
ROUTE_ROWS = 8


def _merge_kernel(x_ref, hm_ref, ha_ref, u_ref, halo_ref, gm_ref, ga_ref, gp_ref, pw_ref, ps_ref,
                  wb_ref, wo_ref, fg_ref, *rest, seq_tiles, routed):
    if routed:
        wr_ref, br_ref, xo_ref, hf_ref, route_ref = rest
    else:
        xo_ref, hf_ref = rest
    TM = MG_TM
    i = pl.program_id(0)
    seq_tile = i % seq_tiles

    u = u_ref[...]
    halo = jnp.where(seq_tile > 0, halo_ref[...], jnp.zeros_like(halo_ref))
    u_ext = jnp.concatenate([halo, u], axis=0)
    dist = (lax.broadcasted_iota(I32, (MG_HALO, 2 * MG_HALO), 0) + MG_HALO
            - lax.broadcasted_iota(I32, (MG_HALO, 2 * MG_HALO), 1))
    t_pos = seq_tile * TM + lax.broadcasted_iota(I32, (TM, 1), 0)
    ps = ps_ref[...]
    hp_parts = []
    for g, w in enumerate(P_WINDOWS):
        sl = slice(g * P_GC, (g + 1) * P_GC)
        band = jnp.where((dist >= 0) & (dist < w), 1.0, 0.0).astype(BF16)
        ug = u[:, sl]
        sums = jnp.concatenate([_dot(band, u_ext[r:r + 2 * MG_HALO, sl]) for r in range(0, TM, MG_HALO)],
                               axis=0)
        cnt = jnp.minimum(t_pos + 1, w).astype(F32)
        pooled = sums / cnt - ug.astype(F32)
        hp_parts.append((_dot(pooled.astype(BF16), pw_ref[g]) * ps[:, sl]).astype(BF16))
    hp = jnp.concatenate(hp_parts, axis=1)

    merged = (_sigmoid(gm_ref[...].astype(F32)) * _dot(hm_ref[...], wb_ref[0])
              + _sigmoid(ga_ref[...].astype(F32)) * _dot(ha_ref[...], wb_ref[1])
              + _sigmoid(gp_ref[...].astype(F32)) * _dot(hp, wb_ref[2]))
    x_new = x_ref[...] + _dot(merged.astype(BF16), wo_ref[...])
    xo_ref[...] = x_new
    hf = _rms(x_new, fg_ref[...])
    hf_ref[...] = hf.astype(BF16)

    if routed:
        hf_hi = hf.astype(BF16)
        hf_lo = (hf - hf_hi.astype(F32)).astype(BF16)
        wr = wr_ref[...]
        wr_hi = wr.astype(BF16)
        wr_lo = (wr - wr_hi.astype(F32)).astype(BF16)
        logits = (_dot_nt(wr_hi, hf_hi) + _dot_nt(wr_lo, hf_hi) + _dot_nt(wr_hi, hf_lo))[:N_EXPERTS, :]
        logits = logits + br_ref[...]
        expert = lax.broadcasted_iota(I32, (N_EXPERTS, TM), 0).astype(F32)
        m1 = jnp.max(logits, axis=0, keepdims=True)
        i1 = jnp.min(jnp.where(logits == m1, expert, float(N_EXPERTS)), axis=0, keepdims=True)
        rest_l = jnp.where(expert == i1, NEG_BIG, logits)
        m2 = jnp.max(rest_l, axis=0, keepdims=True)
        i2 = jnp.min(jnp.where(rest_l == m2, expert, float(N_EXPERTS)), axis=0, keepdims=True)
        g1 = 1.0 / (1.0 + jnp.exp(m2 - m1))
        route_ref[...] = jnp.concatenate(
            [g1, 1.0 - g1, i1, i2, jnp.zeros((ROUTE_ROWS - 4, TM), F32)], axis=0)


def _merge(x2, hm2, ha2, proj2, layer, pool_w, pool_scale, w_branch, w_out, ffn_gain, seq, router=None):
    tok = x2.shape[0]
    TM = MG_TM
    routed = router is not None
    hb = TM // MG_HALO
    in_specs = [
        pl.BlockSpec((TM, D_MODEL), lambda i: (i, 0)),
        pl.BlockSpec((TM, BRANCH_WIDTH), lambda i: (i, 0)),
        pl.BlockSpec((TM, BRANCH_WIDTH), lambda i: (i, 0)),
        pl.BlockSpec((TM, BRANCH_WIDTH), lambda i: (i, COL_PU // BRANCH_WIDTH)),
        pl.BlockSpec((MG_HALO, BRANCH_WIDTH), lambda i: (jnp.maximum(i * hb - 1, 0), COL_PU // BRANCH_WIDTH)),
        pl.BlockSpec((TM, D_MODEL), lambda i: (i, COL_G // D_MODEL)),
        pl.BlockSpec((TM, D_MODEL), lambda i: (i, COL_G // D_MODEL + 1)),
        pl.BlockSpec((TM, D_MODEL), lambda i: (i, COL_G // D_MODEL + 2)),
        _layer_spec(layer, (len(P_WINDOWS), P_GC, P_GC)),
        _layer_spec(layer, (1, BRANCH_WIDTH)),
        _layer_spec(layer, (3, BRANCH_WIDTH, D_MODEL)),
        _layer_spec(layer, (D_MODEL, D_MODEL)),
        _layer_spec(layer, (1, D_MODEL)),
    ]
    args = [x2, hm2, ha2, proj2, proj2, proj2, proj2, proj2, pool_w, pool_scale, w_branch, w_out, ffn_gain]
    out_specs = [pl.BlockSpec((TM, D_MODEL), lambda i: (i, 0)),
                 pl.BlockSpec((TM, D_MODEL), lambda i: (i, 0))]
    out_shape = [jax.ShapeDtypeStruct((tok, D_MODEL), F32), jax.ShapeDtypeStruct((tok, D_MODEL), BF16)]
    if routed:
        moe_layer, router_w, router_b = router
        in_specs += [_layer_spec(moe_layer, (ROUTER_ROWS, D_MODEL)), _layer_spec(moe_layer, (N_EXPERTS, 1))]
        args += [router_w, router_b]
        out_specs.append(pl.BlockSpec((ROUTE_ROWS, TM), lambda i: (0, i)))
        out_shape.append(jax.ShapeDtypeStruct((ROUTE_ROWS, tok), F32))
    return pl.pallas_call(
        functools.partial(_merge_kernel, seq_tiles=seq // TM, routed=routed),
        grid=(tok // TM,),
        in_specs=in_specs,
        out_specs=out_specs,
        out_shape=out_shape,
        compiler_params=pltpu.CompilerParams(
            dimension_semantics=("parallel",), vmem_limit_bytes=VMEM_LIMIT),
        name="merge_routed" if routed else "merge",
    )(*args)


def _ple_epilogue(x_new, p, pg_gain, wpg, wpp):
    gate = _sigmoid(_dot(_rms(x_new, pg_gain).astype(BF16), wpg))
    return x_new + gate * _dot(p.astype(BF16), wpp)


FF_TM = 512


def _ffn_kernel(hf_ref, x_ref, p_ref, wgu_ref, wd_ref, pgn_ref, wpg_ref, wpp_ref, out_ref):
    hf = hf_ref[...]
    g = _dot(hf, wgu_ref[:, :D_FF])
    u = _dot(hf, wgu_ref[:, D_FF:])
    y = _dot((g * _sigmoid(g) * u).astype(BF16), wd_ref[...])
    out_ref[...] = _ple_epilogue(x_ref[...] + y, p_ref[...], pgn_ref[...], wpg_ref[...], wpp_ref[...])


def _ffn(hf2, x2, p2, p_row0, ffn_layer, w_gu, w_down, layer, ple_gain, wpg, wpp):
    tok = x2.shape[0]
    TM = FF_TM
    once = pl.Buffered(1)
    return pl.pallas_call(
        _ffn_kernel,
        grid=(tok // TM,),
        in_specs=[
            pl.BlockSpec((TM, D_MODEL), lambda i: (i, 0)),
            pl.BlockSpec((TM, D_MODEL), lambda i: (i, 0)),
            pl.BlockSpec((TM, PLE_DIM), lambda i: (i + p_row0 // TM, 0)),
            _layer_spec(ffn_layer, (D_MODEL, 2 * D_FF), pipeline_mode=once),
            _layer_spec(ffn_layer, (D_FF, D_MODEL), pipeline_mode=once),
            _layer_spec(layer, (1, D_MODEL)),
            _layer_spec(layer, (D_MODEL, D_MODEL), pipeline_mode=once),
            _layer_spec(layer, (PLE_DIM, D_MODEL), pipeline_mode=once),
        ],
        out_specs=pl.BlockSpec((TM, D_MODEL), lambda i: (i, 0)),
        out_shape=jax.ShapeDtypeStruct((tok, D_MODEL), F32),
        compiler_params=pltpu.CompilerParams(
            dimension_semantics=("parallel",), vmem_limit_bytes=VMEM_LIMIT),
        name="ffn_dense",
    )(hf2, x2, p2, w_gu, w_down, ple_gain, wpg, wpp)


MOE_TM = 1024
MOE_TF = 512
DP_TM = 256
DP_CH = 256
CB_TB = 256
CB_ALIGN = 16
CB_W = CB_TB + CB_ALIGN


def _dispatch_kernel(clo_ref, chi_ref, pos_ref, gate_ref, hf_ref, x_ref, g_ref, acc_sc, gacc_sc):
    i = pl.program_id(0)
    rows = i * DP_TM + lax.broadcasted_iota(I32, (DP_TM, 1), 0)
    acc_sc[...] = jnp.zeros_like(acc_sc)
    gacc_sc[...] = jnp.zeros_like(gacc_sc)

    lo = clo_ref[i]
    hi = chi_ref[i]

    def chunk(c, live):
        start = pl.multiple_of(c * DP_CH, DP_CH)
        pos = pos_ref[:, pl.ds(start, DP_CH)]
        gate = gate_ref[:, pl.ds(start, DP_CH)]
        want = jnp.where(live, rows, -1)
        hit1 = pos[0:1, :] == want
        hit2 = pos[1:2, :] == want
        onehot = jnp.where(hit1, 1.0, jnp.where(hit2, 1.0, 0.0)).astype(BF16)
        gates = jnp.sum(jnp.where(hit1, gate[0:1, :], jnp.where(hit2, gate[1:2, :], 0.0)),
                        axis=-1, keepdims=True)
        return _dot(onehot, hf_ref[pl.ds(start, DP_CH), :]), gates

    def pair(t, carry):
        c = lo + 2 * t
        x0, g0 = chunk(c, True)
        x1, g1 = chunk(jnp.minimum(c + 1, hi), c + 1 <= hi)
        acc_sc[...] += x0 + x1
        gacc_sc[...] += g0 + g1
        return carry

    lax.fori_loop(0, (hi - lo + 2) >> 1, pair, 0)
    x_ref[...] = acc_sc[...].astype(BF16)
    g_ref[...] = gacc_sc[...]


def _dispatch(hf2, pos_t, gate_t, chunk_lo, chunk_hi):
    tok = hf2.shape[0]
    ns = chunk_lo.shape[0]
    grid_spec = pltpu.PrefetchScalarGridSpec(
        num_scalar_prefetch=2,
        grid=(ns,),
        in_specs=[
            pl.BlockSpec((2, tok), lambda i, lo, hi: (0, 0)),
            pl.BlockSpec((2, tok), lambda i, lo, hi: (0, 0)),
            pl.BlockSpec((tok, D_MODEL), lambda i, lo, hi: (0, 0), pipeline_mode=pl.Buffered(1)),
        ],
        out_specs=[pl.BlockSpec((DP_TM, D_MODEL), lambda i, lo, hi: (i, 0)),
                   pl.BlockSpec((DP_TM, 1), lambda i, lo, hi: (i, 0))],
        scratch_shapes=[pltpu.VMEM((DP_TM, D_MODEL), F32), pltpu.VMEM((DP_TM, 1), F32)],
    )
    return pl.pallas_call(
        _dispatch_kernel,
        grid_spec=grid_spec,
        out_shape=[jax.ShapeDtypeStruct((ns * DP_TM, D_MODEL), BF16),
                   jax.ShapeDtypeStruct((ns * DP_TM, 1), F32)],
        compiler_params=pltpu.CompilerParams(
            dimension_semantics=("arbitrary",), vmem_limit_bytes=VMEM_LIMIT),
        name="moe_dispatch",
    )(chunk_lo, chunk_hi, pos_t, gate_t, hf2)


def _moe_kernel(te_ref, tr_ref, x_ref, gate_ref, wg_ref, wu_ref, wd_ref, y_ref, acc_sc):
    i = pl.program_id(0)
    j = pl.program_id(1)
    last = pl.num_programs(1) - 1
    rows = tr_ref[i]
    half = MOE_TM // 2

    def step(n):
        @pl.when(j == 0)
        def _():
            acc_sc[:n] = jnp.zeros((n, D_MODEL), F32)

        xb = x_ref[:n]
        g = _dot(xb, wg_ref[...].astype(BF16))
        u = _dot(xb, wu_ref[...].astype(BF16))
        acc_sc[:n] += _dot((g * _sigmoid(g) * u).astype(BF16), wd_ref[...].astype(BF16))

        @pl.when(j == last)
        def _():
            y_ref[:n] = (acc_sc[:n] * gate_ref[:n]).astype(BF16)
            if n < MOE_TM:
                y_ref[n:] = jnp.zeros((MOE_TM - n, D_MODEL), BF16)

    @pl.when(rows > half)
    def _():
        step(MOE_TM)

    @pl.when((rows > 0) & (rows <= half))
    def _():
        step(half)

    @pl.when((rows == 0) & (j == last))
    def _():
        y_ref[...] = jnp.zeros_like(y_ref)


def _moe(x_sorted, gate_sorted, w_gu, w_down, tile_expert, tile_rows):
    TM, TF = MOE_TM, MOE_TF
    nt = tile_expert.shape[0]
    nf = D_FF_EXPERT // TF

    def col(j, tv, i):
        return jnp.where(tv[i] > 0, j, nf - 1)

    grid_spec = pltpu.PrefetchScalarGridSpec(
        num_scalar_prefetch=2,
        grid=(nt, nf),
        in_specs=[
            pl.BlockSpec((TM, D_MODEL), lambda i, j, te, tv: (i, 0)),
            pl.BlockSpec((TM, 1), lambda i, j, te, tv: (i, 0)),
            pl.BlockSpec((None, D_MODEL, TF), lambda i, j, te, tv: (te[i], 0, col(j, tv, i))),
            pl.BlockSpec((None, D_MODEL, TF), lambda i, j, te, tv: (te[i], 0, col(j, tv, i) + nf)),
            pl.BlockSpec((None, TF, D_MODEL), lambda i, j, te, tv: (te[i], col(j, tv, i), 0)),
        ],
        out_specs=pl.BlockSpec((TM, D_MODEL), lambda i, j, te, tv: (i, 0)),
        scratch_shapes=[pltpu.VMEM((TM, D_MODEL), F32)],
    )
    return pl.pallas_call(
        _moe_kernel,
        grid_spec=grid_spec,
        out_shape=jax.ShapeDtypeStruct((nt * TM, D_MODEL), BF16),
        compiler_params=pltpu.CompilerParams(
            dimension_semantics=("arbitrary", "arbitrary"), vmem_limit_bytes=VMEM_LIMIT),
        name="moe_experts",
    )(tile_expert, tile_rows, x_sorted, gate_sorted, w_gu, w_gu, w_down)


def _combine_kernel(w0_ref, rng_ref, x_ref, p_ref, pos_ref, y_hbm, pgn_ref, wpg_ref, wpp_ref, out_ref,
                    win_sc, acc_sc, sem):
    i = pl.program_id(0)
    slot = i & 1

    def window_copy(step, buf, e):
        start = pl.multiple_of(w0_ref[step * N_EXPERTS + e], CB_ALIGN)
        return pltpu.make_async_copy(y_hbm.at[pl.ds(start, CB_W), :], win_sc.at[buf, e], sem.at[buf, e])

    @pl.when(i == 0)
    def _():
        for e in range(N_EXPERTS):
            window_copy(0, 0, e).start()

    @pl.when(i + 1 < pl.num_programs(0))
    def _():
        for e in range(N_EXPERTS):
            window_copy(i + 1, 1 - slot, e).start()

    pos = pos_ref[...]
    pos1 = pos[:, 0:1]
    pos2 = pos[:, 1:2]

    n_win = pl.num_programs(0) * N_EXPERTS

    def row_begin(e):
        return rng_ref[i * N_EXPERTS + e]

    def row_end(e):
        return rng_ref[n_win + i * N_EXPERTS + e]

    def onehot(e, first, width):
        rows = lax.broadcasted_iota(I32, (1, width), 1) + (first + w0_ref[i * N_EXPERTS + e])
        rows = jnp.where((rows >= row_begin(e)) & (rows < row_end(e)), rows, -1)
        return jnp.where(pos1 == rows, 1.0, jnp.where(pos2 == rows, 1.0, 0.0)).astype(BF16)

    moe_out = jnp.zeros((CB_TB, D_MODEL), F32)
    for e in range(N_EXPERTS):
        window_copy(i, slot, e).wait()
        moe_out = moe_out + _dot(onehot(e, 0, CB_TB), win_sc[slot, e, :CB_TB, :])
    acc_sc[...] = moe_out

    for e in range(N_EXPERTS):
        @pl.when(row_end(e) - w0_ref[i * N_EXPERTS + e] > CB_TB)
        def _():
            acc_sc[...] += _dot(onehot(e, CB_TB, CB_ALIGN), win_sc[slot, e, CB_TB:, :])

    out_ref[...] = _ple_epilogue(x_ref[...] + acc_sc[...], p_ref[...], pgn_ref[...], wpg_ref[...],
                                 wpp_ref[...])


def _combine(x2, p2, p_row0, pos, y_sorted, win_start, win_rows, layer, ple_gain, wpg, wpp):
    tok = x2.shape[0]
    TB = CB_TB
    grid_spec = pltpu.PrefetchScalarGridSpec(
        num_scalar_prefetch=2,
        grid=(tok // TB,),
        in_specs=[
            pl.BlockSpec((TB, D_MODEL), lambda i, w0, wt: (i, 0)),
            pl.BlockSpec((TB, PLE_DIM), lambda i, w0, wt: (i + p_row0 // TB, 0)),
            pl.BlockSpec((TB, 2), lambda i, w0, wt: (i, 0)),
            pl.BlockSpec(memory_space=pl.ANY),
            _layer_spec(layer, (1, D_MODEL)),
            _layer_spec(layer, (D_MODEL, D_MODEL)),
            _layer_spec(layer, (PLE_DIM, D_MODEL)),
        ],
        out_specs=pl.BlockSpec((TB, D_MODEL), lambda i, w0, wt: (i, 0)),
        scratch_shapes=[pltpu.VMEM((2, N_EXPERTS, CB_W, D_MODEL), BF16), pltpu.VMEM((TB, D_MODEL), F32),
                        pltpu.SemaphoreType.DMA((2, N_EXPERTS))],
    )
    return pl.pallas_call(
        _combine_kernel,
        grid_spec=grid_spec,
        out_shape=jax.ShapeDtypeStruct((tok, D_MODEL), F32),
        compiler_params=pltpu.CompilerParams(
            dimension_semantics=("arbitrary",), vmem_limit_bytes=VMEM_LIMIT),
        name="moe_combine",
    )(win_start, win_rows, x2, p2, pos, y_sorted, ple_gain, wpg, wpp)


def _route_metadata(route, tok):
    TM = MOE_TM
    nt = (2 * tok) // TM + N_EXPERTS
    rows = nt * TM
    g1, g2 = route[0], route[1]
    i1, i2 = route[2].astype(I32), route[3].astype(I32)
    experts = jnp.arange(N_EXPERTS, dtype=I32)
    member = ((i1[:, None] == experts) | (i2[:, None] == experts)).astype(I32)
    rank = jnp.cumsum(member, axis=0) - member
    counts = jnp.sum(member, axis=0)
    tiles_e = (counts + TM - 1) // TM
    tile_end = jnp.cumsum(tiles_e)
    tile_start = tile_end - tiles_e
    seg_start = tile_start * TM
    sel1 = i1[:, None] == experts
    sel2 = i2[:, None] == experts
    pos1 = jnp.sum(jnp.where(sel1, seg_start[None, :] + rank, 0), axis=1).astype(I32)
    pos2 = jnp.sum(jnp.where(sel2, seg_start[None, :] + rank, 0), axis=1).astype(I32)

    tiles = jnp.arange(nt, dtype=I32)
    n_used = tile_end[-1]
    tile_rows = (tiles < n_used).astype(I32)
    te = jnp.sum((tiles[:, None] >= tile_end[None, :]).astype(I32), axis=1)
    te_last = jnp.sum(((n_used - 1) >= tile_end).astype(I32))
    tile_expert = jnp.where(tile_rows > 0, jnp.minimum(te, N_EXPERTS - 1), te_last).astype(I32)
    mine = tile_expert[:, None] == experts[None, :]
    rows_left = (jnp.sum(jnp.where(mine, counts[None, :], 0), axis=1)
                 - (tiles - jnp.sum(jnp.where(mine, tile_start[None, :], 0), axis=1)) * TM)
    tile_rows = jnp.where(tile_rows > 0, jnp.clip(rows_left, 0, TM), 0).astype(I32)

    steps = jnp.arange(rows // DP_TM, dtype=I32)
    step_tile = steps // (TM // DP_TM)
    step_valid = step_tile < n_used
    step_expert = jnp.minimum(jnp.sum((step_tile[:, None] >= tile_end[None, :]).astype(I32), axis=1),
                              N_EXPERTS - 1)
    pick = step_expert[:, None] == experts[None, :]
    rank_lo = steps * DP_TM - jnp.sum(jnp.where(pick, seg_start[None, :], 0), axis=1)
    rank_hi = jnp.minimum(rank_lo + DP_TM, jnp.sum(jnp.where(pick, counts[None, :], 0), axis=1)) - 1
    before = jnp.sum(jnp.where(pick[:, :, None], rank[::DP_CH, :].T[None, :, :], 0), axis=1)
    chunk_lo = jnp.sum((before <= rank_lo[:, None]).astype(I32), axis=1) - 1
    chunk_hi = jnp.sum((before <= rank_hi[:, None]).astype(I32), axis=1) - 1
    live = step_valid & (rank_hi >= rank_lo)
    chunk_lo = jnp.where(live, chunk_lo, 0).astype(I32)
    chunk_hi = jnp.where(live, chunk_hi, -1).astype(I32)

    blk_start = seg_start[None, :] + rank[::CB_TB, :]
    win_start = jnp.minimum((blk_start // CB_ALIGN) * CB_ALIGN, rows - CB_W).astype(I32)
    blk_count = jnp.concatenate([rank[CB_TB::CB_TB, :], counts[None, :]], axis=0) - rank[::CB_TB, :]
    win_rows = jnp.concatenate([blk_start.reshape(-1), (blk_start + blk_count).reshape(-1)]).astype(I32)
    win_start = win_start.reshape(-1)
    pos = jnp.stack([pos1, pos2], axis=1)
    return (tile_expert, tile_rows, chunk_lo, chunk_hi, jnp.stack([pos1, pos2]), jnp.stack([g1, g2]),
            pos, win_start, win_rows)


def _pack_w_in(w):
    o = [0, 1024, 1536, 2048, 2052, 2056, 2568, 3080, 3592, 4104, 7176]
    main = jnp.concatenate([w[..., o[0]:o[3]], w[..., o[5]:o[10]]], axis=-1).astype(BF16)
    gates = jnp.pad(w[..., o[3]:o[5]], ((0, 0), (0, 0), (0, LANES - 2 * M_HEADS))).astype(BF16)
    return main, gates


def kernel(x, p, attn_norm, w_in, m_conv_w, m_conv_b, m_gate_bias, m_head_norm, a_q_norm, a_k_norm,
           a_lambda, a_head_norm, pool_w, pool_scale, w_branch, w_out, ffn_norm, dense_w_gu, dense_w_down,
           router_w, router_b, moe_w_gu, moe_w_down, ple_norm, ple_w_gate, ple_w_proj):
    b, s, d = x.shape
    depth = w_in.shape[0]
    tok = b * s
    x2 = x.reshape(tok, d)

    def rows(a):
        return a.reshape(a.shape[0], 1, a.shape[1])

    w_main, w_if = _pack_w_in(w_in)
    gate_bias = rows(jnp.pad(m_gate_bias, ((0, 0), (0, LANES - 2 * M_HEADS))))
    q_gain = rows(jnp.tile(a_q_norm, (1, 2)))
    k_gain = rows(jnp.tile(a_k_norm, (1, 2)))
    p2 = p.reshape(depth * tok, PLE_DIM)
    pool_wb, w_branch_b, w_out_b = pool_w.astype(BF16), w_branch.astype(BF16), w_out.astype(BF16)
    ple = (rows(ple_norm), ple_w_gate.astype(BF16), ple_w_proj.astype(BF16))
    dense_gu, dense_down = dense_w_gu.astype(BF16), dense_w_down.astype(BF16)
    router_wt = jnp.pad(jnp.swapaxes(router_w, 1, 2), ((0, 0), (0, ROUTER_ROWS - N_EXPERTS), (0, 0)))
    router_bc = router_b.reshape(router_b.shape[0], N_EXPERTS, 1)

    for layer in range(depth):
        proj, gates = _inproj(x2, layer, rows(attn_norm), w_main, w_if)
        proj3 = proj.reshape(b, s, PROJ_WIDTH)
        h_m = _mlstm(proj3, gates.reshape(b, s, LANES), layer, m_conv_w, rows(m_conv_b), gate_bias,
                     rows(m_head_norm))
        lam_init = 0.8 - 0.6 * math.exp(-0.3 * layer)
        h_a = _attention(proj3, layer, q_gain, k_gain, a_lambda, rows(a_head_norm), lam_init)
        merge_args = (x2, h_m.reshape(tok, BRANCH_WIDTH), h_a.reshape(tok, BRANCH_WIDTH), proj, layer,
                      pool_wb, rows(pool_scale), w_branch_b, w_out_b, rows(ffn_norm), s)
        j = layer // 2
        if layer % 2 == 0:
            x_mid, hf = _merge(*merge_args)
            x2 = _ffn(hf, x_mid, p2, layer * tok, j, dense_gu, dense_down, layer, *ple)
        else:
            x_mid, hf, route = _merge(*merge_args, router=(j, router_wt, router_bc))
            (tile_expert, tile_rows, chunk_lo, chunk_hi, pos_t, gate_t, pos,
             win_start, win_rows) = _route_metadata(route, tok)
            x_sorted, gate_sorted = _dispatch(hf, pos_t, gate_t, chunk_lo, chunk_hi)
            y_sorted = _moe(x_sorted, gate_sorted, moe_w_gu[j], moe_w_down[j], tile_expert, tile_rows)
            x2 = _combine(x_mid, p2, layer * tok, pos, y_sorted, win_start, win_rows, layer, *ple)
    return x2.reshape(b, s, d)
```

```python
import functools
import math

import jax
import jax.numpy as jnp
from jax import lax
from jax.experimental import pallas as pl
from jax.experimental.pallas import tpu as pltpu

F32 = jnp.float32
BF16 = jnp.bfloat16
I32 = jnp.int32

D_MODEL = 1024
PLE_DIM = 256
NORM_EPS = 1e-6
BRANCH_WIDTH = 512

M_HEADS = 4
M_DK = 128
M_CONV = 4
M_CHUNK = 128

A_HEADS = 4
A_DV = 128
A_DQK = 64
ALIBI_SLOPES = tuple(2.0 ** (-8.0 * (h + 1) / A_HEADS) for h in range(A_HEADS))

P_WINDOWS = (2, 4, 8, 16)
P_GC = 128

D_FF = 2816
N_EXPERTS = 8
D_FF_EXPERT = 3584

PROJ_WIDTH = 7168
COL_MQK, COL_MV, COL_MO, COL_AQ, COL_AK, COL_AV, COL_PU, COL_G = 0, 1024, 1536, 2048, 2560, 3072, 3584, 4096

LANES = 128
NEG_BIG = -1e30

VMEM_LIMIT = 56 * 1024 * 1024


def _sigmoid(x):
    return 0.5 * jnp.tanh(0.5 * x) + 0.5


def _rms(x, gain):
    return x * lax.rsqrt(jnp.mean(x * x, axis=-1, keepdims=True) + NORM_EPS) * gain


def _dot(a, b):
    return jnp.dot(a, b, preferred_element_type=F32)


def _dot_nt(a, b):
    return lax.dot_general(a, b, (((1,), (1,)), ((), ())), preferred_element_type=F32)


def _dot_tn(a, b):
    return lax.dot_general(a, b, (((0,), (0,)), ((), ())), preferred_element_type=F32)


def _layer_spec(layer, shape, **kw):
    zeros = (0,) * len(shape)
    return pl.BlockSpec((None,) + tuple(shape), lambda *_: (layer,) + zeros, **kw)


IN_TM = 512
IN_TN = 1024


def _inproj_kernel(x_ref, gain_ref, w_ref, wif_ref, out_ref, gates_ref):
    hn = _rms(x_ref[...], gain_ref[...]).astype(BF16)
    gates_ref[...] = _dot(hn, wif_ref[...])
    for n in range(PROJ_WIDTH // IN_TN):
        cols = slice(n * IN_TN, (n + 1) * IN_TN)
        out_ref[:, cols] = _dot(hn, w_ref[:, cols]).astype(BF16)


def _inproj(x2, layer, gain, w_main, w_if):
    tok = x2.shape[0]
    return pl.pallas_call(
        _inproj_kernel,
        grid=(tok // IN_TM,),
        in_specs=[
            pl.BlockSpec((IN_TM, D_MODEL), lambda i: (i, 0)),
            _layer_spec(layer, (1, D_MODEL)),
            _layer_spec(layer, (D_MODEL, PROJ_WIDTH), pipeline_mode=pl.Buffered(1)),
            _layer_spec(layer, (D_MODEL, LANES)),
        ],
        out_specs=[
            pl.BlockSpec((IN_TM, PROJ_WIDTH), lambda i: (i, 0)),
            pl.BlockSpec((IN_TM, LANES), lambda i: (i, 0)),
        ],
        out_shape=[
            jax.ShapeDtypeStruct((tok, PROJ_WIDTH), BF16),
            jax.ShapeDtypeStruct((tok, LANES), F32),
        ],
        compiler_params=pltpu.CompilerParams(
            dimension_semantics=("parallel",), vmem_limit_bytes=VMEM_LIMIT),
        name="inproj",
    )(x2, gain, w_main, w_if)


CONV_TAIL = 16
ML_NB = 2


def _mlstm_kernel(qk_ref, vo_ref, g_ref, cw_ref, cb_ref, gb_ref, hg_ref, out_ref,
                  c_sc, m_sc, prev_sc):
    @pl.when(pl.program_id(1) == 0)
    def _():
        c_sc[...] = jnp.zeros_like(c_sc)
        m_sc[...] = jnp.zeros_like(m_sc)
        prev_sc[...] = jnp.zeros_like(prev_sc)

    L = M_CHUNK
    delay = (lax.broadcasted_iota(I32, (L, CONV_TAIL + L), 0) + CONV_TAIL
             - lax.broadcasted_iota(I32, (L, CONV_TAIL + L), 1))
    shift = {s: jnp.where(delay == s, 1.0, 0.0).astype(BF16) for s in range(1, M_CONV)}
    lane = lax.broadcasted_iota(I32, (L, LANES), 1)
    rowl = lax.broadcasted_iota(I32, (L, LANES), 0)
    causal = lax.broadcasted_iota(I32, (L, L), 0) >= lax.broadcasted_iota(I32, (L, L), 1)
    is_f = (lane >= M_HEADS) & (lane < 2 * M_HEADS)
    cw = cw_ref[...]
    hg = hg_ref[...]
    items = [(bb, h) for bb in range(ML_NB) for h in range(M_HEADS)]

    qk, gc, bcum, gc_t, bcum_t = [], [], [], [], []
    for bb in range(ML_NB):
        xb = qk_ref[bb]
        ext = jnp.concatenate([prev_sc[bb], xb], axis=0)
        y = cb_ref[...] + cw[M_CONV - 1:M_CONV, :] * xb.astype(F32)
        for s in range(1, M_CONV):
            y = y + cw[M_CONV - 1 - s:M_CONV - s, :] * _dot(shift[s], ext)
        prev_sc[bb] = xb[L - CONV_TAIL:]
        qk.append(y * _sigmoid(y))

        g = g_ref[bb] + gb_ref[...]
        logf = jnp.minimum(g, 0.0) - jnp.log(1.0 + jnp.exp(-jnp.abs(g)))
        gcb = jnp.where(is_f, logf, jnp.where(lane < M_HEADS, g, 0.0))
        acc = jnp.where(is_f, logf, 0.0)
        k = 1
        while k < L:
            acc = acc + jnp.where(rowl >= k, pltpu.roll(acc, k, 0), 0.0)
            k *= 2
        gc.append(gcb)
        bcum.append(acc)
        gc_t.append(gcb.T)
        bcum_t.append(acc.T)

    def each(fn):
        return {it: fn(*it) for it in items}

    def head(h):
        return slice(h * M_DK, (h + 1) * M_DK)

    b_col = each(lambda bb, h: bcum[bb][:, M_HEADS + h:M_HEADS + h + 1])
    i_col = each(lambda bb, h: gc[bb][:, h:h + 1])
    m_prev = each(lambda bb, h: m_sc[bb, h:h + 1, 0:1])
    log_d = each(lambda bb, h: jnp.where(
        causal, b_col[bb, h] - bcum_t[bb][M_HEADS + h:M_HEADS + h + 1, :] + gc_t[bb][h:h + 1, :], NEG_BIG))
    inter = each(lambda bb, h: b_col[bb, h] + m_prev[bb, h])
    m_t = each(lambda bb, h: jnp.maximum(jnp.max(log_d[bb, h], axis=-1, keepdims=True), inter[bb, h]))
    d_w = each(lambda bb, h: jnp.exp(log_d[bb, h] - m_t[bb, h]))
    w_inter = each(lambda bb, h: jnp.exp(inter[bb, h] - m_t[bb, h]))

    kh = each(lambda bb, h: qk[bb][:, BRANCH_WIDTH + h * M_DK:BRANCH_WIDTH + (h + 1) * M_DK] * (M_DK ** -0.5))
    qb = each(lambda bb, h: qk[bb][:, head(h)].astype(BF16))
    ones = jnp.ones((L, M_DK), BF16)
    v_aug = each(lambda bb, h: jnp.concatenate([vo_ref[bb, :, head(h)], ones], axis=1))
    sc = each(lambda bb, h: (_dot_nt(qb[bb, h], kh[bb, h].astype(BF16)) * d_w[bb, h]).astype(BF16))

    intra = each(lambda bb, h: _dot(sc[bb, h], v_aug[bb, h]))
    carried = each(lambda bb, h: _dot(qb[bb, h], c_sc[bb, h].astype(BF16)))
    both = each(lambda bb, h: intra[bb, h] + w_inter[bb, h] * carried[bb, h])
    floor = each(lambda bb, h: jnp.exp(-m_t[bb, h]))
    hh = each(lambda bb, h: both[bb, h][:, :M_DK] / jnp.maximum(jnp.abs(both[bb, h][:, M_DK:]), floor[bb, h]))
    ms = each(lambda bb, h: jnp.mean(hh[bb, h] * hh[bb, h], axis=-1, keepdims=True))
    for bb, h in items:
        hn = hh[bb, h] * lax.rsqrt(ms[bb, h] + NORM_EPS) * hg[:, head(h)]
        o_pre = vo_ref[bb, :, BRANCH_WIDTH + h * M_DK:BRANCH_WIDTH + (h + 1) * M_DK].astype(F32)
        out_ref[bb, :, head(h)] = (hn * _sigmoid(o_pre)).astype(BF16)

    b_last = each(lambda bb, h: b_col[bb, h][L - 1:L, :])
    log_w = each(lambda bb, h: b_last[bb, h] - b_col[bb, h] + i_col[bb, h])
    m_new = each(lambda bb, h: jnp.maximum(b_last[bb, h] + m_prev[bb, h],
                                           jnp.max(log_w[bb, h], axis=0, keepdims=True)))
    decay = each(lambda bb, h: jnp.exp(b_last[bb, h] + m_prev[bb, h] - m_new[bb, h]))
    kw = each(lambda bb, h: (kh[bb, h] * jnp.exp(log_w[bb, h] - m_new[bb, h])).astype(BF16))
    upd = each(lambda bb, h: _dot_tn(kw[bb, h], v_aug[bb, h]))
    for bb, h in items:
        c_sc[bb, h] = decay[bb, h] * c_sc[bb, h] + upd[bb, h]
        m_sc[bb, h:h + 1, :] = jnp.broadcast_to(m_new[bb, h], (1, LANES))


def _mlstm(proj3, gates3, layer, conv_w, conv_b, gate_bias, head_gain):
    b, s, _ = proj3.shape
    L = M_CHUNK
    return pl.pallas_call(
        _mlstm_kernel,
        grid=(b // ML_NB, s // L),
        in_specs=[
            pl.BlockSpec((ML_NB, L, 1024), lambda i, c: (i, c, COL_MQK // 1024)),
            pl.BlockSpec((ML_NB, L, 1024), lambda i, c: (i, c, COL_MV // 1024)),
            pl.BlockSpec((ML_NB, L, LANES), lambda i, c: (i, c, 0)),
            _layer_spec(layer, (M_CONV, 1024)),
            _layer_spec(layer, (1, 1024)),
            _layer_spec(layer, (1, LANES)),
            _layer_spec(layer, (1, BRANCH_WIDTH)),
        ],
        out_specs=pl.BlockSpec((ML_NB, L, BRANCH_WIDTH), lambda i, c: (i, c, 0)),
        out_shape=jax.ShapeDtypeStruct((b, s, BRANCH_WIDTH), BF16),
        scratch_shapes=[
            pltpu.VMEM((ML_NB, M_HEADS, M_DK, 2 * M_DK), F32),
            pltpu.VMEM((ML_NB, 8, LANES), F32),
            pltpu.VMEM((ML_NB, CONV_TAIL, 1024), BF16),
        ],
        compiler_params=pltpu.CompilerParams(
            dimension_semantics=("parallel", "arbitrary"), vmem_limit_bytes=VMEM_LIMIT),
        name="mlstm",
    )(proj3, proj3, gates3, conv_w, conv_b, gate_bias, head_gain)


AT_T = 256
ALIBI_BASE = 256
AT_TK = 512


def _halfnorm(x, gain):
    lane = lax.broadcasted_iota(I32, (1, LANES), 1)
    lo = lane < A_DQK
    x2 = x * x
    ms_lo = jnp.sum(jnp.where(lo, x2, 0.0), axis=-1, keepdims=True)
    ms_hi = jnp.sum(jnp.where(lo, 0.0, x2), axis=-1, keepdims=True)
    ms = jnp.where(lo, ms_lo, ms_hi) * (1.0 / A_DQK)
    return x * lax.rsqrt(ms + NORM_EPS) * gain


AV_ROWS = A_DV + 16


AT_NH = 2


def _attn_kernel(slopes_ref, q_ref, k_ref, v_ref, qg_ref, kg_ref, lam_ref, hg_ref, out_ref,
                 k1_sc, k2_sc, vt_sc, s_sc, p_sc, acc_sc, res_sc, *, lam_init):
    T = AT_T
    TK = AT_TK
    CH = 128
    heads = range(AT_NH)
    qi = pl.program_id(2)
    lane = lax.broadcasted_iota(I32, (1, LANES), 1)
    slope = [slopes_ref[pl.program_id(1) * AT_NH + hd] for hd in heads]

    def cols(hd):
        return slice(hd * LANES, (hd + 1) * LANES)

    @pl.when(qi == 0)
    def _():
        s_len = k_ref.shape[0]
        pos = lax.broadcasted_iota(I32, (s_len, 1), 0)
        for hd in heads:
            kn = _halfnorm(k_ref[:, cols(hd)].astype(F32), kg_ref[...])
            c_loc = (pos & (ALIBI_BASE - 1)).astype(F32) * slope[hd]
            c_blk = (pos >> int(math.log2(ALIBI_BASE))).astype(F32) * (slope[hd] * ALIBI_BASE)
            k1_sc[hd] = jnp.where(lane < A_DQK, kn, jnp.where(lane == A_DQK, c_loc, jnp.where(
                lane == A_DQK + 1, c_blk, 0.0))).astype(BF16)
            k2_sc[hd] = jnp.where(lane >= A_DQK, kn, jnp.where(lane == 0, c_loc, jnp.where(
                lane == 1, c_blk, 0.0))).astype(BF16)
            vt_sc[hd, :A_DV, :] = v_ref[:, cols(hd)].astype(F32).T.astype(BF16)
            vt_sc[hd, A_DV:, :] = jnp.ones((AV_ROWS - A_DV, s_len), BF16)
        res_sc[...] = jnp.ones_like(res_sc)

    lp = lam_ref[...]
    lam = (jnp.exp(jnp.sum(lp[0:1, :] * lp[1:2, :], axis=-1, keepdims=True))
           - jnp.exp(jnp.sum(lp[2:3, :] * lp[3:4, :], axis=-1, keepdims=True)) + lam_init)

    def finish(src, qblock):
        row0 = pl.multiple_of(qblock * T, T)
        hg = hg_ref[...]
        for hd in heads:
            a1 = src[hd, :, :T]
            a2 = src[hd, :, T:]
            o_t = a1[:A_DV, :] / a1[A_DV:A_DV + 1, :] - lam * (a2[:A_DV, :] / a2[A_DV:A_DV + 1, :])
            out_ref[pl.ds(row0, T), cols(hd)] = (
                _rms(o_t.T, hg[:, cols(hd)]) * (1.0 - lam_init)).astype(BF16)

    finish(res_sc, jnp.maximum(qi - 1, 0))

    qn = [_halfnorm(q_ref[:, cols(hd)].astype(F32), qg_ref[...]) * (A_DQK ** -0.5) for hd in heads]
    q1 = [jnp.where(lane < A_DQK, qn[hd], jnp.where(lane < A_DQK + 2, 1.0, 0.0)).astype(BF16) for hd in heads]
    q2 = [jnp.where(lane >= A_DQK, qn[hd], jnp.where(lane < 2, 1.0, 0.0)).astype(BF16) for hd in heads]

    def scores_to_scratch(hd, j):
        start = pl.multiple_of(j * TK, TK)
        s_sc[hd, 0] = _dot_nt(k1_sc[hd, pl.ds(start, TK), :], q1[hd])
        s_sc[hd, 1] = _dot_nt(k2_sc[hd, pl.ds(start, TK), :], q2[hd])

    def causal_bias():
        first = (qi * T - last * TK) >> int(math.log2(CH))
        tri = jnp.where(lax.broadcasted_iota(I32, (CH, LANES), 0) <= lax.broadcasted_iota(I32, (CH, LANES), 1),
                        0.0, NEG_BIG)
        bias = {}
        for c in range(TK // CH):
            for hh in range(T // LANES):
                ahead = first + hh - c
                bias[c, hh] = jnp.where(ahead > 0, 0.0, jnp.where(ahead == 0, tri, NEG_BIG))
        return bias

    def softmax_to_scratch(hd, j, m, bias, keep=None):
        m_out, alpha_out = [], []
        for mp in range(2):
            for hh in range(T // LANES):
                idx = mp * (T // LANES) + hh
                lanes = slice(hh * LANES, (hh + 1) * LANES)

                def chunk(c):
                    sc = s_sc[hd, mp, c * CH:(c + 1) * CH, lanes]
                    return sc if bias is None else sc + bias[c, hh]

                m_new = m[idx]
                for c in range(TK // CH):
                    m_new = jnp.maximum(m_new, jnp.max(chunk(c), axis=0, keepdims=True))
                for c in range(TK // CH):
                    p = jnp.exp(chunk(c) - m_new).astype(BF16)
                    if keep is None:
                        p_sc[hd, c * CH:(c + 1) * CH, idx * LANES:(idx + 1) * LANES] = p
                    else:
                        keep[c, idx] = p
                alpha_out.append(jnp.exp(m[idx] - m_new))
                m_out.append(m_new)
        return tuple(m_out), tuple(alpha_out)

    def values_from_scratch(hd, j, alpha, p=None):
        vt = vt_sc[hd, :, pl.ds(pl.multiple_of(j * TK, TK), TK)]
        pv = _dot(vt, p_sc[hd] if p is None else p)
        for idx in range(2 * T // LANES):
            sl = slice(idx * LANES, (idx + 1) * LANES)
            acc_sc[hd, :, sl] = alpha[idx] * acc_sc[hd, :, sl] + pv[:, sl]

    def body(j, carry):
        m, alpha = carry
        for hd in heads:
            values_from_scratch(hd, jnp.maximum(j - 1, 0), alpha[hd])
        stats = [softmax_to_scratch(hd, j, m[hd], None) for hd in heads]
        for hd in heads:
            scores_to_scratch(hd, j + 1)
        return tuple(st[0] for st in stats), tuple(st[1] for st in stats)

    n_slab = 2 * T // LANES
    last = (qi * T) >> int(math.log2(TK))
    p_sc[...] = jnp.zeros_like(p_sc)
    acc_sc[...] = jnp.zeros_like(acc_sc)
    for hd in heads:
        scores_to_scratch(hd, 0)
    m, alpha = lax.fori_loop(0, last, body, (((jnp.full((1, LANES), NEG_BIG, F32),) * n_slab,) * AT_NH,
                                             ((jnp.ones((1, LANES), F32),) * n_slab,) * AT_NH))
    for hd in heads:
        values_from_scratch(hd, jnp.maximum(last - 1, 0), alpha[hd])
    bias = causal_bias()
    kept = [dict() for _ in heads]
    alpha = [softmax_to_scratch(hd, last, m[hd], bias, kept[hd])[1] for hd in heads]
    for hd in heads:
        p_last = jnp.concatenate([jnp.concatenate([kept[hd][c, idx] for c in range(TK // CH)], axis=0)
                                  for idx in range(n_slab)], axis=1)
        values_from_scratch(hd, last, alpha[hd], p_last)

    res_sc[...] = acc_sc[...]

    @pl.when(qi == pl.num_programs(2) - 1)
    def _():
        finish(acc_sc, qi)


def _attention(proj3, layer, q_gain, k_gain, lam_params, head_gain, lam_init):
    b, s, _ = proj3.shape
    T = AT_T
    W = AT_NH * LANES
    slopes = jnp.asarray(ALIBI_SLOPES, F32)
    return pl.pallas_call(
        functools.partial(_attn_kernel, lam_init=lam_init),
        grid=(b, A_HEADS // AT_NH, s // T),
        in_specs=[
            pl.BlockSpec(memory_space=pltpu.SMEM),
            pl.BlockSpec((None, T, W), lambda i, h, q: (i, q, COL_AQ // W + h)),
            pl.BlockSpec((None, s, W), lambda i, h, q: (i, 0, COL_AK // W + h)),
            pl.BlockSpec((None, s, W), lambda i, h, q: (i, 0, COL_AV // W + h)),
            _layer_spec(layer, (1, LANES)),
            _layer_spec(layer, (1, LANES)),
            _layer_spec(layer, (4, A_DQK)),
            pl.BlockSpec((None, 1, W), lambda i, h, q: (layer, 0, h)),
        ],
        out_specs=pl.BlockSpec((None, s, W), lambda i, h, q: (i, 0, h)),
        out_shape=jax.ShapeDtypeStruct((b, s, BRANCH_WIDTH), BF16),
        scratch_shapes=[pltpu.VMEM((AT_NH, s, LANES), BF16), pltpu.VMEM((AT_NH, s, LANES), BF16),
                        pltpu.VMEM((AT_NH, AV_ROWS, s), BF16), pltpu.VMEM((AT_NH, 2, AT_TK, T), F32),
                        pltpu.VMEM((AT_NH, AT_TK, 2 * T), BF16), pltpu.VMEM((AT_NH, AV_ROWS, 2 * T), F32),
                        pltpu.VMEM((AT_NH, AV_ROWS, 2 * T), F32)],
        compiler_params=pltpu.CompilerParams(
            dimension_semantics=("parallel", "parallel", "arbitrary"), vmem_limit_bytes=VMEM_LIMIT),
        name="diff_attn",
    )(slopes, proj3, proj3, proj3, q_gain, k_gain, lam_params, head_gain)


MG_TM = 512
MG_HALO = 128
ROUTER_ROWS = 16
ROUTE_ROWS = 8


def _merge_kernel(x_ref, hm_ref, ha_ref, u_ref, halo_ref, gm_ref, ga_ref, gp_ref, pw_ref, ps_ref,
                  wb_ref, wo_ref, fg_ref, *rest, seq_tiles, routed):
    if routed:
        wr_ref, br_ref, xo_ref, hf_ref, route_ref = rest
    else:
        xo_ref, hf_ref = rest
    TM = MG_TM
    i = pl.program_id(0)
    seq_tile = i % seq_tiles

    u = u_ref[...]
    halo = jnp.where(seq_tile > 0, halo_ref[...], jnp.zeros_like(halo_ref))
    u_ext = jnp.concatenate([halo, u], axis=0)
    dist = (lax.broadcasted_iota(I32, (MG_HALO, 2 * MG_HALO), 0) + MG_HALO
            - lax.broadcasted_iota(I32, (MG_HALO, 2 * MG_HALO), 1))
    t_pos = seq_tile * TM + lax.broadcasted_iota(I32, (TM, 1), 0)
    ps = ps_ref[...]
    hp_parts = []
    for g, w in enumerate(P_WINDOWS):
        sl = slice(g * P_GC, (g + 1) * P_GC)
        band = jnp.where((dist >= 0) & (dist < w), 1.0, 0.0).astype(BF16)
        ug = u[:, sl]
        sums = jnp.concatenate([_dot(band, u_ext[r:r + 2 * MG_HALO, sl]) for r in range(0, TM, MG_HALO)],
                               axis=0)
        cnt = jnp.minimum(t_pos + 1, w).astype(F32)
        pooled = sums / cnt - ug.astype(F32)
        hp_parts.append((_dot(pooled.astype(BF16), pw_ref[g]) * ps[:, sl]).astype(BF16))
    hp = jnp.concatenate(hp_parts, axis=1)

    merged = (_sigmoid(gm_ref[...].astype(F32)) * _dot(hm_ref[...], wb_ref[0])
              + _sigmoid(ga_ref[...].astype(F32)) * _dot(ha_ref[...], wb_ref[1])
              + _sigmoid(gp_ref[...].astype(F32)) * _dot(hp, wb_ref[2]))
    x_new = x_ref[...] + _dot(merged.astype(BF16), wo_ref[...])
    xo_ref[...] = x_new
    hf = _rms(x_new, fg_ref[...])
    hf_ref[...] = hf.astype(BF16)

    if routed:
        hf_hi = hf.astype(BF16)
        hf_lo = (hf - hf_hi.astype(F32)).astype(BF16)
        wr = wr_ref[...]
        wr_hi = wr.astype(BF16)
        wr_lo = (wr - wr_hi.astype(F32)).astype(BF16)
        logits = (_dot_nt(wr_hi, hf_hi) + _dot_nt(wr_lo, hf_hi) + _dot_nt(wr_hi, hf_lo))[:N_EXPERTS, :]
        logits = logits + br_ref[...]
        expert = lax.broadcasted_iota(I32, (N_EXPERTS, TM), 0).astype(F32)
        m1 = jnp.max(logits, axis=0, keepdims=True)
        i1 = jnp.min(jnp.where(logits == m1, expert, float(N_EXPERTS)), axis=0, keepdims=True)
        rest_l = jnp.where(expert == i1, NEG_BIG, logits)
        m2 = jnp.max(rest_l, axis=0, keepdims=True)
        i2 = jnp.min(jnp.where(rest_l == m2, expert, float(N_EXPERTS)), axis=0, keepdims=True)
        g1 = 1.0 / (1.0 + jnp.exp(m2 - m1))
        route_ref[...] = jnp.concatenate(
            [g1, 1.0 - g1, i1, i2, jnp.zeros((ROUTE_ROWS - 4, TM), F32)], axis=0)


def _merge(x2, hm2, ha2, proj2, layer, pool_w, pool_scale, w_branch, w_out, ffn_gain, seq, router=None):
    tok = x2.shape[0]
    TM = MG_TM
    routed = router is not None
    hb = TM // MG_HALO
    in_specs = [
        pl.BlockSpec((TM, D_MODEL), lambda i: (i, 0)),
        pl.BlockSpec((TM, BRANCH_WIDTH), lambda i: (i, 0)),
        pl.BlockSpec((TM, BRANCH_WIDTH), lambda i: (i, 0)),
        pl.BlockSpec((TM, BRANCH_WIDTH), lambda i: (i, COL_PU // BRANCH_WIDTH)),
        pl.BlockSpec((MG_HALO, BRANCH_WIDTH), lambda i: (jnp.maximum(i * hb - 1, 0), COL_PU // BRANCH_WIDTH)),
        pl.BlockSpec((TM, D_MODEL), lambda i: (i, COL_G // D_MODEL)),
        pl.BlockSpec((TM, D_MODEL), lambda i: (i, COL_G // D_MODEL + 1)),
        pl.BlockSpec((TM, D_MODEL), lambda i: (i, COL_G // D_MODEL + 2)),
        _layer_spec(layer, (len(P_WINDOWS), P_GC, P_GC)),
        _layer_spec(layer, (1, BRANCH_WIDTH)),
        _layer_spec(layer, (3, BRANCH_WIDTH, D_MODEL)),
        _layer_spec(layer, (D_MODEL, D_MODEL)),
        _layer_spec(layer, (1, D_MODEL)),
    ]
    args = [x2, hm2, ha2, proj2, proj2, proj2, proj2, proj2, pool_w, pool_scale, w_branch, w_out, ffn_gain]
    out_specs = [pl.BlockSpec((TM, D_MODEL), lambda i: (i, 0)),
                 pl.BlockSpec((TM, D_MODEL), lambda i: (i, 0))]
    out_shape = [jax.ShapeDtypeStruct((tok, D_MODEL), F32), jax.ShapeDtypeStruct((tok, D_MODEL), BF16)]
    if routed:
        moe_layer, router_w, router_b = router
        in_specs += [_layer_spec(moe_layer, (ROUTER_ROWS, D_MODEL)), _layer_spec(moe_layer, (N_EXPERTS, 1))]
        args += [router_w, router_b]
        out_specs.append(pl.BlockSpec((ROUTE_ROWS, TM), lambda i: (0, i)))
        out_shape.append(jax.ShapeDtypeStruct((ROUTE_ROWS, tok), F32))
    return pl.pallas_call(
        functools.partial(_merge_kernel, seq_tiles=seq // TM, routed=routed),
        grid=(tok // TM,),
        in_specs=in_specs,
        out_specs=out_specs,
        out_shape=out_shape,
        compiler_params=pltpu.CompilerParams(
            dimension_semantics=("parallel",), vmem_limit_bytes=VMEM_LIMIT),
        name="merge_routed" if routed else "merge",
    )(*args)


def _ple_epilogue(x_new, p, pg_gain, wpg, wpp):
    gate = _sigmoid(_dot(_rms(x_new, pg_gain).astype(BF16), wpg))
    return x_new + gate * _dot(p.astype(BF16), wpp)


FF_TM = 512


def _ffn_kernel(hf_ref, x_ref, p_ref, wgu_ref, wd_ref, pgn_ref, wpg_ref, wpp_ref, out_ref):
    hf = hf_ref[...]
    g = _dot(hf, wgu_ref[:, :D_FF])
    u = _dot(hf, wgu_ref[:, D_FF:])
    y = _dot((g * _sigmoid(g) * u).astype(BF16), wd_ref[...])
    out_ref[...] = _ple_epilogue(x_ref[...] + y, p_ref[...], pgn_ref[...], wpg_ref[...], wpp_ref[...])


def _ffn(hf2, x2, p2, p_row0, ffn_layer, w_gu, w_down, layer, ple_gain, wpg, wpp):
    tok = x2.shape[0]
    TM = FF_TM
    once = pl.Buffered(1)
    return pl.pallas_call(
        _ffn_kernel,
        grid=(tok // TM,),
        in_specs=[
            pl.BlockSpec((TM, D_MODEL), lambda i: (i, 0)),
            pl.BlockSpec((TM, D_MODEL), lambda i: (i, 0)),
            pl.BlockSpec((TM, PLE_DIM), lambda i: (i + p_row0 // TM, 0)),
            _layer_spec(ffn_layer, (D_MODEL, 2 * D_FF), pipeline_mode=once),
            _layer_spec(ffn_layer, (D_FF, D_MODEL), pipeline_mode=once),
            _layer_spec(layer, (1, D_MODEL)),
            _layer_spec(layer, (D_MODEL, D_MODEL), pipeline_mode=once),
            _layer_spec(layer, (PLE_DIM, D_MODEL), pipeline_mode=once),
        ],
        out_specs=pl.BlockSpec((TM, D_MODEL), lambda i: (i, 0)),
        out_shape=jax.ShapeDtypeStruct((tok, D_MODEL), F32),
        compiler_params=pltpu.CompilerParams(
            dimension_semantics=("parallel",), vmem_limit_bytes=VMEM_LIMIT),
        name="ffn_dense",
    )(hf2, x2, p2, w_gu, w_down, ple_gain, wpg, wpp)


MOE_TM = 1024
MOE_TF = 512
DP_TM = 256
DP_CH = 256
CB_TB = 256
CB_ALIGN = 16
CB_W = CB_TB + CB_ALIGN


def _dispatch_kernel(clo_ref, chi_ref, pos_ref, gate_ref, hf_ref, x_ref, g_ref, acc_sc, gacc_sc):
    i = pl.program_id(0)
    rows = i * DP_TM + lax.broadcasted_iota(I32, (DP_TM, 1), 0)
    acc_sc[...] = jnp.zeros_like(acc_sc)
    gacc_sc[...] = jnp.zeros_like(gacc_sc)

    lo = clo_ref[i]
    hi = chi_ref[i]

    def chunk(c, live):
        start = pl.multiple_of(c * DP_CH, DP_CH)
        pos = pos_ref[:, pl.ds(start, DP_CH)]
        gate = gate_ref[:, pl.ds(start, DP_CH)]
        want = jnp.where(live, rows, -1)
        hit1 = pos[0:1, :] == want
        hit2 = pos[1:2, :] == want
        onehot = jnp.where(hit1, 1.0, jnp.where(hit2, 1.0, 0.0)).astype(BF16)
        gates = jnp.sum(jnp.where(hit1, gate[0:1, :], jnp.where(hit2, gate[1:2, :], 0.0)),
                        axis=-1, keepdims=True)
        return _dot(onehot, hf_ref[pl.ds(start, DP_CH), :]), gates

    def pair(t, carry):
        c = lo + 2 * t
        x0, g0 = chunk(c, True)
        x1, g1 = chunk(jnp.minimum(c + 1, hi), c + 1 <= hi)
        acc_sc[...] += x0 + x1
        gacc_sc[...] += g0 + g1
        return carry

    lax.fori_loop(0, (hi - lo + 2) >> 1, pair, 0)
    x_ref[...] = acc_sc[...].astype(BF16)
    g_ref[...] = gacc_sc[...]


def _dispatch(hf2, pos_t, gate_t, chunk_lo, chunk_hi):
    tok = hf2.shape[0]
    ns = chunk_lo.shape[0]
    grid_spec = pltpu.PrefetchScalarGridSpec(
        num_scalar_prefetch=2,
        grid=(ns,),
        in_specs=[
            pl.BlockSpec((2, tok), lambda i, lo, hi: (0, 0)),
            pl.BlockSpec((2, tok), lambda i, lo, hi: (0, 0)),
            pl.BlockSpec((tok, D_MODEL), lambda i, lo, hi: (0, 0), pipeline_mode=pl.Buffered(1)),
        ],
        out_specs=[pl.BlockSpec((DP_TM, D_MODEL), lambda i, lo, hi: (i, 0)),
                   pl.BlockSpec((DP_TM, 1), lambda i, lo, hi: (i, 0))],
        scratch_shapes=[pltpu.VMEM((DP_TM, D_MODEL), F32), pltpu.VMEM((DP_TM, 1), F32)],
    )
    return pl.pallas_call(
        _dispatch_kernel,
        grid_spec=grid_spec,
        out_shape=[jax.ShapeDtypeStruct((ns * DP_TM, D_MODEL), BF16),
                   jax.ShapeDtypeStruct((ns * DP_TM, 1), F32)],
        compiler_params=pltpu.CompilerParams(
            dimension_semantics=("arbitrary",), vmem_limit_bytes=VMEM_LIMIT),
        name="moe_dispatch",
    )(chunk_lo, chunk_hi, pos_t, gate_t, hf2)


def _moe_kernel(te_ref, tr_ref, x_ref, gate_ref, wg_ref, wu_ref, wd_ref, y_ref, acc_sc):
    i = pl.program_id(0)
    j = pl.program_id(1)
    last = pl.num_programs(1) - 1
    rows = tr_ref[i]
    half = MOE_TM // 2

    def step(n):
        @pl.when(j == 0)
        def _():
            acc_sc[:n] = jnp.zeros((n, D_MODEL), F32)

        xb = x_ref[:n]
        g = _dot(xb, wg_ref[...].astype(BF16))
        u = _dot(xb, wu_ref[...].astype(BF16))
        acc_sc[:n] += _dot((g * _sigmoid(g) * u).astype(BF16), wd_ref[...].astype(BF16))

        @pl.when(j == last)
        def _():
            y_ref[:n] = (acc_sc[:n] * gate_ref[:n]).astype(BF16)
            if n < MOE_TM:
                y_ref[n:] = jnp.zeros((MOE_TM - n, D_MODEL), BF16)

    @pl.when(rows > half)
    def _():
        step(MOE_TM)

    @pl.when((rows > 0) & (rows <= half))
    def _():
        step(half)

    @pl.when((rows == 0) & (j == last))
    def _():
        y_ref[...] = jnp.zeros_like(y_ref)


def _moe(x_sorted, gate_sorted, w_gu, w_down, tile_expert, tile_rows):
    TM, TF = MOE_TM, MOE_TF
    nt = tile_expert.shape[0]
    nf = D_FF_EXPERT // TF

    def col(j, tv, i):
        return jnp.where(tv[i] > 0, j, nf - 1)

    grid_spec = pltpu.PrefetchScalarGridSpec(
        num_scalar_prefetch=2,
        grid=(nt, nf),
        in_specs=[
            pl.BlockSpec((TM, D_MODEL), lambda i, j, te, tv: (i, 0)),
            pl.BlockSpec((TM, 1), lambda i, j, te, tv: (i, 0)),
            pl.BlockSpec((None, D_MODEL, TF), lambda i, j, te, tv: (te[i], 0, col(j, tv, i))),
            pl.BlockSpec((None, D_MODEL, TF), lambda i, j, te, tv: (te[i], 0, col(j, tv, i) + nf)),
            pl.BlockSpec((None, TF, D_MODEL), lambda i, j, te, tv: (te[i], col(j, tv, i), 0)),
        ],
        out_specs=pl.BlockSpec((TM, D_MODEL), lambda i, j, te, tv: (i, 0)),
        scratch_shapes=[pltpu.VMEM((TM, D_MODEL), F32)],
    )
    return pl.pallas_call(
        _moe_kernel,
        grid_spec=grid_spec,
        out_shape=jax.ShapeDtypeStruct((nt * TM, D_MODEL), BF16),
        compiler_params=pltpu.CompilerParams(
            dimension_semantics=("arbitrary", "arbitrary"), vmem_limit_bytes=VMEM_LIMIT),
        name="moe_experts",
    )(tile_expert, tile_rows, x_sorted, gate_sorted, w_gu, w_gu, w_down)


def _combine_kernel(w0_ref, rng_ref, x_ref, p_ref, pos_ref, y_hbm, pgn_ref, wpg_ref, wpp_ref, out_ref,
                    win_sc, acc_sc, sem):
    i = pl.program_id(0)
    slot = i & 1

    def window_copy(step, buf, e):
        start = pl.multiple_of(w0_ref[step * N_EXPERTS + e], CB_ALIGN)
        return pltpu.make_async_copy(y_hbm.at[pl.ds(start, CB_W), :], win_sc.at[buf, e], sem.at[buf, e])

    @pl.when(i == 0)
    def _():
        for e in range(N_EXPERTS):
            window_copy(0, 0, e).start()

    @pl.when(i + 1 < pl.num_programs(0))
    def _():
        for e in range(N_EXPERTS):
            window_copy(i + 1, 1 - slot, e).start()

    pos = pos_ref[...]
    pos1 = pos[:, 0:1]
    pos2 = pos[:, 1:2]

    n_win = pl.num_programs(0) * N_EXPERTS

    def row_begin(e):
        return rng_ref[i * N_EXPERTS + e]

    def row_end(e):
        return rng_ref[n_win + i * N_EXPERTS + e]

    def onehot(e, first, width):
        rows = lax.broadcasted_iota(I32, (1, width), 1) + (first + w0_ref[i * N_EXPERTS + e])
        rows = jnp.where((rows >= row_begin(e)) & (rows < row_end(e)), rows, -1)
        return jnp.where(pos1 == rows, 1.0, jnp.where(pos2 == rows, 1.0, 0.0)).astype(BF16)

    moe_out = jnp.zeros((CB_TB, D_MODEL), F32)
    for e in range(N_EXPERTS):
        window_copy(i, slot, e).wait()
        moe_out = moe_out + _dot(onehot(e, 0, CB_TB), win_sc[slot, e, :CB_TB, :])
    acc_sc[...] = moe_out

    for e in range(N_EXPERTS):
        @pl.when(row_end(e) - w0_ref[i * N_EXPERTS + e] > CB_TB)
        def _():
            acc_sc[...] += _dot(onehot(e, CB_TB, CB_ALIGN), win_sc[slot, e, CB_TB:, :])

    out_ref[...] = _ple_epilogue(x_ref[...] + acc_sc[...], p_ref[...], pgn_ref[...], wpg_ref[...],
                                 wpp_ref[...])


def _combine(x2, p2, p_row0, pos, y_sorted, win_start, win_rows, layer, ple_gain, wpg, wpp):
    tok = x2.shape[0]
    TB = CB_TB
    grid_spec = pltpu.PrefetchScalarGridSpec(
        num_scalar_prefetch=2,
        grid=(tok // TB,),
        in_specs=[
            pl.BlockSpec((TB, D_MODEL), lambda i, w0, wt: (i, 0)),
            pl.BlockSpec((TB, PLE_DIM), lambda i, w0, wt: (i + p_row0 // TB, 0)),
            pl.BlockSpec((TB, 2), lambda i, w0, wt: (i, 0)),
            pl.BlockSpec(memory_space=pl.ANY),
            _layer_spec(layer, (1, D_MODEL)),
            _layer_spec(layer, (D_MODEL, D_MODEL)),
            _layer_spec(layer, (PLE_DIM, D_MODEL)),
        ],
        out_specs=pl.BlockSpec((TB, D_MODEL), lambda i, w0, wt: (i, 0)),
        scratch_shapes=[pltpu.VMEM((2, N_EXPERTS, CB_W, D_MODEL), BF16), pltpu.VMEM((TB, D_MODEL), F32),
                        pltpu.SemaphoreType.DMA((2, N_EXPERTS))],
    )
    return pl.pallas_call(
        _combine_kernel,
        grid_spec=grid_spec,
        out_shape=jax.ShapeDtypeStruct((tok, D_MODEL), F32),
        compiler_params=pltpu.CompilerParams(
            dimension_semantics=("arbitrary",), vmem_limit_bytes=VMEM_LIMIT),
        name="moe_combine",
    )(win_start, win_rows, x2, p2, pos, y_sorted, ple_gain, wpg, wpp)


def _route_metadata(route, tok):
    TM = MOE_TM
    nt = (2 * tok) // TM + N_EXPERTS
    rows = nt * TM
    g1, g2 = route[0], route[1]
    i1, i2 = route[2].astype(I32), route[3].astype(I32)
    experts = jnp.arange(N_EXPERTS, dtype=I32)
    member = ((i1[:, None] == experts) | (i2[:, None] == experts)).astype(I32)
    rank = jnp.cumsum(member, axis=0) - member
    counts = jnp.sum(member, axis=0)
    tiles_e = (counts + TM - 1) // TM
    tile_end = jnp.cumsum(tiles_e)
    tile_start = tile_end - tiles_e
    seg_start = tile_start * TM
    sel1 = i1[:, None] == experts
    sel2 = i2[:, None] == experts
    pos1 = jnp.sum(jnp.where(sel1, seg_start[None, :] + rank, 0), axis=1).astype(I32)
    pos2 = jnp.sum(jnp.where(sel2, seg_start[None, :] + rank, 0), axis=1).astype(I32)

    tiles = jnp.arange(nt, dtype=I32)
    n_used = tile_end[-1]
    tile_rows = (tiles < n_used).astype(I32)
    te = jnp.sum((tiles[:, None] >= tile_end[None, :]).astype(I32), axis=1)
    te_last = jnp.sum(((n_used - 1) >= tile_end).astype(I32))
    tile_expert = jnp.where(tile_rows > 0, jnp.minimum(te, N_EXPERTS - 1), te_last).astype(I32)
    mine = tile_expert[:, None] == experts[None, :]
    rows_left = (jnp.sum(jnp.where(mine, counts[None, :], 0), axis=1)
                 - (tiles - jnp.sum(jnp.where(mine, tile_start[None, :], 0), axis=1)) * TM)
    tile_rows = jnp.where(tile_rows > 0, jnp.clip(rows_left, 0, TM), 0).astype(I32)

    steps = jnp.arange(rows // DP_TM, dtype=I32)
    step_tile = steps // (TM // DP_TM)
    step_valid = step_tile < n_used
    step_expert = jnp.minimum(jnp.sum((step_tile[:, None] >= tile_end[None, :]).astype(I32), axis=1),
                              N_EXPERTS - 1)
    pick = step_expert[:, None] == experts[None, :]
    rank_lo = steps * DP_TM - jnp.sum(jnp.where(pick, seg_start[None, :], 0), axis=1)
    rank_hi = jnp.minimum(rank_lo + DP_TM, jnp.sum(jnp.where(pick, counts[None, :], 0), axis=1)) - 1
    before = jnp.sum(jnp.where(pick[:, :, None], rank[::DP_CH, :].T[None, :, :], 0), axis=1)
    chunk_lo = jnp.sum((before <= rank_lo[:, None]).astype(I32), axis=1) - 1
    chunk_hi = jnp.sum((before <= rank_hi[:, None]).astype(I32), axis=1) - 1
    live = step_valid & (rank_hi >= rank_lo)
    chunk_lo = jnp.where(live, chunk_lo, 0).astype(I32)
    chunk_hi = jnp.where(live, chunk_hi, -1).astype(I32)

    blk_start = seg_start[None, :] + rank[::CB_TB, :]
    win_start = jnp.minimum((blk_start // CB_ALIGN) * CB_ALIGN, rows - CB_W).astype(I32)
    blk_count = jnp.concatenate([rank[CB_TB::CB_TB, :], counts[None, :]], axis=0) - rank[::CB_TB, :]
    win_rows = jnp.concatenate([blk_start.reshape(-1), (blk_start + blk_count).reshape(-1)]).astype(I32)
    win_start = win_start.reshape(-1)
    pos = jnp.stack([pos1, pos2], axis=1)
    return (tile_expert, tile_rows, chunk_lo, chunk_hi, jnp.stack([pos1, pos2]), jnp.stack([g1, g2]),
            pos, win_start, win_rows)


def _pack_w_in(w):
    o = [0, 1024, 1536, 2048, 2052, 2056, 2568, 3080, 3592, 4104, 7176]
    main = jnp.concatenate([w[..., o[0]:o[3]], w[..., o[5]:o[10]]], axis=-1).astype(BF16)
    gates = jnp.pad(w[..., o[3]:o[5]], ((0, 0), (0, 0), (0, LANES - 2 * M_HEADS))).astype(BF16)
    return main, gates


def kernel(x, p, attn_norm, w_in, m_conv_w, m_conv_b, m_gate_bias, m_head_norm, a_q_norm, a_k_norm,
           a_lambda, a_head_norm, pool_w, pool_scale, w_branch, w_out, ffn_norm, dense_w_gu, dense_w_down,
           router_w, router_b, moe_w_gu, moe_w_down, ple_norm, ple_w_gate, ple_w_proj):
    b, s, d = x.shape
    depth = w_in.shape[0]
    tok = b * s
    x2 = x.reshape(tok, d)

    def rows(a):
        return a.reshape(a.shape[0], 1, a.shape[1])

    w_main, w_if = _pack_w_in(w_in)
    gate_bias = rows(jnp.pad(m_gate_bias, ((0, 0), (0, LANES - 2 * M_HEADS))))
    q_gain = rows(jnp.tile(a_q_norm, (1, 2)))
    k_gain = rows(jnp.tile(a_k_norm, (1, 2)))
    p2 = p.reshape(depth * tok, PLE_DIM)
    pool_wb, w_branch_b, w_out_b = pool_w.astype(BF16), w_branch.astype(BF16), w_out.astype(BF16)
    ple = (rows(ple_norm), ple_w_gate.astype(BF16), ple_w_proj.astype(BF16))
    dense_gu, dense_down = dense_w_gu.astype(BF16), dense_w_down.astype(BF16)
    router_wt = jnp.pad(jnp.swapaxes(router_w, 1, 2), ((0, 0), (0, ROUTER_ROWS - N_EXPERTS), (0, 0)))
    router_bc = router_b.reshape(router_b.shape[0], N_EXPERTS, 1)

    for layer in range(depth):
        proj, gates = _inproj(x2, layer, rows(attn_norm), w_main, w_if)
        proj3 = proj.reshape(b, s, PROJ_WIDTH)
        h_m = _mlstm(proj3, gates.reshape(b, s, LANES), layer, m_conv_w, rows(m_conv_b), gate_bias,
                     rows(m_head_norm))
        lam_init = 0.8 - 0.6 * math.exp(-0.3 * layer)
        h_a = _attention(proj3, layer, q_gain, k_gain, a_lambda, rows(a_head_norm), lam_init)
        merge_args = (x2, h_m.reshape(tok, BRANCH_WIDTH), h_a.reshape(tok, BRANCH_WIDTH), proj, layer,
                      pool_wb, rows(pool_scale), w_branch_b, w_out_b, rows(ffn_norm), s)
        j = layer // 2
        if layer % 2 == 0:
            x_mid, hf = _merge(*merge_args)
            x2 = _ffn(hf, x_mid, p2, layer * tok, j, dense_gu, dense_down, layer, *ple)
        else:
            x_mid, hf, route = _merge(*merge_args, router=(j, router_wt, router_bc))
            (tile_expert, tile_rows, chunk_lo, chunk_hi, pos_t, gate_t, pos,
             win_start, win_rows) = _route_metadata(route, tok)
            x_sorted, gate_sorted = _dispatch(hf, pos_t, gate_t, chunk_lo, chunk_hi)
            y_sorted = _moe(x_sorted, gate_sorted, moe_w_gu[j], moe_w_down[j], tile_expert, tile_rows)
            x2 = _combine(x_mid, p2, layer * tok, pos, y_sorted, win_start, win_rows, layer, *ple)
    return x2.reshape(b, s, d)
```

```python
import functools
import math

import jax
import jax.numpy as jnp
from jax import lax
from jax.experimental import pallas as pl
from jax.experimental.pallas import tpu as pltpu

F32 = jnp.float32
BF16 = jnp.bfloat16
I32 = jnp.int32

D_MODEL = 1024
PLE_DIM = 256
NORM_EPS = 1e-6
BRANCH_WIDTH = 512

M_HEADS = 4
M_DK = 128
M_CONV = 4
M_CHUNK = 128

A_HEADS = 4
A_DV = 128
A_DQK = 64
ALIBI_SLOPES = tuple(2.0 ** (-8.0 * (h + 1) / A_HEADS) for h in range(A_HEADS))

P_WINDOWS = (2, 4, 8, 16)
P_GC = 128

D_FF = 2816
N_EXPERTS = 8
D_FF_EXPERT = 3584

PROJ_WIDTH = 7168
COL_MQK, COL_MV, COL_MO, COL_AQ, COL_AK, COL_AV, COL_PU, COL_G = 0, 1024, 1536, 2048, 2560, 3072, 3584, 4096

LANES = 128
NEG_BIG = -1e30

VMEM_LIMIT = 56 * 1024 * 1024


def _sigmoid(x):
    return 0.5 * jnp.tanh(0.5 * x) + 0.5


def _rms(x, gain):
    return x * lax.rsqrt(jnp.mean(x * x, axis=-1, keepdims=True) + NORM_EPS) * gain


def _dot(a, b):
    return jnp.dot(a, b, preferred_element_type=F32)


def _dot_nt(a, b):
    return lax.dot_general(a, b, (((1,), (1,)), ((), ())), preferred_element_type=F32)


def _dot_tn(a, b):
    return lax.dot_general(a, b, (((0,), (0,)), ((), ())), preferred_element_type=F32)


def _layer_spec(layer, shape, **kw):
    zeros = (0,) * len(shape)
    return pl.BlockSpec((None,) + tuple(shape), lambda *_: (layer,) + zeros, **kw)


IN_TM = 512
IN_TN = 1024


def _inproj_kernel(x_ref, gain_ref, w_ref, wif_ref, out_ref, gates_ref):
    hn = _rms(x_ref[...], gain_ref[...]).astype(BF16)
    gates_ref[...] = _dot(hn, wif_ref[...])
    for n in range(PROJ_WIDTH // IN_TN):
        cols = slice(n * IN_TN, (n + 1) * IN_TN)
        out_ref[:, cols] = _dot(hn, w_ref[:, cols]).astype(BF16)


def _inproj(x2, layer, gain, w_main, w_if):
    tok = x2.shape[0]
    return pl.pallas_call(
        _inproj_kernel,
        grid=(tok // IN_TM,),
        in_specs=[
            pl.BlockSpec((IN_TM, D_MODEL), lambda i: (i, 0)),
            _layer_spec(layer, (1, D_MODEL)),
            _layer_spec(layer, (D_MODEL, PROJ_WIDTH), pipeline_mode=pl.Buffered(1)),
            _layer_spec(layer, (D_MODEL, LANES)),
        ],
        out_specs=[
            pl.BlockSpec((IN_TM, PROJ_WIDTH), lambda i: (i, 0)),
            pl.BlockSpec((IN_TM, LANES), lambda i: (i, 0)),
        ],
        out_shape=[
            jax.ShapeDtypeStruct((tok, PROJ_WIDTH), BF16),
            jax.ShapeDtypeStruct((tok, LANES), F32),
        ],
        compiler_params=pltpu.CompilerParams(
            dimension_semantics=("parallel",), vmem_limit_bytes=VMEM_LIMIT),
        name="inproj",
    )(x2, gain, w_main, w_if)


CONV_TAIL = 16
ML_NB = 2


def _mlstm_kernel(qk_ref, vo_ref, g_ref, cw_ref, cb_ref, gb_ref, hg_ref, out_ref,
                  c_sc, m_sc, prev_sc):
    @pl.when(pl.program_id(1) == 0)
    def _():
        c_sc[...] = jnp.zeros_like(c_sc)
        m_sc[...] = jnp.zeros_like(m_sc)
        prev_sc[...] = jnp.zeros_like(prev_sc)

    L = M_CHUNK
    delay = (lax.broadcasted_iota(I32, (L, CONV_TAIL + L), 0) + CONV_TAIL
             - lax.broadcasted_iota(I32, (L, CONV_TAIL + L), 1))
    shift = {s: jnp.where(delay == s, 1.0, 0.0).astype(BF16) for s in range(1, M_CONV)}
    lane = lax.broadcasted_iota(I32, (L, LANES), 1)
    rowl = lax.broadcasted_iota(I32, (L, LANES), 0)
    causal = lax.broadcasted_iota(I32, (L, L), 0) >= lax.broadcasted_iota(I32, (L, L), 1)
    is_f = (lane >= M_HEADS) & (lane < 2 * M_HEADS)
    cw = cw_ref[...]
    hg = hg_ref[...]
    items = [(bb, h) for bb in range(ML_NB) for h in range(M_HEADS)]

    qk, gc, bcum, gc_t, bcum_t = [], [], [], [], []
    for bb in range(ML_NB):
        xb = qk_ref[bb]
        ext = jnp.concatenate([prev_sc[bb], xb], axis=0)
        y = cb_ref[...] + cw[M_CONV - 1:M_CONV, :] * xb.astype(F32)
        for s in range(1, M_CONV):
            y = y + cw[M_CONV - 1 - s:M_CONV - s, :] * _dot(shift[s], ext)
        prev_sc[bb] = xb[L - CONV_TAIL:]
        qk.append(y * _sigmoid(y))

        g = g_ref[bb] + gb_ref[...]
        logf = jnp.minimum(g, 0.0) - jnp.log(1.0 + jnp.exp(-jnp.abs(g)))
        gcb = jnp.where(is_f, logf, jnp.where(lane < M_HEADS, g, 0.0))
        acc = jnp.where(is_f, logf, 0.0)
        k = 1
        while k < L:
            acc = acc + jnp.where(rowl >= k, pltpu.roll(acc, k, 0), 0.0)
            k *= 2
        gc.append(gcb)
        bcum.append(acc)
        gc_t.append(gcb.T)
        bcum_t.append(acc.T)

    def each(fn):
        return {it: fn(*it) for it in items}

    def head(h):
        return slice(h * M_DK, (h + 1) * M_DK)

    b_col = each(lambda bb, h: bcum[bb][:, M_HEADS + h:M_HEADS + h + 1])
    i_col = each(lambda bb, h: gc[bb][:, h:h + 1])
    m_prev = each(lambda bb, h: m_sc[bb, h:h + 1, 0:1])
    log_d = each(lambda bb, h: jnp.where(
        causal, b_col[bb, h] - bcum_t[bb][M_HEADS + h:M_HEADS + h + 1, :] + gc_t[bb][h:h + 1, :], NEG_BIG))
    inter = each(lambda bb, h: b_col[bb, h] + m_prev[bb, h])
    m_t = each(lambda bb, h: jnp.maximum(jnp.max(log_d[bb, h], axis=-1, keepdims=True), inter[bb, h]))
    d_w = each(lambda bb, h: jnp.exp(log_d[bb, h] - m_t[bb, h]))
    w_inter = each(lambda bb, h: jnp.exp(inter[bb, h] - m_t[bb, h]))

    kh = each(lambda bb, h: qk[bb][:, BRANCH_WIDTH + h * M_DK:BRANCH_WIDTH + (h + 1) * M_DK] * (M_DK ** -0.5))
    qb = each(lambda bb, h: qk[bb][:, head(h)].astype(BF16))
    ones = jnp.ones((L, M_DK), BF16)
    v_aug = each(lambda bb, h: jnp.concatenate([vo_ref[bb, :, head(h)], ones], axis=1))
    sc = each(lambda bb, h: (_dot_nt(qb[bb, h], kh[bb, h].astype(BF16)) * d_w[bb, h]).astype(BF16))

    intra = each(lambda bb, h: _dot(sc[bb, h], v_aug[bb, h]))
    carried = each(lambda bb, h: _dot(qb[bb, h], c_sc[bb, h].astype(BF16)))
    both = each(lambda bb, h: intra[bb, h] + w_inter[bb, h] * carried[bb, h])
    floor = each(lambda bb, h: jnp.exp(-m_t[bb, h]))
    hh = each(lambda bb, h: both[bb, h][:, :M_DK] / jnp.maximum(jnp.abs(both[bb, h][:, M_DK:]), floor[bb, h]))
    ms = each(lambda bb, h: jnp.mean(hh[bb, h] * hh[bb, h], axis=-1, keepdims=True))
    for bb, h in items:
        hn = hh[bb, h] * lax.rsqrt(ms[bb, h] + NORM_EPS) * hg[:, head(h)]
        o_pre = vo_ref[bb, :, BRANCH_WIDTH + h * M_DK:BRANCH_WIDTH + (h + 1) * M_DK].astype(F32)
        out_ref[bb, :, head(h)] = (hn * _sigmoid(o_pre)).astype(BF16)

    b_last = each(lambda bb, h: b_col[bb, h][L - 1:L, :])
    log_w = each(lambda bb, h: b_last[bb, h] - b_col[bb, h] + i_col[bb, h])
    m_new = each(lambda bb, h: jnp.maximum(b_last[bb, h] + m_prev[bb, h],
                                           jnp.max(log_w[bb, h], axis=0, keepdims=True)))
    decay = each(lambda bb, h: jnp.exp(b_last[bb, h] + m_prev[bb, h] - m_new[bb, h]))
    kw = each(lambda bb, h: (kh[bb, h] * jnp.exp(log_w[bb, h] - m_new[bb, h])).astype(BF16))
    upd = each(lambda bb, h: _dot_tn(kw[bb, h], v_aug[bb, h]))
    for bb, h in items:
        c_sc[bb, h] = decay[bb, h] * c_sc[bb, h] + upd[bb, h]
        m_sc[bb, h:h + 1, :] = jnp.broadcast_to(m_new[bb, h], (1, LANES))


def _mlstm(proj3, gates3, layer, conv_w, conv_b, gate_bias, head_gain):
    b, s, _ = proj3.shape
    L = M_CHUNK
    return pl.pallas_call(
        _mlstm_kernel,
        grid=(b // ML_NB, s // L),
        in_specs=[
            pl.BlockSpec((ML_NB, L, 1024), lambda i, c: (i, c, COL_MQK // 1024)),
            pl.BlockSpec((ML_NB, L, 1024), lambda i, c: (i, c, COL_MV // 1024)),
            pl.BlockSpec((ML_NB, L, LANES), lambda i, c: (i, c, 0)),
            _layer_spec(layer, (M_CONV, 1024)),
            _layer_spec(layer, (1, 1024)),
            _layer_spec(layer, (1, LANES)),
            _layer_spec(layer, (1, BRANCH_WIDTH)),
        ],
        out_specs=pl.BlockSpec((ML_NB, L, BRANCH_WIDTH), lambda i, c: (i, c, 0)),
        out_shape=jax.ShapeDtypeStruct((b, s, BRANCH_WIDTH), BF16),
        scratch_shapes=[
            pltpu.VMEM((ML_NB, M_HEADS, M_DK, 2 * M_DK), F32),
            pltpu.VMEM((ML_NB, 8, LANES), F32),
            pltpu.VMEM((ML_NB, CONV_TAIL, 1024), BF16),
        ],
        compiler_params=pltpu.CompilerParams(
            dimension_semantics=("parallel", "arbitrary"), vmem_limit_bytes=VMEM_LIMIT),
        name="mlstm",
    )(proj3, proj3, gates3, conv_w, conv_b, gate_bias, head_gain)


AT_T = 256
ALIBI_BASE = 256
AT_TK = 512


def _halfnorm(x, gain):
    lane = lax.broadcasted_iota(I32, (1, LANES), 1)
    lo = lane < A_DQK
    x2 = x * x
    ms_lo = jnp.sum(jnp.where(lo, x2, 0.0), axis=-1, keepdims=True)
    ms_hi = jnp.sum(jnp.where(lo, 0.0, x2), axis=-1, keepdims=True)
    ms = jnp.where(lo, ms_lo, ms_hi) * (1.0 / A_DQK)
    return x * lax.rsqrt(ms + NORM_EPS) * gain


AV_ROWS = A_DV + 16


AT_NH = 2


def _attn_kernel(slopes_ref, q_ref, k_ref, v_ref, qg_ref, kg_ref, lam_ref, hg_ref, out_ref,
                 k1_sc, k2_sc, vt_sc, s_sc, p_sc, acc_sc, res_sc, *, lam_init):
    T = AT_T
    TK = AT_TK
    CH = 128
    heads = range(AT_NH)
    qi = pl.program_id(2)
    lane = lax.broadcasted_iota(I32, (1, LANES), 1)
    slope = [slopes_ref[pl.program_id(1) * AT_NH + hd] for hd in heads]

    def cols(hd):
        return slice(hd * LANES, (hd + 1) * LANES)

    @pl.when(qi == 0)
    def _():
        s_len = k_ref.shape[0]
        pos = lax.broadcasted_iota(I32, (s_len, 1), 0)
        for hd in heads:
            kn = _halfnorm(k_ref[:, cols(hd)].astype(F32), kg_ref[...])
            c_loc = (pos & (ALIBI_BASE - 1)).astype(F32) * slope[hd]
            c_blk = (pos >> int(math.log2(ALIBI_BASE))).astype(F32) * (slope[hd] * ALIBI_BASE)
            k1_sc[hd] = jnp.where(lane < A_DQK, kn, jnp.where(lane == A_DQK, c_loc, jnp.where(
                lane == A_DQK + 1, c_blk, 0.0))).astype(BF16)
            k2_sc[hd] = jnp.where(lane >= A_DQK, kn, jnp.where(lane == 0, c_loc, jnp.where(
                lane == 1, c_blk, 0.0))).astype(BF16)
            vt_sc[hd, :A_DV, :] = v_ref[:, cols(hd)].astype(F32).T.astype(BF16)
            vt_sc[hd, A_DV:, :] = jnp.ones((AV_ROWS - A_DV, s_len), BF16)
        res_sc[...] = jnp.ones_like(res_sc)

    lp = lam_ref[...]
    lam = (jnp.exp(jnp.sum(lp[0:1, :] * lp[1:2, :], axis=-1, keepdims=True))
           - jnp.exp(jnp.sum(lp[2:3, :] * lp[3:4, :], axis=-1, keepdims=True)) + lam_init)

    def finish(src, qblock):
        row0 = pl.multiple_of(qblock * T, T)
        hg = hg_ref[...]
        for hd in heads:
            per_map = T // LANES
            a1 = jnp.concatenate([src[hd, i] for i in range(per_map)], axis=1)
            a2 = jnp.concatenate([src[hd, per_map + i] for i in range(per_map)], axis=1)
            o_t = a1[:A_DV, :] / a1[A_DV:A_DV + 1, :] - lam * (a2[:A_DV, :] / a2[A_DV:A_DV + 1, :])
            out_ref[pl.ds(row0, T), cols(hd)] = (
                _rms(o_t.T, hg[:, cols(hd)]) * (1.0 - lam_init)).astype(BF16)

    finish(res_sc, jnp.maximum(qi - 1, 0))

    qn = [_halfnorm(q_ref[:, cols(hd)].astype(F32), qg_ref[...]) * (A_DQK ** -0.5) for hd in heads]
    q1 = [jnp.where(lane < A_DQK, qn[hd], jnp.where(lane < A_DQK + 2, 1.0, 0.0)).astype(BF16) for hd in heads]
    q2 = [jnp.where(lane >= A_DQK, qn[hd], jnp.where(lane < 2, 1.0, 0.0)).astype(BF16) for hd in heads]

    def scores_to_scratch(hd, j):
        start = pl.multiple_of(j * TK, TK)
        for mp, (k_sc, q) in enumerate(((k1_sc, q1), (k2_sc, q2))):
            sc = _dot_nt(k_sc[hd, pl.ds(start, TK), :], q[hd])
            for hh in range(T // LANES):
                s_sc[hd, mp, hh] = sc[:, hh * LANES:(hh + 1) * LANES]

    def causal_bias():
        first = (qi * T - last * TK) >> int(math.log2(CH))
        tri = jnp.where(lax.broadcasted_iota(I32, (CH, LANES), 0) <= lax.broadcasted_iota(I32, (CH, LANES), 1),
                        0.0, NEG_BIG)
        bias = {}
        for c in range(TK // CH):
            for hh in range(T // LANES):
                ahead = first + hh - c
                bias[c, hh] = jnp.where(ahead > 0, 0.0, jnp.where(ahead == 0, tri, NEG_BIG))
        return bias

    def softmax_to_scratch(hd, j, m, bias, keep=None):
        m_out, alpha_out = [], []
        for mp in range(2):
            for hh in range(T // LANES):
                idx = mp * (T // LANES) + hh

                def chunk(c):
                    sc = s_sc[hd, mp, hh, c * CH:(c + 1) * CH, :]
                    return sc if bias is None else sc + bias[c, hh]

                m_new = m[idx]
                for c in range(TK // CH):
                    m_new = jnp.maximum(m_new, jnp.max(chunk(c), axis=0, keepdims=True))
                for c in range(TK // CH):
                    p = jnp.exp(chunk(c) - m_new).astype(BF16)
                    if keep is None:
                        p_sc[hd, idx, c * CH:(c + 1) * CH, :] = p
                    else:
                        keep[c, idx] = p
                alpha_out.append(jnp.exp(m[idx] - m_new))
                m_out.append(m_new)
        return tuple(m_out), tuple(alpha_out)

    def values_from_scratch(hd, j, alpha, p=None):
        vt = vt_sc[hd, :, pl.ds(pl.multiple_of(j * TK, TK), TK)]
        if p is None:
            p = jnp.concatenate([p_sc[hd, idx] for idx in range(2 * T // LANES)], axis=1)
        pv = _dot(vt, p)
        for idx in range(2 * T // LANES):
            acc_sc[hd, idx] = alpha[idx] * acc_sc[hd, idx] + pv[:, idx * LANES:(idx + 1) * LANES]

    def body(j, carry):
        m, alpha = carry
        for hd in heads:
            values_from_scratch(hd, jnp.maximum(j - 1, 0), alpha[hd])
        stats = [softmax_to_scratch(hd, j, m[hd], None) for hd in heads]
        for hd in heads:
            scores_to_scratch(hd, j + 1)
        return tuple(st[0] for st in stats), tuple(st[1] for st in stats)

    n_slab = 2 * T // LANES
    last = (qi * T) >> int(math.log2(TK))
    p_sc[...] = jnp.zeros_like(p_sc)
    acc_sc[...] = jnp.zeros_like(acc_sc)
    for hd in heads:
        scores_to_scratch(hd, 0)
    m, alpha = lax.fori_loop(0, last, body, (((jnp.full((1, LANES), NEG_BIG, F32),) * n_slab,) * AT_NH,
                                             ((jnp.ones((1, LANES), F32),) * n_slab,) * AT_NH))
    for hd in heads:
        values_from_scratch(hd, jnp.maximum(last - 1, 0), alpha[hd])
    bias = causal_bias()
    kept = [dict() for _ in heads]
    alpha = [softmax_to_scratch(hd, last, m[hd], bias, kept[hd])[1] for hd in heads]
    for hd in heads:
        p_last = jnp.concatenate([jnp.concatenate([kept[hd][c, idx] for c in range(TK // CH)], axis=0)
                                  for idx in range(n_slab)], axis=1)
        values_from_scratch(hd, last, alpha[hd], p_last)

    res_sc[...] = acc_sc[...]

    @pl.when(qi == pl.num_programs(2) - 1)
    def _():
        finish(acc_sc, qi)


def _attention(proj3, layer, q_gain, k_gain, lam_params, head_gain, lam_init):
    b, s, _ = proj3.shape
    T = AT_T
    W = AT_NH * LANES
    slopes = jnp.asarray(ALIBI_SLOPES, F32)
    return pl.pallas_call(
        functools.partial(_attn_kernel, lam_init=lam_init),
        grid=(b, A_HEADS // AT_NH, s // T),
        in_specs=[
            pl.BlockSpec(memory_space=pltpu.SMEM),
            pl.BlockSpec((None, T, W), lambda i, h, q: (i, q, COL_AQ // W + h)),
            pl.BlockSpec((None, s, W), lambda i, h, q: (i, 0, COL_AK // W + h)),
            pl.BlockSpec((None, s, W), lambda i, h, q: (i, 0, COL_AV // W + h)),
            _layer_spec(layer, (1, LANES)),
            _layer_spec(layer, (1, LANES)),
            _layer_spec(layer, (4, A_DQK)),
            pl.BlockSpec((None, 1, W), lambda i, h, q: (layer, 0, h)),
        ],
        out_specs=pl.BlockSpec((None, s, W), lambda i, h, q: (i, 0, h)),
        out_shape=jax.ShapeDtypeStruct((b, s, BRANCH_WIDTH), BF16),
        scratch_shapes=[pltpu.VMEM((AT_NH, s, LANES), BF16), pltpu.VMEM((AT_NH, s, LANES), BF16),
                        pltpu.VMEM((AT_NH, AV_ROWS, s), BF16),
                        pltpu.VMEM((AT_NH, 2, T // LANES, AT_TK, LANES), F32),
                        pltpu.VMEM((AT_NH, 2 * T // LANES, AT_TK, LANES), BF16),
                        pltpu.VMEM((AT_NH, 2 * T // LANES, AV_ROWS, LANES), F32),
                        pltpu.VMEM((AT_NH, 2 * T // LANES, AV_ROWS, LANES), F32)],
        compiler_params=pltpu.CompilerParams(
            dimension_semantics=("parallel", "parallel", "arbitrary"), vmem_limit_bytes=VMEM_LIMIT),
        name="diff_attn",
    )(slopes, proj3, proj3, proj3, q_gain, k_gain, lam_params, head_gain)


MG_TM = 512
MG_HALO = 128
ROUTER_ROWS = 16
ROUTE_ROWS = 8


def _merge_kernel(x_ref, hm_ref, ha_ref, u_ref, halo_ref, gm_ref, ga_ref, gp_ref, pw_ref, ps_ref,
                  wb_ref, wo_ref, fg_ref, *rest, seq_tiles, routed):
    if routed:
        wr_ref, br_ref, xo_ref, hf_ref, route_ref = rest
    else:
        xo_ref, hf_ref = rest
    TM = MG_TM
    i = pl.program_id(0)
    seq_tile = i % seq_tiles

    u = u_ref[...]
    halo = jnp.where(seq_tile > 0, halo_ref[...], jnp.zeros_like(halo_ref))
    u_ext = jnp.concatenate([halo, u], axis=0)
    dist = (lax.broadcasted_iota(I32, (MG_HALO, 2 * MG_HALO), 0) + MG_HALO
            - lax.broadcasted_iota(I32, (MG_HALO, 2 * MG_HALO), 1))
    t_pos = seq_tile * TM + lax.broadcasted_iota(I32, (TM, 1), 0)
    ps = ps_ref[...]
    hp_parts = []
    for g, w in enumerate(P_WINDOWS):
        sl = slice(g * P_GC, (g + 1) * P_GC)
        band = jnp.where((dist >= 0) & (dist < w), 1.0, 0.0).astype(BF16)
        ug = u[:, sl]
        sums = jnp.concatenate([_dot(band, u_ext[r:r + 2 * MG_HALO, sl]) for r in range(0, TM, MG_HALO)],
                               axis=0)
        cnt = jnp.minimum(t_pos + 1, w).astype(F32)
        pooled = sums / cnt - ug.astype(F32)
        hp_parts.append((_dot(pooled.astype(BF16), pw_ref[g]) * ps[:, sl]).astype(BF16))
    hp = jnp.concatenate(hp_parts, axis=1)

    merged = (_sigmoid(gm_ref[...].astype(F32)) * _dot(hm_ref[...], wb_ref[0])
              + _sigmoid(ga_ref[...].astype(F32)) * _dot(ha_ref[...], wb_ref[1])
              + _sigmoid(gp_ref[...].astype(F32)) * _dot(hp, wb_ref[2]))
    x_new = x_ref[...] + _dot(merged.astype(BF16), wo_ref[...])
    xo_ref[...] = x_new
    hf = _rms(x_new, fg_ref[...])
    hf_ref[...] = hf.astype(BF16)

    if routed:
        hf_hi = hf.astype(BF16)
        hf_lo = (hf - hf_hi.astype(F32)).astype(BF16)
        wr = wr_ref[...]
        wr_hi = wr.astype(BF16)
        wr_lo = (wr - wr_hi.astype(F32)).astype(BF16)
        logits = (_dot_nt(wr_hi, hf_hi) + _dot_nt(wr_lo, hf_hi) + _dot_nt(wr_hi, hf_lo))[:N_EXPERTS, :]
        logits = logits + br_ref[...]
        expert = lax.broadcasted_iota(I32, (N_EXPERTS, TM), 0).astype(F32)
        m1 = jnp.max(logits, axis=0, keepdims=True)
        i1 = jnp.min(jnp.where(logits == m1, expert, float(N_EXPERTS)), axis=0, keepdims=True)
        rest_l = jnp.where(expert == i1, NEG_BIG, logits)
        m2 = jnp.max(rest_l, axis=0, keepdims=True)
        i2 = jnp.min(jnp.where(rest_l == m2, expert, float(N_EXPERTS)), axis=0, keepdims=True)
        g1 = 1.0 / (1.0 + jnp.exp(m2 - m1))
        route_ref[...] = jnp.concatenate(
            [g1, 1.0 - g1, i1, i2, jnp.zeros((ROUTE_ROWS - 4, TM), F32)], axis=0)


def _merge(x2, hm2, ha2, proj2, layer, pool_w, pool_scale, w_branch, w_out, ffn_gain, seq, router=None):
    tok = x2.shape[0]
    TM = MG_TM
    routed = router is not None
    hb = TM // MG_HALO
    in_specs = [
        pl.BlockSpec((TM, D_MODEL), lambda i: (i, 0)),
        pl.BlockSpec((TM, BRANCH_WIDTH), lambda i: (i, 0)),
        pl.BlockSpec((TM, BRANCH_WIDTH), lambda i: (i, 0)),
        pl.BlockSpec((TM, BRANCH_WIDTH), lambda i: (i, COL_PU // BRANCH_WIDTH)),
        pl.BlockSpec((MG_HALO, BRANCH_WIDTH), lambda i: (jnp.maximum(i * hb - 1, 0), COL_PU // BRANCH_WIDTH)),
        pl.BlockSpec((TM, D_MODEL), lambda i: (i, COL_G // D_MODEL)),
        pl.BlockSpec((TM, D_MODEL), lambda i: (i, COL_G // D_MODEL + 1)),
        pl.BlockSpec((TM, D_MODEL), lambda i: (i, COL_G // D_MODEL + 2)),
        _layer_spec(layer, (len(P_WINDOWS), P_GC, P_GC)),
        _layer_spec(layer, (1, BRANCH_WIDTH)),
        _layer_spec(layer, (3, BRANCH_WIDTH, D_MODEL)),
        _layer_spec(layer, (D_MODEL, D_MODEL)),
        _layer_spec(layer, (1, D_MODEL)),
    ]
    args = [x2, hm2, ha2, proj2, proj2, proj2, proj2, proj2, pool_w, pool_scale, w_branch, w_out, ffn_gain]
    out_specs = [pl.BlockSpec((TM, D_MODEL), lambda i: (i, 0)),
                 pl.BlockSpec((TM, D_MODEL), lambda i: (i, 0))]
    out_shape = [jax.ShapeDtypeStruct((tok, D_MODEL), F32), jax.ShapeDtypeStruct((tok, D_MODEL), BF16)]
    if routed:
        moe_layer, router_w, router_b = router
        in_specs += [_layer_spec(moe_layer, (ROUTER_ROWS, D_MODEL)), _layer_spec(moe_layer, (N_EXPERTS, 1))]
        args += [router_w, router_b]
        out_specs.append(pl.BlockSpec((ROUTE_ROWS, TM), lambda i: (0, i)))
        out_shape.append(jax.ShapeDtypeStruct((ROUTE_ROWS, tok), F32))
    return pl.pallas_call(
        functools.partial(_merge_kernel, seq_tiles=seq // TM, routed=routed),
        grid=(tok // TM,),
        in_specs=in_specs,
        out_specs=out_specs,
        out_shape=out_shape,
        compiler_params=pltpu.CompilerParams(
            dimension_semantics=("parallel",), vmem_limit_bytes=VMEM_LIMIT),
        name="merge_routed" if routed else "merge",
    )(*args)


def _ple_epilogue(x_new, p, pg_gain, wpg, wpp):
    gate = _sigmoid(_dot(_rms(x_new, pg_gain).astype(BF16), wpg))
    return x_new + gate * _dot(p.astype(BF16), wpp)


FF_TM = 512


def _ffn_kernel(hf_ref, x_ref, p_ref, wgu_ref, wd_ref, pgn_ref, wpg_ref, wpp_ref, out_ref):
    hf = hf_ref[...]
    g = _dot(hf, wgu_ref[:, :D_FF])
    u = _dot(hf, wgu_ref[:, D_FF:])
    y = _dot((g * _sigmoid(g) * u).astype(BF16), wd_ref[...])
    out_ref[...] = _ple_epilogue(x_ref[...] + y, p_ref[...], pgn_ref[...], wpg_ref[...], wpp_ref[...])


def _ffn(hf2, x2, p2, p_row0, ffn_layer, w_gu, w_down, layer, ple_gain, wpg, wpp):
    tok = x2.shape[0]
    TM = FF_TM
    once = pl.Buffered(1)
    return pl.pallas_call(
        _ffn_kernel,
        grid=(tok // TM,),
        in_specs=[
            pl.BlockSpec((TM, D_MODEL), lambda i: (i, 0)),
            pl.BlockSpec((TM, D_MODEL), lambda i: (i, 0)),
            pl.BlockSpec((TM, PLE_DIM), lambda i: (i + p_row0 // TM, 0)),
            _layer_spec(ffn_layer, (D_MODEL, 2 * D_FF), pipeline_mode=once),
            _layer_spec(ffn_layer, (D_FF, D_MODEL), pipeline_mode=once),
            _layer_spec(layer, (1, D_MODEL)),
            _layer_spec(layer, (D_MODEL, D_MODEL), pipeline_mode=once),
            _layer_spec(layer, (PLE_DIM, D_MODEL), pipeline_mode=once),
        ],
        out_specs=pl.BlockSpec((TM, D_MODEL), lambda i: (i, 0)),
        out_shape=jax.ShapeDtypeStruct((tok, D_MODEL), F32),
        compiler_params=pltpu.CompilerParams(
            dimension_semantics=("parallel",), vmem_limit_bytes=VMEM_LIMIT),
        name="ffn_dense",
    )(hf2, x2, p2, w_gu, w_down, ple_gain, wpg, wpp)


MOE_TM = 1024
MOE_TF = 512
DP_TM = 256
DP_CH = 256
CB_TB = 256
CB_ALIGN = 16
CB_W = CB_TB + CB_ALIGN


def _dispatch_kernel(clo_ref, chi_ref, pos_ref, gate_ref, hf_ref, x_ref, g_ref, acc_sc, gacc_sc):
    i = pl.program_id(0)
    rows = i * DP_TM + lax.broadcasted_iota(I32, (DP_TM, 1), 0)
    acc_sc[...] = jnp.zeros_like(acc_sc)
    gacc_sc[...] = jnp.zeros_like(gacc_sc)

    lo = clo_ref[i]
    hi = chi_ref[i]

    def chunk(c, live):
        start = pl.multiple_of(c * DP_CH, DP_CH)
        pos = pos_ref[:, pl.ds(start, DP_CH)]
        gate = gate_ref[:, pl.ds(start, DP_CH)]
        want = jnp.where(live, rows, -1)
        hit1 = pos[0:1, :] == want
        hit2 = pos[1:2, :] == want
        onehot = jnp.where(hit1, 1.0, jnp.where(hit2, 1.0, 0.0)).astype(BF16)
        gates = jnp.sum(jnp.where(hit1, gate[0:1, :], jnp.where(hit2, gate[1:2, :], 0.0)),
                        axis=-1, keepdims=True)
        return _dot(onehot, hf_ref[pl.ds(start, DP_CH), :]), gates

    def pair(t, carry):
        c = lo + 2 * t
        x0, g0 = chunk(c, True)
        x1, g1 = chunk(jnp.minimum(c + 1, hi), c + 1 <= hi)
        acc_sc[...] += x0 + x1
        gacc_sc[...] += g0 + g1
        return carry

    lax.fori_loop(0, (hi - lo + 2) >> 1, pair, 0)
    x_ref[...] = acc_sc[...].astype(BF16)
    g_ref[...] = gacc_sc[...]


def _dispatch(hf2, pos_t, gate_t, chunk_lo, chunk_hi):
    tok = hf2.shape[0]
    ns = chunk_lo.shape[0]
    grid_spec = pltpu.PrefetchScalarGridSpec(
        num_scalar_prefetch=2,
        grid=(ns,),
        in_specs=[
            pl.BlockSpec((2, tok), lambda i, lo, hi: (0, 0)),
            pl.BlockSpec((2, tok), lambda i, lo, hi: (0, 0)),
            pl.BlockSpec((tok, D_MODEL), lambda i, lo, hi: (0, 0), pipeline_mode=pl.Buffered(1)),
        ],
        out_specs=[pl.BlockSpec((DP_TM, D_MODEL), lambda i, lo, hi: (i, 0)),
                   pl.BlockSpec((DP_TM, 1), lambda i, lo, hi: (i, 0))],
        scratch_shapes=[pltpu.VMEM((DP_TM, D_MODEL), F32), pltpu.VMEM((DP_TM, 1), F32)],
    )
    return pl.pallas_call(
        _dispatch_kernel,
        grid_spec=grid_spec,
        out_shape=[jax.ShapeDtypeStruct((ns * DP_TM, D_MODEL), BF16),
                   jax.ShapeDtypeStruct((ns * DP_TM, 1), F32)],
        compiler_params=pltpu.CompilerParams(
            dimension_semantics=("arbitrary",), vmem_limit_bytes=VMEM_LIMIT),
        name="moe_dispatch",
    )(chunk_lo, chunk_hi, pos_t, gate_t, hf2)


def _moe_kernel(te_ref, tr_ref, x_ref, gate_ref, wg_ref, wu_ref, wd_ref, y_ref, acc_sc):
    i = pl.program_id(0)
    j = pl.program_id(1)
    last = pl.num_programs(1) - 1
    rows = tr_ref[i]
    half = MOE_TM // 2

    def step(n):
        @pl.when(j == 0)
        def _():
            acc_sc[:n] = jnp.zeros((n, D_MODEL), F32)

        xb = x_ref[:n]
        g = _dot(xb, wg_ref[...].astype(BF16))
        u = _dot(xb, wu_ref[...].astype(BF16))
        acc_sc[:n] += _dot((g * _sigmoid(g) * u).astype(BF16), wd_ref[...].astype(BF16))

        @pl.when(j == last)
        def _():
            y_ref[:n] = (acc_sc[:n] * gate_ref[:n]).astype(BF16)
            if n < MOE_TM:
                y_ref[n:] = jnp.zeros((MOE_TM - n, D_MODEL), BF16)

    @pl.when(rows > half)
    def _():
        step(MOE_TM)

    @pl.when((rows > 0) & (rows <= half))
    def _():
        step(half)

    @pl.when((rows == 0) & (j == last))
    def _():
        y_ref[...] = jnp.zeros_like(y_ref)


def _moe(x_sorted, gate_sorted, w_gu, w_down, tile_expert, tile_rows):
    TM, TF = MOE_TM, MOE_TF
    nt = tile_expert.shape[0]
    nf = D_FF_EXPERT // TF

    def col(j, tv, i):
        return jnp.where(tv[i] > 0, j, nf - 1)

    grid_spec = pltpu.PrefetchScalarGridSpec(
        num_scalar_prefetch=2,
        grid=(nt, nf),
        in_specs=[
            pl.BlockSpec((TM, D_MODEL), lambda i, j, te, tv: (i, 0)),
            pl.BlockSpec((TM, 1), lambda i, j, te, tv: (i, 0)),
            pl.BlockSpec((None, D_MODEL, TF), lambda i, j, te, tv: (te[i], 0, col(j, tv, i))),
            pl.BlockSpec((None, D_MODEL, TF), lambda i, j, te, tv: (te[i], 0, col(j, tv, i) + nf)),
            pl.BlockSpec((None, TF, D_MODEL), lambda i, j, te, tv: (te[i], col(j, tv, i), 0)),
        ],
        out_specs=pl.BlockSpec((TM, D_MODEL), lambda i, j, te, tv: (i, 0)),
        scratch_shapes=[pltpu.VMEM((TM, D_MODEL), F32)],
    )
    return pl.pallas_call(
        _moe_kernel,
        grid_spec=grid_spec,
        out_shape=jax.ShapeDtypeStruct((nt * TM, D_MODEL), BF16),
        compiler_params=pltpu.CompilerParams(
            dimension_semantics=("arbitrary", "arbitrary"), vmem_limit_bytes=VMEM_LIMIT),
        name="moe_experts",
    )(tile_expert, tile_rows, x_sorted, gate_sorted, w_gu, w_gu, w_down)


def _combine_kernel(w0_ref, rng_ref, x_ref, p_ref, pos_ref, y_hbm, pgn_ref, wpg_ref, wpp_ref, out_ref,
                    win_sc, acc_sc, sem):
    i = pl.program_id(0)
    slot = i & 1

    def window_copy(step, buf, e):
        start = pl.multiple_of(w0_ref[step * N_EXPERTS + e], CB_ALIGN)
        return pltpu.make_async_copy(y_hbm.at[pl.ds(start, CB_W), :], win_sc.at[buf, e], sem.at[buf, e])

    @pl.when(i == 0)
    def _():
        for e in range(N_EXPERTS):
            window_copy(0, 0, e).start()

    @pl.when(i + 1 < pl.num_programs(0))
    def _():
        for e in range(N_EXPERTS):
            window_copy(i + 1, 1 - slot, e).start()

    pos = pos_ref[...]
    pos1 = pos[:, 0:1]
    pos2 = pos[:, 1:2]

    n_win = pl.num_programs(0) * N_EXPERTS

    def row_begin(e):
        return rng_ref[i * N_EXPERTS + e]

    def row_end(e):
        return rng_ref[n_win + i * N_EXPERTS + e]

    def onehot(e, first, width):
        rows = lax.broadcasted_iota(I32, (1, width), 1) + (first + w0_ref[i * N_EXPERTS + e])
        rows = jnp.where((rows >= row_begin(e)) & (rows < row_end(e)), rows, -1)
        return jnp.where(pos1 == rows, 1.0, jnp.where(pos2 == rows, 1.0, 0.0)).astype(BF16)

    moe_out = jnp.zeros((CB_TB, D_MODEL), F32)
    for e in range(N_EXPERTS):
        window_copy(i, slot, e).wait()
        moe_out = moe_out + _dot(onehot(e, 0, CB_TB), win_sc[slot, e, :CB_TB, :])
    acc_sc[...] = moe_out

    for e in range(N_EXPERTS):
        @pl.when(row_end(e) - w0_ref[i * N_EXPERTS + e] > CB_TB)
        def _():
            acc_sc[...] += _dot(onehot(e, CB_TB, CB_ALIGN), win_sc[slot, e, CB_TB:, :])

    out_ref[...] = _ple_epilogue(x_ref[...] + acc_sc[...], p_ref[...], pgn_ref[...], wpg_ref[...],
                                 wpp_ref[...])


def _combine(x2, p2, p_row0, pos, y_sorted, win_start, win_rows, layer, ple_gain, wpg, wpp):
    tok = x2.shape[0]
    TB = CB_TB
    grid_spec = pltpu.PrefetchScalarGridSpec(
        num_scalar_prefetch=2,
        grid=(tok // TB,),
        in_specs=[
            pl.BlockSpec((TB, D_MODEL), lambda i, w0, wt: (i, 0)),
            pl.BlockSpec((TB, PLE_DIM), lambda i, w0, wt: (i + p_row0 // TB, 0)),
            pl.BlockSpec((TB, 2), lambda i, w0, wt: (i, 0)),
            pl.BlockSpec(memory_space=pl.ANY),
            _layer_spec(layer, (1, D_MODEL)),
            _layer_spec(layer, (D_MODEL, D_MODEL)),
            _layer_spec(layer, (PLE_DIM, D_MODEL)),
        ],
        out_specs=pl.BlockSpec((TB, D_MODEL), lambda i, w0, wt: (i, 0)),
        scratch_shapes=[pltpu.VMEM((2, N_EXPERTS, CB_W, D_MODEL), BF16), pltpu.VMEM((TB, D_MODEL), F32),
                        pltpu.SemaphoreType.DMA((2, N_EXPERTS))],
    )
    return pl.pallas_call(
        _combine_kernel,
        grid_spec=grid_spec,
        out_shape=jax.ShapeDtypeStruct((tok, D_MODEL), F32),
        compiler_params=pltpu.CompilerParams(
            dimension_semantics=("arbitrary",), vmem_limit_bytes=VMEM_LIMIT),
        name="moe_combine",
    )(win_start, win_rows, x2, p2, pos, y_sorted, ple_gain, wpg, wpp)


def _route_metadata(route, tok):
    TM = MOE_TM
    nt = (2 * tok) // TM + N_EXPERTS
    rows = nt * TM
    g1, g2 = route[0], route[1]
    i1, i2 = route[2].astype(I32), route[3].astype(I32)
    experts = jnp.arange(N_EXPERTS, dtype=I32)
    member = ((i1[:, None] == experts) | (i2[:, None] == experts)).astype(I32)
    rank = jnp.cumsum(member, axis=0) - member
    counts = jnp.sum(member, axis=0)
    tiles_e = (counts + TM - 1) // TM
    tile_end = jnp.cumsum(tiles_e)
    tile_start = tile_end - tiles_e
    seg_start = tile_start * TM
    sel1 = i1[:, None] == experts
    sel2 = i2[:, None] == experts
    pos1 = jnp.sum(jnp.where(sel1, seg_start[None, :] + rank, 0), axis=1).astype(I32)
    pos2 = jnp.sum(jnp.where(sel2, seg_start[None, :] + rank, 0), axis=1).astype(I32)

    tiles = jnp.arange(nt, dtype=I32)
    n_used = tile_end[-1]
    tile_rows = (tiles < n_used).astype(I32)
    te = jnp.sum((tiles[:, None] >= tile_end[None, :]).astype(I32), axis=1)
    te_last = jnp.sum(((n_used - 1) >= tile_end).astype(I32))
    tile_expert = jnp.where(tile_rows > 0, jnp.minimum(te, N_EXPERTS - 1), te_last).astype(I32)
    mine = tile_expert[:, None] == experts[None, :]
    rows_left = (jnp.sum(jnp.where(mine, counts[None, :], 0), axis=1)
                 - (tiles - jnp.sum(jnp.where(mine, tile_start[None, :], 0), axis=1)) * TM)
    tile_rows = jnp.where(tile_rows > 0, jnp.clip(rows_left, 0, TM), 0).astype(I32)

    steps = jnp.arange(rows // DP_TM, dtype=I32)
    step_tile = steps // (TM // DP_TM)
    step_valid = step_tile < n_used
    step_expert = jnp.minimum(jnp.sum((step_tile[:, None] >= tile_end[None, :]).astype(I32), axis=1),
                              N_EXPERTS - 1)
    pick = step_expert[:, None] == experts[None, :]
    rank_lo = steps * DP_TM - jnp.sum(jnp.where(pick, seg_start[None, :], 0), axis=1)
    rank_hi = jnp.minimum(rank_lo + DP_TM, jnp.sum(jnp.where(pick, counts[None, :], 0), axis=1)) - 1
    before = jnp.sum(jnp.where(pick[:, :, None], rank[::DP_CH, :].T[None, :, :], 0), axis=1)
    chunk_lo = jnp.sum((before <= rank_lo[:, None]).astype(I32), axis=1) - 1
    chunk_hi = jnp.sum((before <= rank_hi[:, None]).astype(I32), axis=1) - 1
    live = step_valid & (rank_hi >= rank_lo)
    chunk_lo = jnp.where(live, chunk_lo, 0).astype(I32)
    chunk_hi = jnp.where(live, chunk_hi, -1).astype(I32)

    blk_start = seg_start[None, :] + rank[::CB_TB, :]
    win_start = jnp.minimum((blk_start // CB_ALIGN) * CB_ALIGN, rows - CB_W).astype(I32)
    blk_count = jnp.concatenate([rank[CB_TB::CB_TB, :], counts[None, :]], axis=0) - rank[::CB_TB, :]
    win_rows = jnp.concatenate([blk_start.reshape(-1), (blk_start + blk_count).reshape(-1)]).astype(I32)
    win_start = win_start.reshape(-1)
    pos = jnp.stack([pos1, pos2], axis=1)
    return (tile_expert, tile_rows, chunk_lo, chunk_hi, jnp.stack([pos1, pos2]), jnp.stack([g1, g2]),
            pos, win_start, win_rows)


def _pack_w_in(w):
    o = [0, 1024, 1536, 2048, 2052, 2056, 2568, 3080, 3592, 4104, 7176]
    main = jnp.concatenate([w[..., o[0]:o[3]], w[..., o[5]:o[10]]], axis=-1).astype(BF16)
    gates = jnp.pad(w[..., o[3]:o[5]], ((0, 0), (0, 0), (0, LANES - 2 * M_HEADS))).astype(BF16)
    return main, gates


def kernel(x, p, attn_norm, w_in, m_conv_w, m_conv_b, m_gate_bias, m_head_norm, a_q_norm, a_k_norm,
           a_lambda, a_head_norm, pool_w, pool_scale, w_branch, w_out, ffn_norm, dense_w_gu, dense_w_down,
           router_w, router_b, moe_w_gu, moe_w_down, ple_norm, ple_w_gate, ple_w_proj):
    b, s, d = x.shape
    depth = w_in.shape[0]
    tok = b * s
    x2 = x.reshape(tok, d)

    def rows(a):
        return a.reshape(a.shape[0], 1, a.shape[1])

    w_main, w_if = _pack_w_in(w_in)
    gate_bias = rows(jnp.pad(m_gate_bias, ((0, 0), (0, LANES - 2 * M_HEADS))))
    q_gain = rows(jnp.tile(a_q_norm, (1, 2)))
    k_gain = rows(jnp.tile(a_k_norm, (1, 2)))
    p2 = p.reshape(depth * tok, PLE_DIM)
    pool_wb, w_branch_b, w_out_b = pool_w.astype(BF16), w_branch.astype(BF16), w_out.astype(BF16)
    ple = (rows(ple_norm), ple_w_gate.astype(BF16), ple_w_proj.astype(BF16))
    dense_gu, dense_down = dense_w_gu.astype(BF16), dense_w_down.astype(BF16)
    router_wt = jnp.pad(jnp.swapaxes(router_w, 1, 2), ((0, 0), (0, ROUTER_ROWS - N_EXPERTS), (0, 0)))
    router_bc = router_b.reshape(router_b.shape[0], N_EXPERTS, 1)

    for layer in range(depth):
        proj, gates = _inproj(x2, layer, rows(attn_norm), w_main, w_if)
        proj3 = proj.reshape(b, s, PROJ_WIDTH)
        h_m = _mlstm(proj3, gates.reshape(b, s, LANES), layer, m_conv_w, rows(m_conv_b), gate_bias,
                     rows(m_head_norm))
        lam_init = 0.8 - 0.6 * math.exp(-0.3 * layer)
        h_a = _attention(proj3, layer, q_gain, k_gain, a_lambda, rows(a_head_norm), lam_init)
        merge_args = (x2, h_m.reshape(tok, BRANCH_WIDTH), h_a.reshape(tok, BRANCH_WIDTH), proj, layer,
                      pool_wb, rows(pool_scale), w_branch_b, w_out_b, rows(ffn_norm), s)
        j = layer // 2
        if layer % 2 == 0:
            x_mid, hf = _merge(*merge_args)
            x2 = _ffn(hf, x_mid, p2, layer * tok, j, dense_gu, dense_down, layer, *ple)
        else:
            x_mid, hf, route = _merge(*merge_args, router=(j, router_wt, router_bc))
            (tile_expert, tile_rows, chunk_lo, chunk_hi, pos_t, gate_t, pos,
             win_start, win_rows) = _route_metadata(route, tok)
            x_sorted, gate_sorted = _dispatch(hf, pos_t, gate_t, chunk_lo, chunk_hi)
            y_sorted = _moe(x_sorted, gate_sorted, moe_w_gu[j], moe_w_down[j], tile_expert, tile_rows)
            x2 = _combine(x_mid, p2, layer * tok, pos, y_sorted, win_start, win_rows, layer, *ple)
    return x2.reshape(b, s, d)
```

```python
import functools
import math

import jax
import jax.numpy as jnp
from jax import lax
from jax.experimental import pallas as pl
from jax.experimental.pallas import tpu as pltpu

F32 = jnp.float32
BF16 = jnp.bfloat16
I32 = jnp.int32

D_MODEL = 1024
PLE_DIM = 256
NORM_EPS = 1e-6
BRANCH_WIDTH = 512

M_HEADS = 4
M_DK = 128
M_CONV = 4
M_CHUNK = 128

A_HEADS = 4
A_DV = 128
A_DQK = 64
ALIBI_SLOPES = tuple(2.0 ** (-8.0 * (h + 1) / A_HEADS) for h in range(A_HEADS))

P_WINDOWS = (2, 4, 8, 16)
P_GC = 128

D_FF = 2816
N_EXPERTS = 8
D_FF_EXPERT = 3584

PROJ_WIDTH = 7168
COL_MQK, COL_MV, COL_MO, COL_AQ, COL_AK, COL_AV, COL_PU, COL_G = 0, 1024, 1536, 2048, 2560, 3072, 3584, 4096

LANES = 128
NEG_BIG = -1e30

V7X_VMEM_BYTES = 64 * 1024 * 1024
VMEM_LIMIT = V7X_VMEM_BYTES - 8 * 1024 * 1024


def _sigmoid(x):
    return 0.5 * jnp.tanh(0.5 * x) + 0.5


def _rms(x, gain):
    return x * lax.rsqrt(jnp.mean(x * x, axis=-1, keepdims=True) + NORM_EPS) * gain


def _dot(a, b):
    return jnp.dot(a, b, preferred_element_type=F32)


def _dot_nt(a, b):
    return lax.dot_general(a, b, (((1,), (1,)), ((), ())), preferred_element_type=F32)


def _dot_tn(a, b):
    return lax.dot_general(a, b, (((0,), (0,)), ((), ())), preferred_element_type=F32)


def _layer_spec(layer, shape, **kw):
    zeros = (0,) * len(shape)
    return pl.BlockSpec((None,) + tuple(shape), lambda *_: (layer,) + zeros, **kw)


IN_TM = 512
IN_TN = 1024


def _inproj_kernel(x_ref, gain_ref, w_ref, wif_ref, out_ref, gates_ref):
    hn = _rms(x_ref[...], gain_ref[...]).astype(BF16)
    gates_ref[...] = _dot(hn, wif_ref[...])
    for n in range(PROJ_WIDTH // IN_TN):
        cols = slice(n * IN_TN, (n + 1) * IN_TN)
        out_ref[:, cols] = _dot(hn, w_ref[:, cols]).astype(BF16)


def _inproj(x2, layer, gain, w_main, w_if):
    tok = x2.shape[0]
    return pl.pallas_call(
        _inproj_kernel,
        grid=(tok // IN_TM,),
        in_specs=[
            pl.BlockSpec((IN_TM, D_MODEL), lambda i: (i, 0)),
            _layer_spec(layer, (1, D_MODEL)),
            _layer_spec(layer, (D_MODEL, PROJ_WIDTH), pipeline_mode=pl.Buffered(1)),
            _layer_spec(layer, (D_MODEL, LANES)),
        ],
        out_specs=[
            pl.BlockSpec((IN_TM, PROJ_WIDTH), lambda i: (i, 0)),
            pl.BlockSpec((IN_TM, LANES), lambda i: (i, 0)),
        ],
        out_shape=[
            jax.ShapeDtypeStruct((tok, PROJ_WIDTH), BF16),
            jax.ShapeDtypeStruct((tok, LANES), F32),
        ],
        compiler_params=pltpu.CompilerParams(
            dimension_semantics=("parallel",), vmem_limit_bytes=VMEM_LIMIT),
        name="inproj",
    )(x2, gain, w_main, w_if)


CONV_TAIL = 16
ML_NB = 2


def _mlstm_kernel(qk_ref, vo_ref, g_ref, cw_ref, cb_ref, gb_ref, hg_ref, out_ref,
                  c_sc, m_sc, prev_sc):
    @pl.when(pl.program_id(1) == 0)
    def _():
        c_sc[...] = jnp.zeros_like(c_sc)
        m_sc[...] = jnp.zeros_like(m_sc)
        prev_sc[...] = jnp.zeros_like(prev_sc)

    L = M_CHUNK
    delay = (lax.broadcasted_iota(I32, (L, CONV_TAIL + L), 0) + CONV_TAIL
             - lax.broadcasted_iota(I32, (L, CONV_TAIL + L), 1))
    shift = {s: jnp.where(delay == s, 1.0, 0.0).astype(BF16) for s in range(1, M_CONV)}
    lane = lax.broadcasted_iota(I32, (L, LANES), 1)
    rowl = lax.broadcasted_iota(I32, (L, LANES), 0)
    causal = lax.broadcasted_iota(I32, (L, L), 0) >= lax.broadcasted_iota(I32, (L, L), 1)
    is_f = (lane >= M_HEADS) & (lane < 2 * M_HEADS)
    cw = cw_ref[...]
    hg = hg_ref[...]
    items = [(bb, h) for bb in range(ML_NB) for h in range(M_HEADS)]

    qk, gc, bcum, gc_t, bcum_t = [], [], [], [], []
    for bb in range(ML_NB):
        xb = qk_ref[bb]
        ext = jnp.concatenate([prev_sc[bb], xb], axis=0)
        y = cb_ref[...] + cw[M_CONV - 1:M_CONV, :] * xb.astype(F32)
        for s in range(1, M_CONV):
            y = y + cw[M_CONV - 1 - s:M_CONV - s, :] * _dot(shift[s], ext)
        prev_sc[bb] = xb[L - CONV_TAIL:]
        qk.append(y * _sigmoid(y))

        g = g_ref[bb] + gb_ref[...]
        logf = jnp.minimum(g, 0.0) - jnp.log(1.0 + jnp.exp(-jnp.abs(g)))
        gcb = jnp.where(is_f, logf, jnp.where(lane < M_HEADS, g, 0.0))
        acc = jnp.where(is_f, logf, 0.0)
        k = 1
        while k < L:
            acc = acc + jnp.where(rowl >= k, pltpu.roll(acc, k, 0), 0.0)
            k *= 2
        gc.append(gcb)
        bcum.append(acc)
        gc_t.append(gcb.T)
        bcum_t.append(acc.T)

    def each(fn):
        return {it: fn(*it) for it in items}

    def head(h):
        return slice(h * M_DK, (h + 1) * M_DK)

    b_col = each(lambda bb, h: bcum[bb][:, M_HEADS + h:M_HEADS + h + 1])
    i_col = each(lambda bb, h: gc[bb][:, h:h + 1])
    m_prev = each(lambda bb, h: m_sc[bb, h:h + 1, 0:1])
    log_d = each(lambda bb, h: jnp.where(
        causal, b_col[bb, h] - bcum_t[bb][M_HEADS + h:M_HEADS + h + 1, :] + gc_t[bb][h:h + 1, :], NEG_BIG))
    inter = each(lambda bb, h: b_col[bb, h] + m_prev[bb, h])
    m_t = each(lambda bb, h: jnp.maximum(jnp.max(log_d[bb, h], axis=-1, keepdims=True), inter[bb, h]))
    d_w = each(lambda bb, h: jnp.exp(log_d[bb, h] - m_t[bb, h]))
    w_inter = each(lambda bb, h: jnp.exp(inter[bb, h] - m_t[bb, h]))

    kh = each(lambda bb, h: qk[bb][:, BRANCH_WIDTH + h * M_DK:BRANCH_WIDTH + (h + 1) * M_DK] * (M_DK ** -0.5))
    qb = each(lambda bb, h: qk[bb][:, head(h)].astype(BF16))
    ones = jnp.ones((L, M_DK), BF16)
    v_aug = each(lambda bb, h: jnp.concatenate([vo_ref[bb, :, head(h)], ones], axis=1))
    sc = each(lambda bb, h: (_dot_nt(qb[bb, h], kh[bb, h].astype(BF16)) * d_w[bb, h]).astype(BF16))

    intra = each(lambda bb, h: _dot(sc[bb, h], v_aug[bb, h]))
    carried = each(lambda bb, h: _dot(qb[bb, h], c_sc[bb, h].astype(BF16)))
    both = each(lambda bb, h: intra[bb, h] + w_inter[bb, h] * carried[bb, h])
    floor = each(lambda bb, h: jnp.exp(-m_t[bb, h]))
    hh = each(lambda bb, h: both[bb, h][:, :M_DK] / jnp.maximum(jnp.abs(both[bb, h][:, M_DK:]), floor[bb, h]))
    ms = each(lambda bb, h: jnp.mean(hh[bb, h] * hh[bb, h], axis=-1, keepdims=True))
    for bb, h in items:
        hn = hh[bb, h] * lax.rsqrt(ms[bb, h] + NORM_EPS) * hg[:, head(h)]
        o_pre = vo_ref[bb, :, BRANCH_WIDTH + h * M_DK:BRANCH_WIDTH + (h + 1) * M_DK].astype(F32)
        out_ref[bb, :, head(h)] = (hn * _sigmoid(o_pre)).astype(BF16)

    b_last = each(lambda bb, h: b_col[bb, h][L - 1:L, :])
    log_w = each(lambda bb, h: b_last[bb, h] - b_col[bb, h] + i_col[bb, h])
    m_new = each(lambda bb, h: jnp.maximum(b_last[bb, h] + m_prev[bb, h],
                                           jnp.max(log_w[bb, h], axis=0, keepdims=True)))
    decay = each(lambda bb, h: jnp.exp(b_last[bb, h] + m_prev[bb, h] - m_new[bb, h]))
    kw = each(lambda bb, h: (kh[bb, h] * jnp.exp(log_w[bb, h] - m_new[bb, h])).astype(BF16))
    upd = each(lambda bb, h: _dot_tn(kw[bb, h], v_aug[bb, h]))
    for bb, h in items:
        c_sc[bb, h] = decay[bb, h] * c_sc[bb, h] + upd[bb, h]
        m_sc[bb, h:h + 1, :] = jnp.broadcast_to(m_new[bb, h], (1, LANES))


def _mlstm(proj3, gates3, layer, conv_w, conv_b, gate_bias, head_gain):
    b, s, _ = proj3.shape
    L = M_CHUNK
    return pl.pallas_call(
        _mlstm_kernel,
        grid=(b // ML_NB, s // L),
        in_specs=[
            pl.BlockSpec((ML_NB, L, 1024), lambda i, c: (i, c, COL_MQK // 1024)),
            pl.BlockSpec((ML_NB, L, 1024), lambda i, c: (i, c, COL_MV // 1024)),
            pl.BlockSpec((ML_NB, L, LANES), lambda i, c: (i, c, 0)),
            _layer_spec(layer, (M_CONV, 1024)),
            _layer_spec(layer, (1, 1024)),
            _layer_spec(layer, (1, LANES)),
            _layer_spec(layer, (1, BRANCH_WIDTH)),
        ],
        out_specs=pl.BlockSpec((ML_NB, L, BRANCH_WIDTH), lambda i, c: (i, c, 0)),
        out_shape=jax.ShapeDtypeStruct((b, s, BRANCH_WIDTH), BF16),
        scratch_shapes=[
            pltpu.VMEM((ML_NB, M_HEADS, M_DK, 2 * M_DK), F32),
            pltpu.VMEM((ML_NB, 8, LANES), F32),
            pltpu.VMEM((ML_NB, CONV_TAIL, 1024), BF16),
        ],
        compiler_params=pltpu.CompilerParams(
            dimension_semantics=("parallel", "arbitrary"), vmem_limit_bytes=VMEM_LIMIT),
        name="mlstm",
    )(proj3, proj3, gates3, conv_w, conv_b, gate_bias, head_gain)


AT_T = 256
ALIBI_BASE = 256
AT_TK = 512


def _halfnorm(x, gain):
    lane = lax.broadcasted_iota(I32, (1, LANES), 1)
    lo = lane < A_DQK
    x2 = x * x
    ms_lo = jnp.sum(jnp.where(lo, x2, 0.0), axis=-1, keepdims=True)
    ms_hi = jnp.sum(jnp.where(lo, 0.0, x2), axis=-1, keepdims=True)
    ms = jnp.where(lo, ms_lo, ms_hi) * (1.0 / A_DQK)
    return x * lax.rsqrt(ms + NORM_EPS) * gain


AV_ROWS = A_DV + 16


AT_NH = 2


def _attn_kernel(slopes_ref, q_ref, k_ref, v_ref, qg_ref, kg_ref, lam_ref, hg_ref, out_ref,
                 k1_sc, k2_sc, vt_sc, s_sc, p_sc, acc_sc, res_sc, *, lam_init):
    T = AT_T
    TK = AT_TK
    CH = 128
    heads = range(AT_NH)
    qi = pl.program_id(2)
    lane = lax.broadcasted_iota(I32, (1, LANES), 1)
    slope = [slopes_ref[pl.program_id(1) * AT_NH + hd] for hd in heads]

    def cols(hd):
        return slice(hd * LANES, (hd + 1) * LANES)

    @pl.when(qi == 0)
    def _():
        s_len = k_ref.shape[0]
        pos = lax.broadcasted_iota(I32, (s_len, 1), 0)
        for hd in heads:
            kn = _halfnorm(k_ref[:, cols(hd)].astype(F32), kg_ref[...])
            c_loc = (pos & (ALIBI_BASE - 1)).astype(F32) * slope[hd]
            c_blk = (pos >> int(math.log2(ALIBI_BASE))).astype(F32) * (slope[hd] * ALIBI_BASE)
            k1_sc[hd] = jnp.where(lane < A_DQK, kn, jnp.where(lane == A_DQK, c_loc, jnp.where(
                lane == A_DQK + 1, c_blk, 0.0))).astype(BF16)
            k2_sc[hd] = jnp.where(lane >= A_DQK, kn, jnp.where(lane == 0, c_loc, jnp.where(
                lane == 1, c_blk, 0.0))).astype(BF16)
            vt_sc[hd, :A_DV, :] = v_ref[:, cols(hd)].astype(F32).T.astype(BF16)
            vt_sc[hd, A_DV:, :] = jnp.ones((AV_ROWS - A_DV, s_len), BF16)
        res_sc[...] = jnp.ones_like(res_sc)

    lp = lam_ref[...]
    lam = (jnp.exp(jnp.sum(lp[0:1, :] * lp[1:2, :], axis=-1, keepdims=True))
           - jnp.exp(jnp.sum(lp[2:3, :] * lp[3:4, :], axis=-1, keepdims=True)) + lam_init)

    def finish(src, qblock):
        row0 = pl.multiple_of(qblock * T, T)
        hg = hg_ref[...]
        for hd in heads:
            per_map = T // LANES
            a1 = jnp.concatenate([src[hd, i] for i in range(per_map)], axis=1)
            a2 = jnp.concatenate([src[hd, per_map + i] for i in range(per_map)], axis=1)
            o_t = a1[:A_DV, :] / a1[A_DV:A_DV + 1, :] - lam * (a2[:A_DV, :] / a2[A_DV:A_DV + 1, :])
            out_ref[pl.ds(row0, T), cols(hd)] = (
                _rms(o_t.T, hg[:, cols(hd)]) * (1.0 - lam_init)).astype(BF16)

    finish(res_sc, jnp.maximum(qi - 1, 0))

    qn = [_halfnorm(q_ref[:, cols(hd)].astype(F32), qg_ref[...]) * (A_DQK ** -0.5) for hd in heads]
    q1 = [jnp.where(lane < A_DQK, qn[hd], jnp.where(lane < A_DQK + 2, 1.0, 0.0)).astype(BF16) for hd in heads]
    q2 = [jnp.where(lane >= A_DQK, qn[hd], jnp.where(lane < 2, 1.0, 0.0)).astype(BF16) for hd in heads]

    def scores_to_scratch(hd, j):
        start = pl.multiple_of(j * TK, TK)
        for mp, (k_sc, q) in enumerate(((k1_sc, q1), (k2_sc, q2))):
            sc = _dot_nt(k_sc[hd, pl.ds(start, TK), :], q[hd])
            for hh in range(T // LANES):
                s_sc[hd, mp, hh] = sc[:, hh * LANES:(hh + 1) * LANES]

    def causal_bias():
        first = (qi * T - last * TK) >> int(math.log2(CH))
        tri = jnp.where(lax.broadcasted_iota(I32, (CH, LANES), 0) <= lax.broadcasted_iota(I32, (CH, LANES), 1),
                        0.0, NEG_BIG)
        bias = {}
        for c in range(TK // CH):
            for hh in range(T // LANES):
                ahead = first + hh - c
                bias[c, hh] = jnp.where(ahead > 0, 0.0, jnp.where(ahead == 0, tri, NEG_BIG))
        return bias

    def softmax_to_scratch(hd, j, m, bias, keep=None):
        m_out, alpha_out = [], []
        for mp in range(2):
            for hh in range(T // LANES):
                idx = mp * (T // LANES) + hh

                def chunk(c):
                    sc = s_sc[hd, mp, hh, c * CH:(c + 1) * CH, :]
                    return sc if bias is None else sc + bias[c, hh]

                m_new = m[idx]
                for c in range(TK // CH):
                    m_new = jnp.maximum(m_new, jnp.max(chunk(c), axis=0, keepdims=True))
                for c in range(TK // CH):
                    p = jnp.exp(chunk(c) - m_new).astype(BF16)
                    if keep is None:
                        p_sc[hd, idx, c * CH:(c + 1) * CH, :] = p
                    else:
                        keep[c, idx] = p
                alpha_out.append(jnp.exp(m[idx] - m_new))
                m_out.append(m_new)
        return tuple(m_out), tuple(alpha_out)

    def values_from_scratch(hd, j, alpha, p=None):
        vt = vt_sc[hd, :, pl.ds(pl.multiple_of(j * TK, TK), TK)]
        if p is None:
            p = jnp.concatenate([p_sc[hd, idx] for idx in range(2 * T // LANES)], axis=1)
        pv = _dot(vt, p)
        for idx in range(2 * T // LANES):
            acc_sc[hd, idx] = alpha[idx] * acc_sc[hd, idx] + pv[:, idx * LANES:(idx + 1) * LANES]

    def body(j, carry):
        m, alpha = carry
        for hd in heads:
            values_from_scratch(hd, jnp.maximum(j - 1, 0), alpha[hd])
        stats = [softmax_to_scratch(hd, j, m[hd], None) for hd in heads]
        for hd in heads:
            scores_to_scratch(hd, j + 1)
        return tuple(st[0] for st in stats), tuple(st[1] for st in stats)

    n_slab = 2 * T // LANES
    last = (qi * T) >> int(math.log2(TK))
    p_sc[...] = jnp.zeros_like(p_sc)
    acc_sc[...] = jnp.zeros_like(acc_sc)
    for hd in heads:
        scores_to_scratch(hd, 0)
    m, alpha = lax.fori_loop(0, last, body, (((jnp.full((1, LANES), NEG_BIG, F32),) * n_slab,) * AT_NH,
                                             ((jnp.ones((1, LANES), F32),) * n_slab,) * AT_NH))
    for hd in heads:
        values_from_scratch(hd, jnp.maximum(last - 1, 0), alpha[hd])
    bias = causal_bias()
    kept = [dict() for _ in heads]
    alpha = [softmax_to_scratch(hd, last, m[hd], bias, kept[hd])[1] for hd in heads]
    for hd in heads:
        p_last = jnp.concatenate([jnp.concatenate([kept[hd][c, idx] for c in range(TK // CH)], axis=0)
                                  for idx in range(n_slab)], axis=1)
        values_from_scratch(hd, last, alpha[hd], p_last)

    res_sc[...] = acc_sc[...]

    @pl.when(qi == pl.num_programs(2) - 1)
    def _():
        finish(acc_sc, qi)


def _attention(proj3, layer, q_gain, k_gain, lam_params, head_gain, lam_init):
    b, s, _ = proj3.shape
    T = AT_T
    W = AT_NH * LANES
    slopes = jnp.asarray(ALIBI_SLOPES, F32)
    return pl.pallas_call(
        functools.partial(_attn_kernel, lam_init=lam_init),
        grid=(b, A_HEADS // AT_NH, s // T),
        in_specs=[
            pl.BlockSpec(memory_space=pltpu.SMEM),
            pl.BlockSpec((None, T, W), lambda i, h, q: (i, q, COL_AQ // W + h)),
            pl.BlockSpec((None, s, W), lambda i, h, q: (i, 0, COL_AK // W + h)),
            pl.BlockSpec((None, s, W), lambda i, h, q: (i, 0, COL_AV // W + h)),
            _layer_spec(layer, (1, LANES)),
            _layer_spec(layer, (1, LANES)),
            _layer_spec(layer, (4, A_DQK)),
            pl.BlockSpec((None, 1, W), lambda i, h, q: (layer, 0, h)),
        ],
        out_specs=pl.BlockSpec((None, s, W), lambda i, h, q: (i, 0, h)),
        out_shape=jax.ShapeDtypeStruct((b, s, BRANCH_WIDTH), BF16),
        scratch_shapes=[pltpu.VMEM((AT_NH, s, LANES), BF16), pltpu.VMEM((AT_NH, s, LANES), BF16),
                        pltpu.VMEM((AT_NH, AV_ROWS, s), BF16),
                        pltpu.VMEM((AT_NH, 2, T // LANES, AT_TK, LANES), F32),
                        pltpu.VMEM((AT_NH, 2 * T // LANES, AT_TK, LANES), BF16),
                        pltpu.VMEM((AT_NH, 2 * T // LANES, AV_ROWS, LANES), F32),
                        pltpu.VMEM((AT_NH, 2 * T // LANES, AV_ROWS, LANES), F32)],
        compiler_params=pltpu.CompilerParams(
            dimension_semantics=("parallel", "parallel", "arbitrary"), vmem_limit_bytes=VMEM_LIMIT),
        name="diff_attn",
    )(slopes, proj3, proj3, proj3, q_gain, k_gain, lam_params, head_gain)


MG_TM = 512
MG_HALO = 128
ROUTER_ROWS = 16
ROUTE_ROWS = 8


def _merge_kernel(x_ref, hm_ref, ha_ref, u_ref, halo_ref, gm_ref, ga_ref, gp_ref, pw_ref, ps_ref,
                  wb_ref, wo_ref, fg_ref, *rest, seq_tiles, routed):
    if routed:
        wr_ref, br_ref, xo_ref, hf_ref, route_ref = rest
    else:
        xo_ref, hf_ref = rest
    TM = MG_TM
    i = pl.program_id(0)
    seq_tile = i % seq_tiles

    u = u_ref[...]
    halo = jnp.where(seq_tile > 0, halo_ref[...], jnp.zeros_like(halo_ref))
    u_ext = jnp.concatenate([halo, u], axis=0)
    dist = (lax.broadcasted_iota(I32, (MG_HALO, 2 * MG_HALO), 0) + MG_HALO
            - lax.broadcasted_iota(I32, (MG_HALO, 2 * MG_HALO), 1))
    t_pos = seq_tile * TM + lax.broadcasted_iota(I32, (TM, 1), 0)
    ps = ps_ref[...]
    hp_parts = []
    for g, w in enumerate(P_WINDOWS):
        sl = slice(g * P_GC, (g + 1) * P_GC)
        band = jnp.where((dist >= 0) & (dist < w), 1.0, 0.0).astype(BF16)
        ug = u[:, sl]
        sums = jnp.concatenate([_dot(band, u_ext[r:r + 2 * MG_HALO, sl]) for r in range(0, TM, MG_HALO)],
                               axis=0)
        cnt = jnp.minimum(t_pos + 1, w).astype(F32)
        pooled = sums / cnt - ug.astype(F32)
        hp_parts.append((_dot(pooled.astype(BF16), pw_ref[g]) * ps[:, sl]).astype(BF16))
    hp = jnp.concatenate(hp_parts, axis=1)

    merged = (_sigmoid(gm_ref[...].astype(F32)) * _dot(hm_ref[...], wb_ref[0])
              + _sigmoid(ga_ref[...].astype(F32)) * _dot(ha_ref[...], wb_ref[1])
              + _sigmoid(gp_ref[...].astype(F32)) * _dot(hp, wb_ref[2]))
    x_new = x_ref[...] + _dot(merged.astype(BF16), wo_ref[...])
    xo_ref[...] = x_new
    hf = _rms(x_new, fg_ref[...])
    hf_ref[...] = hf.astype(BF16)

    if routed:
        hf_hi = hf.astype(BF16)
        hf_lo = (hf - hf_hi.astype(F32)).astype(BF16)
        wr = wr_ref[...]
        wr_hi = wr.astype(BF16)
        wr_lo = (wr - wr_hi.astype(F32)).astype(BF16)
        logits = (_dot_nt(wr_hi, hf_hi) + _dot_nt(wr_lo, hf_hi) + _dot_nt(wr_hi, hf_lo))[:N_EXPERTS, :]
        logits = logits + br_ref[...]
        expert = lax.broadcasted_iota(I32, (N_EXPERTS, TM), 0).astype(F32)
        m1 = jnp.max(logits, axis=0, keepdims=True)
        i1 = jnp.min(jnp.where(logits == m1, expert, float(N_EXPERTS)), axis=0, keepdims=True)
        rest_l = jnp.where(expert == i1, NEG_BIG, logits)
        m2 = jnp.max(rest_l, axis=0, keepdims=True)
        i2 = jnp.min(jnp.where(rest_l == m2, expert, float(N_EXPERTS)), axis=0, keepdims=True)
        g1 = 1.0 / (1.0 + jnp.exp(m2 - m1))
        route_ref[...] = jnp.concatenate(
            [g1, 1.0 - g1, i1, i2, jnp.zeros((ROUTE_ROWS - 4, TM), F32)], axis=0)


def _merge(x2, hm2, ha2, proj2, layer, pool_w, pool_scale, w_branch, w_out, ffn_gain, seq, router=None):
    tok = x2.shape[0]
    TM = MG_TM
    routed = router is not None
    hb = TM // MG_HALO
    in_specs = [
        pl.BlockSpec((TM, D_MODEL), lambda i: (i, 0)),
        pl.BlockSpec((TM, BRANCH_WIDTH), lambda i: (i, 0)),
        pl.BlockSpec((TM, BRANCH_WIDTH), lambda i: (i, 0)),
        pl.BlockSpec((TM, BRANCH_WIDTH), lambda i: (i, COL_PU // BRANCH_WIDTH)),
        pl.BlockSpec((MG_HALO, BRANCH_WIDTH), lambda i: (jnp.maximum(i * hb - 1, 0), COL_PU // BRANCH_WIDTH)),
        pl.BlockSpec((TM, D_MODEL), lambda i: (i, COL_G // D_MODEL)),
        pl.BlockSpec((TM, D_MODEL), lambda i: (i, COL_G // D_MODEL + 1)),
        pl.BlockSpec((TM, D_MODEL), lambda i: (i, COL_G // D_MODEL + 2)),
        _layer_spec(layer, (len(P_WINDOWS), P_GC, P_GC)),
        _layer_spec(layer, (1, BRANCH_WIDTH)),
        _layer_spec(layer, (3, BRANCH_WIDTH, D_MODEL)),
        _layer_spec(layer, (D_MODEL, D_MODEL)),
        _layer_spec(layer, (1, D_MODEL)),
    ]
    args = [x2, hm2, ha2, proj2, proj2, proj2, proj2, proj2, pool_w, pool_scale, w_branch, w_out, ffn_gain]
    out_specs = [pl.BlockSpec((TM, D_MODEL), lambda i: (i, 0)),
                 pl.BlockSpec((TM, D_MODEL), lambda i: (i, 0))]
    out_shape = [jax.ShapeDtypeStruct((tok, D_MODEL), F32), jax.ShapeDtypeStruct((tok, D_MODEL), BF16)]
    if routed:
        moe_layer, router_w, router_b = router
        in_specs += [_layer_spec(moe_layer, (ROUTER_ROWS, D_MODEL)), _layer_spec(moe_layer, (N_EXPERTS, 1))]
        args += [router_w, router_b]
        out_specs.append(pl.BlockSpec((ROUTE_ROWS, TM), lambda i: (0, i)))
        out_shape.append(jax.ShapeDtypeStruct((ROUTE_ROWS, tok), F32))
    return pl.pallas_call(
        functools.partial(_merge_kernel, seq_tiles=seq // TM, routed=routed),
        grid=(tok // TM,),
        in_specs=in_specs,
        out_specs=out_specs,
        out_shape=out_shape,
        compiler_params=pltpu.CompilerParams(
            dimension_semantics=("parallel",), vmem_limit_bytes=VMEM_LIMIT),
        name="merge_routed" if routed else "merge",
    )(*args)


def _ple_epilogue(x_new, p, pg_gain, wpg, wpp):
    gate = _sigmoid(_dot(_rms(x_new, pg_gain).astype(BF16), wpg))
    return x_new + gate * _dot(p.astype(BF16), wpp)


FF_TM = 512


def _ffn_kernel(hf_ref, x_ref, p_ref, wgu_ref, wd_ref, pgn_ref, wpg_ref, wpp_ref, out_ref):
    hf = hf_ref[...]
    g = _dot(hf, wgu_ref[:, :D_FF])
    u = _dot(hf, wgu_ref[:, D_FF:])
    y = _dot((g * _sigmoid(g) * u).astype(BF16), wd_ref[...])
    out_ref[...] = _ple_epilogue(x_ref[...] + y, p_ref[...], pgn_ref[...], wpg_ref[...], wpp_ref[...])


def _ffn(hf2, x2, p2, p_row0, ffn_layer, w_gu, w_down, layer, ple_gain, wpg, wpp):
    tok = x2.shape[0]
    TM = FF_TM
    once = pl.Buffered(1)
    return pl.pallas_call(
        _ffn_kernel,
        grid=(tok // TM,),
        in_specs=[
            pl.BlockSpec((TM, D_MODEL), lambda i: (i, 0)),
            pl.BlockSpec((TM, D_MODEL), lambda i: (i, 0)),
            pl.BlockSpec((TM, PLE_DIM), lambda i: (i + p_row0 // TM, 0)),
            _layer_spec(ffn_layer, (D_MODEL, 2 * D_FF), pipeline_mode=once),
            _layer_spec(ffn_layer, (D_FF, D_MODEL), pipeline_mode=once),
            _layer_spec(layer, (1, D_MODEL)),
            _layer_spec(layer, (D_MODEL, D_MODEL), pipeline_mode=once),
            _layer_spec(layer, (PLE_DIM, D_MODEL), pipeline_mode=once),
        ],
        out_specs=pl.BlockSpec((TM, D_MODEL), lambda i: (i, 0)),
        out_shape=jax.ShapeDtypeStruct((tok, D_MODEL), F32),
        compiler_params=pltpu.CompilerParams(
            dimension_semantics=("parallel",), vmem_limit_bytes=VMEM_LIMIT),
        name="ffn_dense",
    )(hf2, x2, p2, w_gu, w_down, ple_gain, wpg, wpp)


MOE_TM = 1024
MOE_TF = 512
DP_TM = 256
DP_CH = 256
CB_TB = 256
CB_ALIGN = 16
CB_W = CB_TB + CB_ALIGN


def _dispatch_kernel(clo_ref, chi_ref, pos_ref, gate_ref, hf_ref, x_ref, g_ref, acc_sc, gacc_sc):
    i = pl.program_id(0)
    rows = i * DP_TM + lax.broadcasted_iota(I32, (DP_TM, 1), 0)
    acc_sc[...] = jnp.zeros_like(acc_sc)
    gacc_sc[...] = jnp.zeros_like(gacc_sc)

    lo = clo_ref[i]
    hi = chi_ref[i]

    def chunk(c, live):
        start = pl.multiple_of(c * DP_CH, DP_CH)
        pos = pos_ref[:, pl.ds(start, DP_CH)]
        gate = gate_ref[:, pl.ds(start, DP_CH)]
        want = jnp.where(live, rows, -1)
        hit1 = pos[0:1, :] == want
        hit2 = pos[1:2, :] == want
        onehot = jnp.where(hit1, 1.0, jnp.where(hit2, 1.0, 0.0)).astype(BF16)
        gates = jnp.sum(jnp.where(hit1, gate[0:1, :], jnp.where(hit2, gate[1:2, :], 0.0)),
                        axis=-1, keepdims=True)
        return _dot(onehot, hf_ref[pl.ds(start, DP_CH), :]), gates

    def pair(t, carry):
        c = lo + 2 * t
        x0, g0 = chunk(c, True)
        x1, g1 = chunk(jnp.minimum(c + 1, hi), c + 1 <= hi)
        acc_sc[...] += x0 + x1
        gacc_sc[...] += g0 + g1
        return carry

    lax.fori_loop(0, (hi - lo + 2) >> 1, pair, 0)
    x_ref[...] = acc_sc[...].astype(BF16)
    g_ref[...] = gacc_sc[...]


def _dispatch(hf2, pos_t, gate_t, chunk_lo, chunk_hi):
    tok = hf2.shape[0]
    ns = chunk_lo.shape[0]
    grid_spec = pltpu.PrefetchScalarGridSpec(
        num_scalar_prefetch=2,
        grid=(ns,),
        in_specs=[
            pl.BlockSpec((2, tok), lambda i, lo, hi: (0, 0)),
            pl.BlockSpec((2, tok), lambda i, lo, hi: (0, 0)),
            pl.BlockSpec((tok, D_MODEL), lambda i, lo, hi: (0, 0), pipeline_mode=pl.Buffered(1)),
        ],
        out_specs=[pl.BlockSpec((DP_TM, D_MODEL), lambda i, lo, hi: (i, 0)),
                   pl.BlockSpec((DP_TM, 1), lambda i, lo, hi: (i, 0))],
        scratch_shapes=[pltpu.VMEM((DP_TM, D_MODEL), F32), pltpu.VMEM((DP_TM, 1), F32)],
    )
    return pl.pallas_call(
        _dispatch_kernel,
        grid_spec=grid_spec,
        out_shape=[jax.ShapeDtypeStruct((ns * DP_TM, D_MODEL), BF16),
                   jax.ShapeDtypeStruct((ns * DP_TM, 1), F32)],
        compiler_params=pltpu.CompilerParams(
            dimension_semantics=("arbitrary",), vmem_limit_bytes=VMEM_LIMIT),
        name="moe_dispatch",
    )(chunk_lo, chunk_hi, pos_t, gate_t, hf2)


def _moe_kernel(te_ref, tr_ref, x_ref, gate_ref, wg_ref, wu_ref, wd_ref, y_ref, acc_sc):
    i = pl.program_id(0)
    j = pl.program_id(1)
    last = pl.num_programs(1) - 1
    rows = tr_ref[i]
    half = MOE_TM // 2

    def step(n):
        @pl.when(j == 0)
        def _():
            acc_sc[:n] = jnp.zeros((n, D_MODEL), F32)

        xb = x_ref[:n]
        g = _dot(xb, wg_ref[...].astype(BF16))
        u = _dot(xb, wu_ref[...].astype(BF16))
        acc_sc[:n] += _dot((g * _sigmoid(g) * u).astype(BF16), wd_ref[...].astype(BF16))

        @pl.when(j == last)
        def _():
            y_ref[:n] = (acc_sc[:n] * gate_ref[:n]).astype(BF16)
            if n < MOE_TM:
                y_ref[n:] = jnp.zeros((MOE_TM - n, D_MODEL), BF16)

    @pl.when(rows > half)
    def _():
        step(MOE_TM)

    @pl.when((rows > 0) & (rows <= half))
    def _():
        step(half)

    @pl.when((rows == 0) & (j == last))
    def _():
        y_ref[...] = jnp.zeros_like(y_ref)


def _moe(x_sorted, gate_sorted, w_gu, w_down, tile_expert, tile_rows):
    TM, TF = MOE_TM, MOE_TF
    nt = tile_expert.shape[0]
    nf = D_FF_EXPERT // TF

    def col(j, tv, i):
        return jnp.where(tv[i] > 0, j, nf - 1)

    grid_spec = pltpu.PrefetchScalarGridSpec(
        num_scalar_prefetch=2,
        grid=(nt, nf),
        in_specs=[
            pl.BlockSpec((TM, D_MODEL), lambda i, j, te, tv: (i, 0)),
            pl.BlockSpec((TM, 1), lambda i, j, te, tv: (i, 0)),
            pl.BlockSpec((None, D_MODEL, TF), lambda i, j, te, tv: (te[i], 0, col(j, tv, i))),
            pl.BlockSpec((None, D_MODEL, TF), lambda i, j, te, tv: (te[i], 0, col(j, tv, i) + nf)),
            pl.BlockSpec((None, TF, D_MODEL), lambda i, j, te, tv: (te[i], col(j, tv, i), 0)),
        ],
        out_specs=pl.BlockSpec((TM, D_MODEL), lambda i, j, te, tv: (i, 0)),
        scratch_shapes=[pltpu.VMEM((TM, D_MODEL), F32)],
    )
    return pl.pallas_call(
        _moe_kernel,
        grid_spec=grid_spec,
        out_shape=jax.ShapeDtypeStruct((nt * TM, D_MODEL), BF16),
        compiler_params=pltpu.CompilerParams(
            dimension_semantics=("arbitrary", "arbitrary"), vmem_limit_bytes=VMEM_LIMIT),
        name="moe_experts",
    )(tile_expert, tile_rows, x_sorted, gate_sorted, w_gu, w_gu, w_down)


def _combine_kernel(w0_ref, rng_ref, x_ref, p_ref, pos_ref, y_hbm, pgn_ref, wpg_ref, wpp_ref, out_ref,
                    win_sc, acc_sc, sem):
    i = pl.program_id(0)
    slot = i & 1

    def window_copy(step, buf, e):
        start = pl.multiple_of(w0_ref[step * N_EXPERTS + e], CB_ALIGN)
        return pltpu.make_async_copy(y_hbm.at[pl.ds(start, CB_W), :], win_sc.at[buf, e], sem.at[buf, e])

    @pl.when(i == 0)
    def _():
        for e in range(N_EXPERTS):
            window_copy(0, 0, e).start()

    @pl.when(i + 1 < pl.num_programs(0))
    def _():
        for e in range(N_EXPERTS):
            window_copy(i + 1, 1 - slot, e).start()

    pos = pos_ref[...]
    pos1 = pos[:, 0:1]
    pos2 = pos[:, 1:2]

    n_win = pl.num_programs(0) * N_EXPERTS

    def row_begin(e):
        return rng_ref[i * N_EXPERTS + e]

    def row_end(e):
        return rng_ref[n_win + i * N_EXPERTS + e]

    def onehot(e, first, width):
        rows = lax.broadcasted_iota(I32, (1, width), 1) + (first + w0_ref[i * N_EXPERTS + e])
        rows = jnp.where((rows >= row_begin(e)) & (rows < row_end(e)), rows, -1)
        return jnp.where(pos1 == rows, 1.0, jnp.where(pos2 == rows, 1.0, 0.0)).astype(BF16)

    moe_out = jnp.zeros((CB_TB, D_MODEL), F32)
    for e in range(N_EXPERTS):
        window_copy(i, slot, e).wait()
        moe_out = moe_out + _dot(onehot(e, 0, CB_TB), win_sc[slot, e, :CB_TB, :])
    acc_sc[...] = moe_out

    for e in range(N_EXPERTS):
        @pl.when(row_end(e) - w0_ref[i * N_EXPERTS + e] > CB_TB)
        def _():
            acc_sc[...] += _dot(onehot(e, CB_TB, CB_ALIGN), win_sc[slot, e, CB_TB:, :])

    out_ref[...] = _ple_epilogue(x_ref[...] + acc_sc[...], p_ref[...], pgn_ref[...], wpg_ref[...],
                                 wpp_ref[...])


def _combine(x2, p2, p_row0, pos, y_sorted, win_start, win_rows, layer, ple_gain, wpg, wpp):
    tok = x2.shape[0]
    TB = CB_TB
    grid_spec = pltpu.PrefetchScalarGridSpec(
        num_scalar_prefetch=2,
        grid=(tok // TB,),
        in_specs=[
            pl.BlockSpec((TB, D_MODEL), lambda i, w0, wt: (i, 0)),
            pl.BlockSpec((TB, PLE_DIM), lambda i, w0, wt: (i + p_row0 // TB, 0)),
            pl.BlockSpec((TB, 2), lambda i, w0, wt: (i, 0)),
            pl.BlockSpec(memory_space=pl.ANY),
            _layer_spec(layer, (1, D_MODEL)),
            _layer_spec(layer, (D_MODEL, D_MODEL)),
            _layer_spec(layer, (PLE_DIM, D_MODEL)),
        ],
        out_specs=pl.BlockSpec((TB, D_MODEL), lambda i, w0, wt: (i, 0)),
        scratch_shapes=[pltpu.VMEM((2, N_EXPERTS, CB_W, D_MODEL), BF16), pltpu.VMEM((TB, D_MODEL), F32),
                        pltpu.SemaphoreType.DMA((2, N_EXPERTS))],
    )
    return pl.pallas_call(
        _combine_kernel,
        grid_spec=grid_spec,
        out_shape=jax.ShapeDtypeStruct((tok, D_MODEL), F32),
        compiler_params=pltpu.CompilerParams(
            dimension_semantics=("arbitrary",), vmem_limit_bytes=VMEM_LIMIT),
        name="moe_combine",
    )(win_start, win_rows, x2, p2, pos, y_sorted, ple_gain, wpg, wpp)


def _route_metadata(route, tok):
    TM = MOE_TM
    nt = (2 * tok) // TM + N_EXPERTS
    rows = nt * TM
    g1, g2 = route[0], route[1]
    i1, i2 = route[2].astype(I32), route[3].astype(I32)
    experts = jnp.arange(N_EXPERTS, dtype=I32)
    member = ((i1[:, None] == experts) | (i2[:, None] == experts)).astype(I32)
    rank = jnp.cumsum(member, axis=0) - member
    counts = jnp.sum(member, axis=0)
    tiles_e = (counts + TM - 1) // TM
    tile_end = jnp.cumsum(tiles_e)
    tile_start = tile_end - tiles_e
    seg_start = tile_start * TM
    sel1 = i1[:, None] == experts
    sel2 = i2[:, None] == experts
    pos1 = jnp.sum(jnp.where(sel1, seg_start[None, :] + rank, 0), axis=1).astype(I32)
    pos2 = jnp.sum(jnp.where(sel2, seg_start[None, :] + rank, 0), axis=1).astype(I32)

    tiles = jnp.arange(nt, dtype=I32)
    n_used = tile_end[-1]
    tile_rows = (tiles < n_used).astype(I32)
    te = jnp.sum((tiles[:, None] >= tile_end[None, :]).astype(I32), axis=1)
    te_last = jnp.sum(((n_used - 1) >= tile_end).astype(I32))
    tile_expert = jnp.where(tile_rows > 0, jnp.minimum(te, N_EXPERTS - 1), te_last).astype(I32)
    mine = tile_expert[:, None] == experts[None, :]
    rows_left = (jnp.sum(jnp.where(mine, counts[None, :], 0), axis=1)
                 - (tiles - jnp.sum(jnp.where(mine, tile_start[None, :], 0), axis=1)) * TM)
    tile_rows = jnp.where(tile_rows > 0, jnp.clip(rows_left, 0, TM), 0).astype(I32)

    steps = jnp.arange(rows // DP_TM, dtype=I32)
    step_tile = steps // (TM // DP_TM)
    step_valid = step_tile < n_used
    step_expert = jnp.minimum(jnp.sum((step_tile[:, None] >= tile_end[None, :]).astype(I32), axis=1),
                              N_EXPERTS - 1)
    pick = step_expert[:, None] == experts[None, :]
    rank_lo = steps * DP_TM - jnp.sum(jnp.where(pick, seg_start[None, :], 0), axis=1)
    rank_hi = jnp.minimum(rank_lo + DP_TM, jnp.sum(jnp.where(pick, counts[None, :], 0), axis=1)) - 1
    before = jnp.sum(jnp.where(pick[:, :, None], rank[::DP_CH, :].T[None, :, :], 0), axis=1)
    chunk_lo = jnp.sum((before <= rank_lo[:, None]).astype(I32), axis=1) - 1
    chunk_hi = jnp.sum((before <= rank_hi[:, None]).astype(I32), axis=1) - 1
    live = step_valid & (rank_hi >= rank_lo)
    chunk_lo = jnp.where(live, chunk_lo, 0).astype(I32)
    chunk_hi = jnp.where(live, chunk_hi, -1).astype(I32)

    blk_start = seg_start[None, :] + rank[::CB_TB, :]
    win_start = jnp.minimum((blk_start // CB_ALIGN) * CB_ALIGN, rows - CB_W).astype(I32)
    blk_count = jnp.concatenate([rank[CB_TB::CB_TB, :], counts[None, :]], axis=0) - rank[::CB_TB, :]
    win_rows = jnp.concatenate([blk_start.reshape(-1), (blk_start + blk_count).reshape(-1)]).astype(I32)
    win_start = win_start.reshape(-1)
    pos = jnp.stack([pos1, pos2], axis=1)
    return (tile_expert, tile_rows, chunk_lo, chunk_hi, jnp.stack([pos1, pos2]), jnp.stack([g1, g2]),
            pos, win_start, win_rows)


def _pack_w_in(w):
    o = [0, 1024, 1536, 2048, 2052, 2056, 2568, 3080, 3592, 4104, 7176]
    w = w.astype(BF16)
    main = jnp.concatenate([w[..., o[0]:o[3]], w[..., o[5]:o[10]]], axis=-1)
    gates = jnp.pad(w[..., o[3]:o[5]], ((0, 0), (0, 0), (0, LANES - 2 * M_HEADS)))
    return main, gates


def kernel(x, p, attn_norm, w_in, m_conv_w, m_conv_b, m_gate_bias, m_head_norm, a_q_norm, a_k_norm,
           a_lambda, a_head_norm, pool_w, pool_scale, w_branch, w_out, ffn_norm, dense_w_gu, dense_w_down,
           router_w, router_b, moe_w_gu, moe_w_down, ple_norm, ple_w_gate, ple_w_proj):
    b, s, d = x.shape
    depth = w_in.shape[0]
    tok = b * s
    x2 = x.reshape(tok, d)

    def rows(a):
        return a.reshape(a.shape[0], 1, a.shape[1])

    w_main, w_if = _pack_w_in(w_in)
    gate_bias = rows(jnp.pad(m_gate_bias, ((0, 0), (0, LANES - 2 * M_HEADS))))
    q_gain = rows(jnp.tile(a_q_norm, (1, 2)))
    k_gain = rows(jnp.tile(a_k_norm, (1, 2)))
    p2 = p.reshape(depth * tok, PLE_DIM)
    pool_wb, w_branch_b, w_out_b = pool_w.astype(BF16), w_branch.astype(BF16), w_out.astype(BF16)
    ple = (rows(ple_norm), ple_w_gate.astype(BF16), ple_w_proj.astype(BF16))
    dense_gu, dense_down = dense_w_gu.astype(BF16), dense_w_down.astype(BF16)
    router_wt = jnp.pad(jnp.swapaxes(router_w, 1, 2), ((0, 0), (0, ROUTER_ROWS - N_EXPERTS), (0, 0)))
    router_bc = router_b.reshape(router_b.shape[0], N_EXPERTS, 1)

    for layer in range(depth):
        proj, gates = _inproj(x2, layer, rows(attn_norm), w_main, w_if)
        proj3 = proj.reshape(b, s, PROJ_WIDTH)
        h_m = _mlstm(proj3, gates.reshape(b, s, LANES), layer, m_conv_w, rows(m_conv_b), gate_bias,
                     rows(m_head_norm))
        lam_init = 0.8 - 0.6 * math.exp(-0.3 * layer)
        h_a = _attention(proj3, layer, q_gain, k_gain, a_lambda, rows(a_head_norm), lam_init)
        merge_args = (x2, h_m.reshape(tok, BRANCH_WIDTH), h_a.reshape(tok, BRANCH_WIDTH), proj, layer,
                      pool_wb, rows(pool_scale), w_branch_b, w_out_b, rows(ffn_norm), s)
        j = layer // 2
        if layer % 2 == 0:
            x_mid, hf = _merge(*merge_args)
            x2 = _ffn(hf, x_mid, p2, layer * tok, j, dense_gu, dense_down, layer, *ple)
        else:
            x_mid, hf, route = _merge(*merge_args, router=(j, router_wt, router_bc))
            (tile_expert, tile_rows, chunk_lo, chunk_hi, pos_t, gate_t, pos,
             win_start, win_rows) = _route_metadata(route, tok)
            x_sorted, gate_sorted = _dispatch(hf, pos_t, gate_t, chunk_lo, chunk_hi)
            y_sorted = _moe(x_sorted, gate_sorted, moe_w_gu[j], moe_w_down[j], tile_expert, tile_rows)
            x2 = _combine(x_mid, p2, layer * tok, pos, y_sorted, win_start, win_rows, layer, *ple)
    return x2.reshape(b, s, d)
```

```python
import functools
import math

import jax
import jax.numpy as jnp
from jax import lax
from jax.experimental import pallas as pl
from jax.experimental.pallas import tpu as pltpu

F32 = jnp.float32
BF16 = jnp.bfloat16
I32 = jnp.int32

D_MODEL = 1024
PLE_DIM = 256
NORM_EPS = 1e-6
BRANCH_WIDTH = 512

M_HEADS = 4
M_DK = 128
M_CONV = 4
M_CHUNK = 128

A_HEADS = 4
A_DV = 128
A_DQK = 64
ALIBI_SLOPES = tuple(2.0 ** (-8.0 * (h + 1) / A_HEADS) for h in range(A_HEADS))

P_WINDOWS = (2, 4, 8, 16)
P_GC = 128

D_FF = 2816
N_EXPERTS = 8
D_FF_EXPERT = 3584

PROJ_WIDTH = 7168
COL_MQK, COL_MV, COL_MO, COL_AQ, COL_AK, COL_AV, COL_PU, COL_G = 0, 1024, 1536, 2048, 2560, 3072, 3584, 4096

LANES = 128
NEG_BIG = -1e30

V7X_VMEM_BYTES = 64 * 1024 * 1024
VMEM_LIMIT = V7X_VMEM_BYTES - 8 * 1024 * 1024


def _sigmoid(x):
    return 0.5 * jnp.tanh(0.5 * x) + 0.5


def _rms(x, gain):
    return x * lax.rsqrt(jnp.mean(x * x, axis=-1, keepdims=True) + NORM_EPS) * gain


def _dot(a, b):
    return jnp.dot(a, b, preferred_element_type=F32)


def _dot_nt(a, b):
    return lax.dot_general(a, b, (((1,), (1,)), ((), ())), preferred_element_type=F32)


def _dot_tn(a, b):
    return lax.dot_general(a, b, (((0,), (0,)), ((), ())), preferred_element_type=F32)


def _layer_spec(layer, shape, **kw):
    zeros = (0,) * len(shape)
    return pl.BlockSpec((None,) + tuple(shape), lambda *_: (layer,) + zeros, **kw)


IN_TM = 512
IN_TN = 1024


def _inproj_kernel(x_ref, gain_ref, w_ref, wif_ref, out_ref, gates_ref):
    hn = _rms(x_ref[...], gain_ref[...]).astype(BF16)
    gates_ref[...] = _dot(hn, wif_ref[...])
    for n in range(PROJ_WIDTH // IN_TN):
        cols = slice(n * IN_TN, (n + 1) * IN_TN)
        out_ref[:, cols] = _dot(hn, w_ref[:, cols]).astype(BF16)


def _inproj(x2, layer, gain, w_main, w_if):
    tok = x2.shape[0]
    return pl.pallas_call(
        _inproj_kernel,
        grid=(tok // IN_TM,),
        in_specs=[
            pl.BlockSpec((IN_TM, D_MODEL), lambda i: (i, 0)),
            _layer_spec(layer, (1, D_MODEL)),
            _layer_spec(layer, (D_MODEL, PROJ_WIDTH), pipeline_mode=pl.Buffered(1)),
            _layer_spec(layer, (D_MODEL, LANES)),
        ],
        out_specs=[
            pl.BlockSpec((IN_TM, PROJ_WIDTH), lambda i: (i, 0)),
            pl.BlockSpec((IN_TM, LANES), lambda i: (i, 0)),
        ],
        out_shape=[
            jax.ShapeDtypeStruct((tok, PROJ_WIDTH), BF16),
            jax.ShapeDtypeStruct((tok, LANES), F32),
        ],
        compiler_params=pltpu.CompilerParams(
            dimension_semantics=("parallel",), vmem_limit_bytes=VMEM_LIMIT),
        name="inproj",
    )(x2, gain, w_main, w_if)


CONV_TAIL = 16
ML_NB = 2


def _mlstm_kernel(qk_ref, vo_ref, g_ref, cw_ref, cb_ref, gb_ref, hg_ref, out_ref,
                  c_sc, m_sc, prev_sc):
    @pl.when(pl.program_id(1) == 0)
    def _():
        c_sc[...] = jnp.zeros_like(c_sc)
        m_sc[...] = jnp.zeros_like(m_sc)
        prev_sc[...] = jnp.zeros_like(prev_sc)

    L = M_CHUNK
    delay = (lax.broadcasted_iota(I32, (L, CONV_TAIL + L), 0) + CONV_TAIL
             - lax.broadcasted_iota(I32, (L, CONV_TAIL + L), 1))
    shift = {s: jnp.where(delay == s, 1.0, 0.0).astype(BF16) for s in range(1, M_CONV)}
    lane = lax.broadcasted_iota(I32, (L, LANES), 1)
    rowl = lax.broadcasted_iota(I32, (L, LANES), 0)
    causal = lax.broadcasted_iota(I32, (L, L), 0) >= lax.broadcasted_iota(I32, (L, L), 1)
    is_f = (lane >= M_HEADS) & (lane < 2 * M_HEADS)
    cw = cw_ref[...]
    hg = hg_ref[...]
    items = [(bb, h) for bb in range(ML_NB) for h in range(M_HEADS)]

    qk, gc, bcum, gc_t, bcum_t = [], [], [], [], []
    for bb in range(ML_NB):
        xb = qk_ref[bb]
        ext = jnp.concatenate([prev_sc[bb], xb], axis=0)
        y = cb_ref[...] + cw[M_CONV - 1:M_CONV, :] * xb.astype(F32)
        for s in range(1, M_CONV):
            y = y + cw[M_CONV - 1 - s:M_CONV - s, :] * _dot(shift[s], ext)
        prev_sc[bb] = xb[L - CONV_TAIL:]
        qk.append(y * _sigmoid(y))

        g = g_ref[bb] + gb_ref[...]
        logf = jnp.minimum(g, 0.0) - jnp.log(1.0 + jnp.exp(-jnp.abs(g)))
        gcb = jnp.where(is_f, logf, jnp.where(lane < M_HEADS, g, 0.0))
        acc = jnp.where(is_f, logf, 0.0)
        k = 1
        while k < L:
            acc = acc + jnp.where(rowl >= k, pltpu.roll(acc, k, 0), 0.0)
            k *= 2
        gc.append(gcb)
        bcum.append(acc)
        gc_t.append(gcb.T)
        bcum_t.append(acc.T)

    def each(fn):
        return {it: fn(*it) for it in items}

    def head(h):
        return slice(h * M_DK, (h + 1) * M_DK)

    b_col = each(lambda bb, h: bcum[bb][:, M_HEADS + h:M_HEADS + h + 1])
    i_col = each(lambda bb, h: gc[bb][:, h:h + 1])
    m_prev = each(lambda bb, h: m_sc[bb, h:h + 1, 0:1])
    log_d = each(lambda bb, h: jnp.where(
        causal, b_col[bb, h] - bcum_t[bb][M_HEADS + h:M_HEADS + h + 1, :] + gc_t[bb][h:h + 1, :], NEG_BIG))
    inter = each(lambda bb, h: b_col[bb, h] + m_prev[bb, h])
    m_t = each(lambda bb, h: jnp.maximum(jnp.max(log_d[bb, h], axis=-1, keepdims=True), inter[bb, h]))
    d_w = each(lambda bb, h: jnp.exp(log_d[bb, h] - m_t[bb, h]))
    w_inter = each(lambda bb, h: jnp.exp(inter[bb, h] - m_t[bb, h]))

    kh = each(lambda bb, h: qk[bb][:, BRANCH_WIDTH + h * M_DK:BRANCH_WIDTH + (h + 1) * M_DK] * (M_DK ** -0.5))
    qb = each(lambda bb, h: qk[bb][:, head(h)].astype(BF16))
    ones = jnp.ones((L, M_DK), BF16)
    v_aug = each(lambda bb, h: jnp.concatenate([vo_ref[bb, :, head(h)], ones], axis=1))
    sc = each(lambda bb, h: (_dot_nt(qb[bb, h], kh[bb, h].astype(BF16)) * d_w[bb, h]).astype(BF16))

    intra = each(lambda bb, h: _dot(sc[bb, h], v_aug[bb, h]))
    carried = each(lambda bb, h: _dot(qb[bb, h], c_sc[bb, h].astype(BF16)))
    both = each(lambda bb, h: intra[bb, h] + w_inter[bb, h] * carried[bb, h])
    floor = each(lambda bb, h: jnp.exp(-m_t[bb, h]))
    hh = each(lambda bb, h: both[bb, h][:, :M_DK] / jnp.maximum(jnp.abs(both[bb, h][:, M_DK:]), floor[bb, h]))
    ms = each(lambda bb, h: jnp.mean(hh[bb, h] * hh[bb, h], axis=-1, keepdims=True))
    for bb, h in items:
        hn = hh[bb, h] * lax.rsqrt(ms[bb, h] + NORM_EPS) * hg[:, head(h)]
        o_pre = vo_ref[bb, :, BRANCH_WIDTH + h * M_DK:BRANCH_WIDTH + (h + 1) * M_DK].astype(F32)
        out_ref[bb, :, head(h)] = (hn * _sigmoid(o_pre)).astype(BF16)

    b_last = each(lambda bb, h: b_col[bb, h][L - 1:L, :])
    log_w = each(lambda bb, h: b_last[bb, h] - b_col[bb, h] + i_col[bb, h])
    m_new = each(lambda bb, h: jnp.maximum(b_last[bb, h] + m_prev[bb, h],
                                           jnp.max(log_w[bb, h], axis=0, keepdims=True)))
    decay = each(lambda bb, h: jnp.exp(b_last[bb, h] + m_prev[bb, h] - m_new[bb, h]))
    kw = each(lambda bb, h: (kh[bb, h] * jnp.exp(log_w[bb, h] - m_new[bb, h])).astype(BF16))
    upd = each(lambda bb, h: _dot_tn(kw[bb, h], v_aug[bb, h]))
    for bb, h in items:
        c_sc[bb, h] = decay[bb, h] * c_sc[bb, h] + upd[bb, h]
        m_sc[bb, h:h + 1, :] = jnp.broadcast_to(m_new[bb, h], (1, LANES))


def _mlstm(proj3, gates3, layer, conv_w, conv_b, gate_bias, head_gain):
    b, s, _ = proj3.shape
    L = M_CHUNK
    return pl.pallas_call(
        _mlstm_kernel,
        grid=(b // ML_NB, s // L),
        in_specs=[
            pl.BlockSpec((ML_NB, L, 1024), lambda i, c: (i, c, COL_MQK // 1024)),
            pl.BlockSpec((ML_NB, L, 1024), lambda i, c: (i, c, COL_MV // 1024)),
            pl.BlockSpec((ML_NB, L, LANES), lambda i, c: (i, c, 0)),
            _layer_spec(layer, (M_CONV, 1024)),
            _layer_spec(layer, (1, 1024)),
            _layer_spec(layer, (1, LANES)),
            _layer_spec(layer, (1, BRANCH_WIDTH)),
        ],
        out_specs=pl.BlockSpec((ML_NB, L, BRANCH_WIDTH), lambda i, c: (i, c, 0)),
        out_shape=jax.ShapeDtypeStruct((b, s, BRANCH_WIDTH), BF16),
        scratch_shapes=[
            pltpu.VMEM((ML_NB, M_HEADS, M_DK, 2 * M_DK), F32),
            pltpu.VMEM((ML_NB, 8, LANES), F32),
            pltpu.VMEM((ML_NB, CONV_TAIL, 1024), BF16),
        ],
        compiler_params=pltpu.CompilerParams(
            dimension_semantics=("parallel", "arbitrary"), vmem_limit_bytes=VMEM_LIMIT),
        name="mlstm",
    )(proj3, proj3, gates3, conv_w, conv_b, gate_bias, head_gain)


AT_T = 256
ALIBI_BASE = 256
AT_TK = 512


def _halfnorm(x, gain):
    lane = lax.broadcasted_iota(I32, (1, LANES), 1)
    lo = lane < A_DQK
    x2 = x * x
    ms_lo = jnp.sum(jnp.where(lo, x2, 0.0), axis=-1, keepdims=True)
    ms_hi = jnp.sum(jnp.where(lo, 0.0, x2), axis=-1, keepdims=True)
    ms = jnp.where(lo, ms_lo, ms_hi) * (1.0 / A_DQK)
    return x * lax.rsqrt(ms + NORM_EPS) * gain


AV_ROWS = A_DV + 16


AT_NH = 2


def _attn_kernel(slopes_ref, q_ref, k_ref, v_ref, qg_ref, kg_ref, lam_ref, hg_ref, out_ref,
                 k1_sc, k2_sc, vt_sc, s_sc, p_sc, acc_sc, res_sc, *, lam_init):
    T = AT_T
    TK = AT_TK
    CH = 128
    heads = range(AT_NH)
    qi = pl.program_id(2)
    lane = lax.broadcasted_iota(I32, (1, LANES), 1)
    slope = [slopes_ref[pl.program_id(1) * AT_NH + hd] for hd in heads]

    def cols(hd):
        return slice(hd * LANES, (hd + 1) * LANES)

    @pl.when(qi == 0)
    def _():
        s_len = k_ref.shape[0]
        pos = lax.broadcasted_iota(I32, (s_len, 1), 0)
        for hd in heads:
            kn = _halfnorm(k_ref[:, cols(hd)].astype(F32), kg_ref[...])
            c_loc = (pos & (ALIBI_BASE - 1)).astype(F32) * slope[hd]
            c_blk = (pos >> int(math.log2(ALIBI_BASE))).astype(F32) * (slope[hd] * ALIBI_BASE)
            k1_sc[hd] = jnp.where(lane < A_DQK, kn, jnp.where(lane == A_DQK, c_loc, jnp.where(
                lane == A_DQK + 1, c_blk, 0.0))).astype(BF16)
            k2_sc[hd] = jnp.where(lane >= A_DQK, kn, jnp.where(lane == 0, c_loc, jnp.where(
                lane == 1, c_blk, 0.0))).astype(BF16)
            vt_sc[hd, :A_DV, :] = v_ref[:, cols(hd)].astype(F32).T.astype(BF16)
            vt_sc[hd, A_DV:, :] = jnp.ones((AV_ROWS - A_DV, s_len), BF16)
        res_sc[...] = jnp.ones_like(res_sc)

    lp = lam_ref[...]
    lam = (jnp.exp(jnp.sum(lp[0:1, :] * lp[1:2, :], axis=-1, keepdims=True))
           - jnp.exp(jnp.sum(lp[2:3, :] * lp[3:4, :], axis=-1, keepdims=True)) + lam_init)

    def finish(src, qblock):
        row0 = pl.multiple_of(qblock * T, T)
        hg = hg_ref[...]
        for hd in heads:
            per_map = T // LANES
            a1 = jnp.concatenate([src[hd, i] for i in range(per_map)], axis=1)
            a2 = jnp.concatenate([src[hd, per_map + i] for i in range(per_map)], axis=1)
            o_t = a1[:A_DV, :] / a1[A_DV:A_DV + 1, :] - lam * (a2[:A_DV, :] / a2[A_DV:A_DV + 1, :])
            out_ref[pl.ds(row0, T), cols(hd)] = (
                _rms(o_t.T, hg[:, cols(hd)]) * (1.0 - lam_init)).astype(BF16)

    finish(res_sc, jnp.maximum(qi - 1, 0))

    qn = [_halfnorm(q_ref[:, cols(hd)].astype(F32), qg_ref[...]) * (A_DQK ** -0.5) for hd in heads]
    q1 = [jnp.where(lane < A_DQK, qn[hd], jnp.where(lane < A_DQK + 2, 1.0, 0.0)).astype(BF16) for hd in heads]
    q2 = [jnp.where(lane >= A_DQK, qn[hd], jnp.where(lane < 2, 1.0, 0.0)).astype(BF16) for hd in heads]

    def scores_to_scratch(hd, j):
        start = pl.multiple_of(j * TK, TK)
        for mp, (k_sc, q) in enumerate(((k1_sc, q1), (k2_sc, q2))):
            sc = _dot_nt(k_sc[hd, pl.ds(start, TK), :], q[hd])
            for hh in range(T // LANES):
                s_sc[hd, mp, hh] = sc[:, hh * LANES:(hh + 1) * LANES]

    def causal_bias():
        first = (qi * T - last * TK) >> int(math.log2(CH))
        tri = jnp.where(lax.broadcasted_iota(I32, (CH, LANES), 0) <= lax.broadcasted_iota(I32, (CH, LANES), 1),
                        0.0, NEG_BIG)
        bias = {}
        for c in range(TK // CH):
            for hh in range(T // LANES):
                ahead = first + hh - c
                bias[c, hh] = jnp.where(ahead > 0, 0.0, jnp.where(ahead == 0, tri, NEG_BIG))
        return bias

    def softmax_to_scratch(hd, j, m, bias, keep=None):
        m_out, alpha_out = [], []
        for mp in range(2):
            for hh in range(T // LANES):
                idx = mp * (T // LANES) + hh

                def chunk(c):
                    sc = s_sc[hd, mp, hh, c * CH:(c + 1) * CH, :]
                    return sc if bias is None else sc + bias[c, hh]

                m_new = m[idx]
                for c in range(TK // CH):
                    m_new = jnp.maximum(m_new, jnp.max(chunk(c), axis=0, keepdims=True))
                for c in range(TK // CH):
                    p = jnp.exp(chunk(c) - m_new).astype(BF16)
                    if keep is None:
                        p_sc[hd, idx, c * CH:(c + 1) * CH, :] = p
                    else:
                        keep[c, idx] = p
                alpha_out.append(jnp.exp(m[idx] - m_new))
                m_out.append(m_new)
        return tuple(m_out), tuple(alpha_out)

    def values_from_scratch(hd, j, alpha, p=None):
        vt = vt_sc[hd, :, pl.ds(pl.multiple_of(j * TK, TK), TK)]
        if p is None:
            p = jnp.concatenate([p_sc[hd, idx] for idx in range(2 * T // LANES)], axis=1)
        pv = _dot(vt, p)
        for idx in range(2 * T // LANES):
            acc_sc[hd, idx] = alpha[idx] * acc_sc[hd, idx] + pv[:, idx * LANES:(idx + 1) * LANES]

    def body(j, carry):
        m, alpha = carry
        for hd in heads:
            values_from_scratch(hd, jnp.maximum(j - 1, 0), alpha[hd])
        stats = [softmax_to_scratch(hd, j, m[hd], None) for hd in heads]
        for hd in heads:
            scores_to_scratch(hd, j + 1)
        return tuple(st[0] for st in stats), tuple(st[1] for st in stats)

    n_slab = 2 * T // LANES
    last = (qi * T) >> int(math.log2(TK))
    p_sc[...] = jnp.zeros_like(p_sc)
    acc_sc[...] = jnp.zeros_like(acc_sc)
    for hd in heads:
        scores_to_scratch(hd, 0)
    m, alpha = lax.fori_loop(0, last, body, (((jnp.full((1, LANES), NEG_BIG, F32),) * n_slab,) * AT_NH,
                                             ((jnp.ones((1, LANES), F32),) * n_slab,) * AT_NH))
    for hd in heads:
        values_from_scratch(hd, jnp.maximum(last - 1, 0), alpha[hd])
    bias = causal_bias()
    kept = [dict() for _ in heads]
    alpha = [softmax_to_scratch(hd, last, m[hd], bias, kept[hd])[1] for hd in heads]
    for hd in heads:
        p_last = jnp.concatenate([jnp.concatenate([kept[hd][c, idx] for c in range(TK // CH)], axis=0)
                                  for idx in range(n_slab)], axis=1)
        values_from_scratch(hd, last, alpha[hd], p_last)

    res_sc[...] = acc_sc[...]

    @pl.when(qi == pl.num_programs(2) - 1)
    def _():
        finish(acc_sc, qi)


def _attention(proj3, layer, q_gain, k_gain, lam_params, head_gain, lam_init):
    b, s, _ = proj3.shape
    T = AT_T
    W = AT_NH * LANES
    slopes = jnp.asarray(ALIBI_SLOPES, F32)
    return pl.pallas_call(
        functools.partial(_attn_kernel, lam_init=lam_init),
        grid=(b, A_HEADS // AT_NH, s // T),
        in_specs=[
            pl.BlockSpec(memory_space=pltpu.SMEM),
            pl.BlockSpec((None, T, W), lambda i, h, q: (i, q, COL_AQ // W + h)),
            pl.BlockSpec((None, s, W), lambda i, h, q: (i, 0, COL_AK // W + h)),
            pl.BlockSpec((None, s, W), lambda i, h, q: (i, 0, COL_AV // W + h)),
            _layer_spec(layer, (1, LANES)),
            _layer_spec(layer, (1, LANES)),
            _layer_spec(layer, (4, A_DQK)),
            pl.BlockSpec((None, 1, W), lambda i, h, q: (layer, 0, h)),
        ],
        out_specs=pl.BlockSpec((None, s, W), lambda i, h, q: (i, 0, h)),
        out_shape=jax.ShapeDtypeStruct((b, s, BRANCH_WIDTH), BF16),
        scratch_shapes=[pltpu.VMEM((AT_NH, s, LANES), BF16), pltpu.VMEM((AT_NH, s, LANES), BF16),
                        pltpu.VMEM((AT_NH, AV_ROWS, s), BF16),
                        pltpu.VMEM((AT_NH, 2, T // LANES, AT_TK, LANES), F32),
                        pltpu.VMEM((AT_NH, 2 * T // LANES, AT_TK, LANES), BF16),
                        pltpu.VMEM((AT_NH, 2 * T // LANES, AV_ROWS, LANES), F32),
                        pltpu.VMEM((AT_NH, 2 * T // LANES, AV_ROWS, LANES), F32)],
        compiler_params=pltpu.CompilerParams(
            dimension_semantics=("parallel", "parallel", "arbitrary"), vmem_limit_bytes=VMEM_LIMIT),
        name="diff_attn",
    )(slopes, proj3, proj3, proj3, q_gain, k_gain, lam_params, head_gain)


MG_TM = 512
MG_HALO = 128
ROUTER_ROWS = 16
ROUTE_ROWS = 8


def _merge_kernel(x_ref, hm_ref, ha_ref, u_ref, halo_ref, gm_ref, ga_ref, gp_ref, pw_ref, ps_ref,
                  wb_ref, wo_ref, fg_ref, *rest, seq_tiles, routed):
    if routed:
        wr_ref, br_ref, xo_ref, hf_ref, route_ref = rest
    else:
        xo_ref, hf_ref = rest
    TM = MG_TM
    i = pl.program_id(0)
    seq_tile = i % seq_tiles

    u = u_ref[...]
    halo = jnp.where(seq_tile > 0, halo_ref[...], jnp.zeros_like(halo_ref))
    u_ext = jnp.concatenate([halo, u], axis=0)
    dist = (lax.broadcasted_iota(I32, (MG_HALO, 2 * MG_HALO), 0) + MG_HALO
            - lax.broadcasted_iota(I32, (MG_HALO, 2 * MG_HALO), 1))
    t_pos = seq_tile * TM + lax.broadcasted_iota(I32, (TM, 1), 0)
    ps = ps_ref[...]
    hp_parts = []
    for g, w in enumerate(P_WINDOWS):
        sl = slice(g * P_GC, (g + 1) * P_GC)
        band = jnp.where((dist >= 0) & (dist < w), 1.0, 0.0).astype(BF16)
        ug = u[:, sl]
        sums = jnp.concatenate([_dot(band, u_ext[r:r + 2 * MG_HALO, sl]) for r in range(0, TM, MG_HALO)],
                               axis=0)
        cnt = jnp.minimum(t_pos + 1, w).astype(F32)
        pooled = sums / cnt - ug.astype(F32)
        hp_parts.append((_dot(pooled.astype(BF16), pw_ref[g]) * ps[:, sl]).astype(BF16))
    hp = jnp.concatenate(hp_parts, axis=1)

    merged = (_sigmoid(gm_ref[...].astype(F32)) * _dot(hm_ref[...], wb_ref[0])
              + _sigmoid(ga_ref[...].astype(F32)) * _dot(ha_ref[...], wb_ref[1])
              + _sigmoid(gp_ref[...].astype(F32)) * _dot(hp, wb_ref[2]))
    x_new = x_ref[...] + _dot(merged.astype(BF16), wo_ref[...])
    xo_ref[...] = x_new
    hf = _rms(x_new, fg_ref[...])
    hf_ref[...] = hf.astype(BF16)

    if routed:
        hf_hi = hf.astype(BF16)
        hf_lo = (hf - hf_hi.astype(F32)).astype(BF16)
        wr = wr_ref[...]
        wr_hi = wr.astype(BF16)
        wr_lo = (wr - wr_hi.astype(F32)).astype(BF16)
        logits = (_dot_nt(wr_hi, hf_hi) + _dot_nt(wr_lo, hf_hi) + _dot_nt(wr_hi, hf_lo))[:N_EXPERTS, :]
        logits = logits + br_ref[...]
        expert = lax.broadcasted_iota(I32, (N_EXPERTS, TM), 0).astype(F32)
        m1 = jnp.max(logits, axis=0, keepdims=True)
        i1 = jnp.min(jnp.where(logits == m1, expert, float(N_EXPERTS)), axis=0, keepdims=True)
        rest_l = jnp.where(expert == i1, NEG_BIG, logits)
        m2 = jnp.max(rest_l, axis=0, keepdims=True)
        i2 = jnp.min(jnp.where(rest_l == m2, expert, float(N_EXPERTS)), axis=0, keepdims=True)
        g1 = 1.0 / (1.0 + jnp.exp(m2 - m1))
        route_ref[...] = jnp.concatenate(
            [g1, 1.0 - g1, i1, i2, jnp.zeros((ROUTE_ROWS - 4, TM), F32)], axis=0)


def _merge(x2, hm2, ha2, proj2, layer, pool_w, pool_scale, w_branch, w_out, ffn_gain, seq, router=None):
    tok = x2.shape[0]
    TM = MG_TM
    routed = router is not None
    hb = TM // MG_HALO
    in_specs = [
        pl.BlockSpec((TM, D_MODEL), lambda i: (i, 0)),
        pl.BlockSpec((TM, BRANCH_WIDTH), lambda i: (i, 0)),
        pl.BlockSpec((TM, BRANCH_WIDTH), lambda i: (i, 0)),
        pl.BlockSpec((TM, BRANCH_WIDTH), lambda i: (i, COL_PU // BRANCH_WIDTH)),
        pl.BlockSpec((MG_HALO, BRANCH_WIDTH), lambda i: (jnp.maximum(i * hb - 1, 0), COL_PU // BRANCH_WIDTH)),
        pl.BlockSpec((TM, D_MODEL), lambda i: (i, COL_G // D_MODEL)),
        pl.BlockSpec((TM, D_MODEL), lambda i: (i, COL_G // D_MODEL + 1)),
        pl.BlockSpec((TM, D_MODEL), lambda i: (i, COL_G // D_MODEL + 2)),
        _layer_spec(layer, (len(P_WINDOWS), P_GC, P_GC)),
        _layer_spec(layer, (1, BRANCH_WIDTH)),
        _layer_spec(layer, (3, BRANCH_WIDTH, D_MODEL)),
        _layer_spec(layer, (D_MODEL, D_MODEL)),
        _layer_spec(layer, (1, D_MODEL)),
    ]
    args = [x2, hm2, ha2, proj2, proj2, proj2, proj2, proj2, pool_w, pool_scale, w_branch, w_out, ffn_gain]
    out_specs = [pl.BlockSpec((TM, D_MODEL), lambda i: (i, 0)),
                 pl.BlockSpec((TM, D_MODEL), lambda i: (i, 0))]
    out_shape = [jax.ShapeDtypeStruct((tok, D_MODEL), F32), jax.ShapeDtypeStruct((tok, D_MODEL), BF16)]
    if routed:
        moe_layer, router_w, router_b = router
        in_specs += [_layer_spec(moe_layer, (ROUTER_ROWS, D_MODEL)), _layer_spec(moe_layer, (N_EXPERTS, 1))]
        args += [router_w, router_b]
        out_specs.append(pl.BlockSpec((ROUTE_ROWS, TM), lambda i: (0, i)))
        out_shape.append(jax.ShapeDtypeStruct((ROUTE_ROWS, tok), F32))
    return pl.pallas_call(
        functools.partial(_merge_kernel, seq_tiles=seq // TM, routed=routed),
        grid=(tok // TM,),
        in_specs=in_specs,
        out_specs=out_specs,
        out_shape=out_shape,
        compiler_params=pltpu.CompilerParams(
            dimension_semantics=("parallel",), vmem_limit_bytes=VMEM_LIMIT),
        name="merge_routed" if routed else "merge",
    )(*args)


def _ple_epilogue(x_new, p, pg_gain, wpg, wpp):
    gate = _sigmoid(_dot(_rms(x_new, pg_gain).astype(BF16), wpg))
    return x_new + gate * _dot(p.astype(BF16), wpp)


FF_TM = 512


def _ffn_kernel(hf_ref, x_ref, p_ref, wgu_ref, wd_ref, pgn_ref, wpg_ref, wpp_ref, out_ref):
    hf = hf_ref[...]
    g = _dot(hf, wgu_ref[:, :D_FF])
    u = _dot(hf, wgu_ref[:, D_FF:])
    y = _dot((g * _sigmoid(g) * u).astype(BF16), wd_ref[...])
    out_ref[...] = _ple_epilogue(x_ref[...] + y, p_ref[...], pgn_ref[...], wpg_ref[...], wpp_ref[...])


def _ffn(hf2, x2, p2, p_row0, ffn_layer, w_gu, w_down, layer, ple_gain, wpg, wpp):
    tok = x2.shape[0]
    TM = FF_TM
    once = pl.Buffered(1)
    return pl.pallas_call(
        _ffn_kernel,
        grid=(tok // TM,),
        in_specs=[
            pl.BlockSpec((TM, D_MODEL), lambda i: (i, 0)),
            pl.BlockSpec((TM, D_MODEL), lambda i: (i, 0)),
            pl.BlockSpec((TM, PLE_DIM), lambda i: (i + p_row0 // TM, 0)),
            _layer_spec(ffn_layer, (D_MODEL, 2 * D_FF), pipeline_mode=once),
            _layer_spec(ffn_layer, (D_FF, D_MODEL), pipeline_mode=once),
            _layer_spec(layer, (1, D_MODEL)),
            _layer_spec(layer, (D_MODEL, D_MODEL), pipeline_mode=once),
            _layer_spec(layer, (PLE_DIM, D_MODEL), pipeline_mode=once),
        ],
        out_specs=pl.BlockSpec((TM, D_MODEL), lambda i: (i, 0)),
        out_shape=jax.ShapeDtypeStruct((tok, D_MODEL), F32),
        compiler_params=pltpu.CompilerParams(
            dimension_semantics=("parallel",), vmem_limit_bytes=VMEM_LIMIT),
        name="ffn_dense",
    )(hf2, x2, p2, w_gu, w_down, ple_gain, wpg, wpp)


MOE_TM = 1024
MOE_TF = 512
DP_TM = 256
DP_CH = 256
CB_TB = 256
CB_ALIGN = 16
CB_W = CB_TB + CB_ALIGN


def _dispatch_kernel(clo_ref, chi_ref, pos_ref, gate_ref, hf_ref, x_ref, g_ref, acc_sc, gacc_sc):
    i = pl.program_id(0)
    rows = i * DP_TM + lax.broadcasted_iota(I32, (DP_TM, 1), 0)
    acc_sc[...] = jnp.zeros_like(acc_sc)
    gacc_sc[...] = jnp.zeros_like(gacc_sc)

    lo = clo_ref[i]
    hi = chi_ref[i]

    def chunk(c, live):
        start = pl.multiple_of(c * DP_CH, DP_CH)
        pos = pos_ref[:, pl.ds(start, DP_CH)]
        gate = gate_ref[:, pl.ds(start, DP_CH)]
        want = jnp.where(live, rows, -1)
        hit1 = pos[0:1, :] == want
        hit2 = pos[1:2, :] == want
        onehot = jnp.where(hit1, 1.0, jnp.where(hit2, 1.0, 0.0)).astype(BF16)
        gates = jnp.sum(jnp.where(hit1, gate[0:1, :], jnp.where(hit2, gate[1:2, :], 0.0)),
                        axis=-1, keepdims=True)
        return _dot(onehot, hf_ref[pl.ds(start, DP_CH), :]), gates

    def pair(t, carry):
        c = lo + 2 * t
        x0, g0 = chunk(c, True)
        x1, g1 = chunk(jnp.minimum(c + 1, hi), c + 1 <= hi)
        acc_sc[...] += x0 + x1
        gacc_sc[...] += g0 + g1
        return carry

    lax.fori_loop(0, (hi - lo + 2) >> 1, pair, 0)
    x_ref[...] = acc_sc[...].astype(BF16)
    g_ref[...] = gacc_sc[...]


def _dispatch(hf2, pos_t, gate_t, chunk_lo, chunk_hi):
    tok = hf2.shape[0]
    ns = chunk_lo.shape[0]
    grid_spec = pltpu.PrefetchScalarGridSpec(
        num_scalar_prefetch=2,
        grid=(ns,),
        in_specs=[
            pl.BlockSpec((2, tok), lambda i, lo, hi: (0, 0)),
            pl.BlockSpec((2, tok), lambda i, lo, hi: (0, 0)),
            pl.BlockSpec((tok, D_MODEL), lambda i, lo, hi: (0, 0), pipeline_mode=pl.Buffered(1)),
        ],
        out_specs=[pl.BlockSpec((DP_TM, D_MODEL), lambda i, lo, hi: (i, 0)),
                   pl.BlockSpec((DP_TM, 1), lambda i, lo, hi: (i, 0))],
        scratch_shapes=[pltpu.VMEM((DP_TM, D_MODEL), F32), pltpu.VMEM((DP_TM, 1), F32)],
    )
    return pl.pallas_call(
        _dispatch_kernel,
        grid_spec=grid_spec,
        out_shape=[jax.ShapeDtypeStruct((ns * DP_TM, D_MODEL), BF16),
                   jax.ShapeDtypeStruct((ns * DP_TM, 1), F32)],
        compiler_params=pltpu.CompilerParams(
            dimension_semantics=("arbitrary",), vmem_limit_bytes=VMEM_LIMIT),
        name="moe_dispatch",
    )(chunk_lo, chunk_hi, pos_t, gate_t, hf2)


def _moe_kernel(te_ref, tr_ref, x_ref, gate_ref, wg_ref, wu_ref, wd_ref, y_ref, acc_sc):
    i = pl.program_id(0)
    j = pl.program_id(1)
    last = pl.num_programs(1) - 1
    rows = tr_ref[i]

    def step(n):
        @pl.when(j == 0)
        def _():
            acc_sc[:n] = jnp.zeros((n, D_MODEL), F32)

        xb = x_ref[:n]
        g = _dot(xb, wg_ref[...].astype(BF16))
        u = _dot(xb, wu_ref[...].astype(BF16))
        acc_sc[:n] += _dot((g * _sigmoid(g) * u).astype(BF16), wd_ref[...].astype(BF16))

        @pl.when(j == last)
        def _():
            y_ref[:n] = (acc_sc[:n] * gate_ref[:n]).astype(BF16)
            if n < MOE_TM:
                y_ref[n:] = jnp.zeros((MOE_TM - n, D_MODEL), BF16)

    quarter = MOE_TM // 4
    for n in range(quarter, MOE_TM + 1, quarter):
        @pl.when((rows > n - quarter) & (rows <= n))
        def _(n=n):
            step(n)

    @pl.when((rows == 0) & (j == last))
    def _():
        y_ref[...] = jnp.zeros_like(y_ref)


def _moe(x_sorted, gate_sorted, w_gu, w_down, tile_expert, tile_rows):
    TM, TF = MOE_TM, MOE_TF
    nt = tile_expert.shape[0]
    nf = D_FF_EXPERT // TF

    def col(j, tv, i):
        return jnp.where(tv[i] > 0, j, nf - 1)

    grid_spec = pltpu.PrefetchScalarGridSpec(
        num_scalar_prefetch=2,
        grid=(nt, nf),
        in_specs=[
            pl.BlockSpec((TM, D_MODEL), lambda i, j, te, tv: (i, 0)),
            pl.BlockSpec((TM, 1), lambda i, j, te, tv: (i, 0)),
            pl.BlockSpec((None, D_MODEL, TF), lambda i, j, te, tv: (te[i], 0, col(j, tv, i))),
            pl.BlockSpec((None, D_MODEL, TF), lambda i, j, te, tv: (te[i], 0, col(j, tv, i) + nf)),
            pl.BlockSpec((None, TF, D_MODEL), lambda i, j, te, tv: (te[i], col(j, tv, i), 0)),
        ],
        out_specs=pl.BlockSpec((TM, D_MODEL), lambda i, j, te, tv: (i, 0)),
        scratch_shapes=[pltpu.VMEM((TM, D_MODEL), F32)],
    )
    return pl.pallas_call(
        _moe_kernel,
        grid_spec=grid_spec,
        out_shape=jax.ShapeDtypeStruct((nt * TM, D_MODEL), BF16),
        compiler_params=pltpu.CompilerParams(
            dimension_semantics=("arbitrary", "arbitrary"), vmem_limit_bytes=VMEM_LIMIT),
        name="moe_experts",
    )(tile_expert, tile_rows, x_sorted, gate_sorted, w_gu, w_gu, w_down)


def _combine_kernel(w0_ref, rng_ref, x_ref, p_ref, pos_ref, y_hbm, pgn_ref, wpg_ref, wpp_ref, out_ref,
                    win_sc, acc_sc, sem):
    i = pl.program_id(0)
    slot = i & 1

    def window_copy(step, buf, e):
        start = pl.multiple_of(w0_ref[step * N_EXPERTS + e], CB_ALIGN)
        return pltpu.make_async_copy(y_hbm.at[pl.ds(start, CB_W), :], win_sc.at[buf, e], sem.at[buf, e])

    @pl.when(i == 0)
    def _():
        for e in range(N_EXPERTS):
            window_copy(0, 0, e).start()

    @pl.when(i + 1 < pl.num_programs(0))
    def _():
        for e in range(N_EXPERTS):
            window_copy(i + 1, 1 - slot, e).start()

    pos = pos_ref[...]
    pos1 = pos[:, 0:1]
    pos2 = pos[:, 1:2]

    n_win = pl.num_programs(0) * N_EXPERTS

    def row_begin(e):
        return rng_ref[i * N_EXPERTS + e]

    def row_end(e):
        return rng_ref[n_win + i * N_EXPERTS + e]

    def onehot(e, first, width):
        rows = lax.broadcasted_iota(I32, (1, width), 1) + (first + w0_ref[i * N_EXPERTS + e])
        rows = jnp.where((rows >= row_begin(e)) & (rows < row_end(e)), rows, -1)
        return jnp.where(pos1 == rows, 1.0, jnp.where(pos2 == rows, 1.0, 0.0)).astype(BF16)

    moe_out = jnp.zeros((CB_TB, D_MODEL), F32)
    for e in range(N_EXPERTS):
        window_copy(i, slot, e).wait()
        moe_out = moe_out + _dot(onehot(e, 0, CB_TB), win_sc[slot, e, :CB_TB, :])
    acc_sc[...] = moe_out

    for e in range(N_EXPERTS):
        @pl.when(row_end(e) - w0_ref[i * N_EXPERTS + e] > CB_TB)
        def _():
            acc_sc[...] += _dot(onehot(e, CB_TB, CB_ALIGN), win_sc[slot, e, CB_TB:, :])

    out_ref[...] = _ple_epilogue(x_ref[...] + acc_sc[...], p_ref[...], pgn_ref[...], wpg_ref[...],
                                 wpp_ref[...])


def _combine(x2, p2, p_row0, pos, y_sorted, win_start, win_rows, layer, ple_gain, wpg, wpp):
    tok = x2.shape[0]
    TB = CB_TB
    grid_spec = pltpu.PrefetchScalarGridSpec(
        num_scalar_prefetch=2,
        grid=(tok // TB,),
        in_specs=[
            pl.BlockSpec((TB, D_MODEL), lambda i, w0, wt: (i, 0)),
            pl.BlockSpec((TB, PLE_DIM), lambda i, w0, wt: (i + p_row0 // TB, 0)),
            pl.BlockSpec((TB, 2), lambda i, w0, wt: (i, 0)),
            pl.BlockSpec(memory_space=pl.ANY),
            _layer_spec(layer, (1, D_MODEL)),
            _layer_spec(layer, (D_MODEL, D_MODEL)),
            _layer_spec(layer, (PLE_DIM, D_MODEL)),
        ],
        out_specs=pl.BlockSpec((TB, D_MODEL), lambda i, w0, wt: (i, 0)),
        scratch_shapes=[pltpu.VMEM((2, N_EXPERTS, CB_W, D_MODEL), BF16), pltpu.VMEM((TB, D_MODEL), F32),
                        pltpu.SemaphoreType.DMA((2, N_EXPERTS))],
    )
    return pl.pallas_call(
        _combine_kernel,
        grid_spec=grid_spec,
        out_shape=jax.ShapeDtypeStruct((tok, D_MODEL), F32),
        compiler_params=pltpu.CompilerParams(
            dimension_semantics=("arbitrary",), vmem_limit_bytes=VMEM_LIMIT),
        name="moe_combine",
    )(win_start, win_rows, x2, p2, pos, y_sorted, ple_gain, wpg, wpp)


def _route_metadata(route, tok):
    TM = MOE_TM
    nt = (2 * tok) // TM + N_EXPERTS
    rows = nt * TM
    g1, g2 = route[0], route[1]
    i1, i2 = route[2].astype(I32), route[3].astype(I32)
    experts = jnp.arange(N_EXPERTS, dtype=I32)
    member = ((i1[:, None] == experts) | (i2[:, None] == experts)).astype(I32)
    rank = jnp.cumsum(member, axis=0) - member
    counts = jnp.sum(member, axis=0)
    tiles_e = (counts + TM - 1) // TM
    tile_end = jnp.cumsum(tiles_e)
    tile_start = tile_end - tiles_e
    seg_start = tile_start * TM
    sel1 = i1[:, None] == experts
    sel2 = i2[:, None] == experts
    pos1 = jnp.sum(jnp.where(sel1, seg_start[None, :] + rank, 0), axis=1).astype(I32)
    pos2 = jnp.sum(jnp.where(sel2, seg_start[None, :] + rank, 0), axis=1).astype(I32)

    tiles = jnp.arange(nt, dtype=I32)
    n_used = tile_end[-1]
    tile_rows = (tiles < n_used).astype(I32)
    te = jnp.sum((tiles[:, None] >= tile_end[None, :]).astype(I32), axis=1)
    te_last = jnp.sum(((n_used - 1) >= tile_end).astype(I32))
    tile_expert = jnp.where(tile_rows > 0, jnp.minimum(te, N_EXPERTS - 1), te_last).astype(I32)
    mine = tile_expert[:, None] == experts[None, :]
    rows_left = (jnp.sum(jnp.where(mine, counts[None, :], 0), axis=1)
                 - (tiles - jnp.sum(jnp.where(mine, tile_start[None, :], 0), axis=1)) * TM)
    tile_rows = jnp.where(tile_rows > 0, jnp.clip(rows_left, 0, TM), 0).astype(I32)

    steps = jnp.arange(rows // DP_TM, dtype=I32)
    step_tile = steps // (TM // DP_TM)
    step_valid = step_tile < n_used
    step_expert = jnp.minimum(jnp.sum((step_tile[:, None] >= tile_end[None, :]).astype(I32), axis=1),
                              N_EXPERTS - 1)
    pick = step_expert[:, None] == experts[None, :]
    rank_lo = steps * DP_TM - jnp.sum(jnp.where(pick, seg_start[None, :], 0), axis=1)
    rank_hi = jnp.minimum(rank_lo + DP_TM, jnp.sum(jnp.where(pick, counts[None, :], 0), axis=1)) - 1
    before = jnp.sum(jnp.where(pick[:, :, None], rank[::DP_CH, :].T[None, :, :], 0), axis=1)
    chunk_lo = jnp.sum((before <= rank_lo[:, None]).astype(I32), axis=1) - 1
    chunk_hi = jnp.sum((before <= rank_hi[:, None]).astype(I32), axis=1) - 1
    live = step_valid & (rank_hi >= rank_lo)
    chunk_lo = jnp.where(live, chunk_lo, 0).astype(I32)
    chunk_hi = jnp.where(live, chunk_hi, -1).astype(I32)

    blk_start = seg_start[None, :] + rank[::CB_TB, :]
    win_start = jnp.minimum((blk_start // CB_ALIGN) * CB_ALIGN, rows - CB_W).astype(I32)
    blk_count = jnp.concatenate([rank[CB_TB::CB_TB, :], counts[None, :]], axis=0) - rank[::CB_TB, :]
    win_rows = jnp.concatenate([blk_start.reshape(-1), (blk_start + blk_count).reshape(-1)]).astype(I32)
    win_start = win_start.reshape(-1)
    pos = jnp.stack([pos1, pos2], axis=1)
    return (tile_expert, tile_rows, chunk_lo, chunk_hi, jnp.stack([pos1, pos2]), jnp.stack([g1, g2]),
            pos, win_start, win_rows)


def _pack_w_in(w):
    o = [0, 1024, 1536, 2048, 2052, 2056, 2568, 3080, 3592, 4104, 7176]
    w = w.astype(BF16)
    main = jnp.concatenate([w[..., o[0]:o[3]], w[..., o[5]:o[10]]], axis=-1)
    gates = jnp.pad(w[..., o[3]:o[5]], ((0, 0), (0, 0), (0, LANES - 2 * M_HEADS)))
    return main, gates


def kernel(x, p, attn_norm, w_in, m_conv_w, m_conv_b, m_gate_bias, m_head_norm, a_q_norm, a_k_norm,
           a_lambda, a_head_norm, pool_w, pool_scale, w_branch, w_out, ffn_norm, dense_w_gu, dense_w_down,
           router_w, router_b, moe_w_gu, moe_w_down, ple_norm, ple_w_gate, ple_w_proj):
    b, s, d = x.shape
    depth = w_in.shape[0]
    tok = b * s
    x2 = x.reshape(tok, d)

    def rows(a):
        return a.reshape(a.shape[0], 1, a.shape[1])

    w_main, w_if = _pack_w_in(w_in)
    gate_bias = rows(jnp.pad(m_gate_bias, ((0, 0), (0, LANES - 2 * M_HEADS))))
    q_gain = rows(jnp.tile(a_q_norm, (1, 2)))
    k_gain = rows(jnp.tile(a_k_norm, (1, 2)))
    p2 = p.reshape(depth * tok, PLE_DIM)
    pool_wb, w_branch_b, w_out_b = pool_w.astype(BF16), w_branch.astype(BF16), w_out.astype(BF16)
    ple = (rows(ple_norm), ple_w_gate.astype(BF16), ple_w_proj.astype(BF16))
    dense_gu, dense_down = dense_w_gu.astype(BF16), dense_w_down.astype(BF16)
    router_wt = jnp.pad(jnp.swapaxes(router_w, 1, 2), ((0, 0), (0, ROUTER_ROWS - N_EXPERTS), (0, 0)))
    router_bc = router_b.reshape(router_b.shape[0], N_EXPERTS, 1)

    for layer in range(depth):
        proj, gates = _inproj(x2, layer, rows(attn_norm), w_main, w_if)
        proj3 = proj.reshape(b, s, PROJ_WIDTH)
        h_m = _mlstm(proj3, gates.reshape(b, s, LANES), layer, m_conv_w, rows(m_conv_b), gate_bias,
                     rows(m_head_norm))
        lam_init = 0.8 - 0.6 * math.exp(-0.3 * layer)
        h_a = _attention(proj3, layer, q_gain, k_gain, a_lambda, rows(a_head_norm), lam_init)
        merge_args = (x2, h_m.reshape(tok, BRANCH_WIDTH), h_a.reshape(tok, BRANCH_WIDTH), proj, layer,
                      pool_wb, rows(pool_scale), w_branch_b, w_out_b, rows(ffn_norm), s)
        j = layer // 2
        if layer % 2 == 0:
            x_mid, hf = _merge(*merge_args)
            x2 = _ffn(hf, x_mid, p2, layer * tok, j, dense_gu, dense_down, layer, *ple)
        else:
            x_mid, hf, route = _merge(*merge_args, router=(j, router_wt, router_bc))
            (tile_expert, tile_rows, chunk_lo, chunk_hi, pos_t, gate_t, pos,
             win_start, win_rows) = _route_metadata(route, tok)
            x_sorted, gate_sorted = _dispatch(hf, pos_t, gate_t, chunk_lo, chunk_hi)
            y_sorted = _moe(x_sorted, gate_sorted, moe_w_gu[j], moe_w_down[j], tile_expert, tile_rows)
            x2 = _combine(x_mid, p2, layer * tok, pos, y_sorted, win_start, win_rows, layer, *ple)
    return x2.reshape(b, s, d)
```
